```python
import jax
import jax.numpy as jnp
from jax import lax
import numpy as np

D_MODEL = 1024
BATCH = 4
SEQ = 8192
DEPTH = 2

GRID_W = 64
CTX_LEN = 256
N_EVEN = (DEPTH + 1) // 2
N_ODD = DEPTH // 2
NORM_EPS = 1e-6

RW_HEAD = 64
RW_DIM = D_MODEL // 2
RW_HEADS = RW_DIM // RW_HEAD
RW_W_LORA = 64
RW_A_LORA = 64
RW_G_LORA = 128
RW_COLS = 3 * RW_DIM + RW_W_LORA + RW_A_LORA + RW_G_LORA
RW_GN_EPS = 64e-5

NA_HEAD = 64
NA_DIM = D_MODEL - RW_DIM
NA_HEADS = NA_DIM // NA_HEAD
NA_ROWS = 8
NA_COLS = 16
NA_SCALE = NA_HEAD ** -0.5
EVEN_COLS = RW_COLS + 3 * NA_DIM

HG_KEY = 128
HG_HEADS = D_MODEL // HG_KEY
HG_KDIM = HG_HEADS * HG_KEY
HG_VAL = D_MODEL // HG_HEADS
HG_VDIM = HG_HEADS * HG_VAL
HG_CHUNK = 64
ODD_COLS = 3 * HG_KDIM + 2 * HG_VDIM

MLP_HIDDEN = 4 * D_MODEL

kernel_name = "hybrid_rwkv7_natten_hgrn2_prefix_dit"


def rmsnorm(x, g, eps=NORM_EPS):
    xf = x.astype(jnp.float32)
    y = xf * lax.rsqrt(jnp.mean(xf * xf, axis=-1, keepdims=True) + eps)
    return (y * g.astype(jnp.float32)).astype(x.dtype)


def modulate(h, shift, scale):
    return h * (1 + scale) + shift


def sq_relu_mlp(h, w1, w2):
    return jnp.square(jax.nn.relu(h @ w1)) @ w2


def bidir_token_shift(p, mu_prev, mu_next):
    zero = jnp.zeros_like(p[:, :1])
    prev = jnp.concatenate([zero, p[:, :-1]], axis=1)
    nxt = jnp.concatenate([p[:, 1:], zero], axis=1)
    return p + (prev - p) * mu_prev + (nxt - p) * mu_next


def _scan_order(z):
    z = jnp.stack([z[0], jnp.flip(z[1], axis=1)])
    return jnp.moveaxis(z, 2, 0)


def _rwkv7_step(S, inp):
    r, w, k, v, kk, a = inp
    s_kk = jnp.einsum('dbhvk,dbhk->dbhv', S, kk)
    S = S * w[..., None, :] - s_kk[..., None] * (kk * a)[..., None, :] + v[..., None] * k[..., None, :]
    return S, jnp.einsum('dbhvk,dbhk->dbhv', S, r)


def head_group_norm(o, g, b):
    of = o.astype(jnp.float32)
    mu = jnp.mean(of, axis=-1, keepdims=True)
    var = jnp.mean(jnp.square(of - mu), axis=-1, keepdims=True)
    y = (of - mu) * lax.rsqrt(var + RW_GN_EPS)
    B, T = o.shape[:2]
    return y.reshape(B, T, -1) * g + b


def rwkv7_mix(p, S0, mu_prev, mu_next, w0, w_up, a0, a_up, g_up, k_k, k_a, r_k, ln_g, ln_b):
    B, T, _ = p.shape
    p = bidir_token_shift(p, mu_prev, mu_next)
    r = p[..., :RW_DIM]
    k = p[..., RW_DIM:2 * RW_DIM]
    v = p[..., 2 * RW_DIM:3 * RW_DIM]
    off = 3 * RW_DIM
    w_lo = p[..., off:off + RW_W_LORA]
    a_lo = p[..., off + RW_W_LORA:off + RW_W_LORA + RW_A_LORA]
    g_lo = p[..., off + RW_W_LORA + RW_A_LORA:]
    w_log = -jax.nn.softplus(-(w0[:, None, None, :] + jnp.einsum('btl,dlc->dbtc', jnp.tanh(w_lo), w_up))) - 0.5
    decay = jnp.exp(-jnp.exp(w_log.astype(jnp.float32)))
    a = jax.nn.sigmoid(a0[:, None, None, :] + jnp.einsum('btl,dlc->dbtc', a_lo, a_up))
    g = jax.nn.sigmoid(g_lo) @ g_up
    heads = lambda z: z.reshape(z.shape[:-1] + (RW_HEADS, RW_HEAD))
    both = lambda z: jnp.stack([z, z])
    kk = heads(k * k_k)
    kk = kk / jnp.maximum(jnp.linalg.norm(kk, axis=-1, keepdims=True), 1e-12)
    k_d = heads(k[None] * (1 + (a - 1) * k_a))
    r_h, v_h = heads(r), heads(v)
    xs = (_scan_order(both(r_h)), _scan_order(heads(decay)), _scan_order(k_d),
          _scan_order(both(v_h)), _scan_order(both(kk)), _scan_order(heads(a)))
    S_fin, o = lax.scan(_rwkv7_step, S0, xs)
    o = jnp.moveaxis(o[:, 0] + jnp.flip(o[:, 1], axis=0), 0, 1)
    o = head_group_norm(o, ln_g, ln_b)
    bonus = jnp.sum(r_h[None] * k_d * r_k, axis=(0, -1))[..., None] * v_h
    out = (o + bonus.reshape(B, T, RW_DIM)) * g
    return out.astype(p.dtype), S_fin


def neighbourhood_attention(q, k, v, k_ctx, v_ctx, rpb):
    B, T, H, dh = q.shape
    rows = T // GRID_W
    kr = min(NA_ROWS, rows)
    q_g = q.reshape(B, rows, GRID_W, H, dh).transpose(1, 0, 3, 2, 4)
    k_g = k.reshape(B, rows, GRID_W, H, dh).transpose(0, 3, 1, 2, 4)
    v_g = v.reshape(B, rows, GRID_W, H, dh).transpose(0, 3, 1, 2, 4)
    kc = k_ctx.transpose(0, 2, 1, 3)
    vc = v_ctx.transpose(0, 2, 1, 3)
    cols = jnp.arange(GRID_W)
    col_start = jnp.clip(cols - NA_COLS // 2, 0, GRID_W - NA_COLS)
    col_idx = col_start[:, None] + jnp.arange(NA_COLS)
    col_rel = col_idx - cols[:, None] + NA_COLS - 1
    n_win = kr * NA_COLS

    def one_row(args):
        r, q_r = args
        r0 = jnp.clip(r - kr // 2, 0, rows - kr)
        row_rel = r0 + jnp.arange(kr) - r + NA_ROWS - 1
        bias = rpb[:, row_rel[:, None, None], col_rel[None, :, :]]
        k_win = lax.dynamic_slice_in_dim(k_g, r0, kr, axis=2)[:, :, :, col_idx]
        v_win = lax.dynamic_slice_in_dim(v_g, r0, kr, axis=2)[:, :, :, col_idx]
        s_win = jnp.einsum('bhqd,bhrqcd->bhqrc', q_r, k_win) * NA_SCALE + bias.transpose(0, 2, 1, 3)[None]
        s_ctx = jnp.einsum('bhqd,bhkd->bhqk', q_r, kc) * NA_SCALE
        s = jnp.concatenate([s_win.reshape(B, H, GRID_W, n_win), s_ctx], axis=-1).astype(jnp.float32)
        pr = jax.nn.softmax(s, axis=-1).astype(v.dtype)
        p_win = pr[..., :n_win].reshape(B, H, GRID_W, kr, NA_COLS)
        return (jnp.einsum('bhqrc,bhrqcd->bhqd', p_win, v_win)
                + jnp.einsum('bhqk,bhkd->bhqd', pr[..., n_win:], vc))

    o = lax.map(one_row, (jnp.arange(rows), q_g))
    return o.transpose(1, 0, 3, 2, 4).reshape(B, T, H * dh)


def context_attention(q, k, v):
    B, L, H, dh = q.shape
    s = jnp.einsum('bqhd,bkhd->bhqk', q, k).astype(jnp.float32) * NA_SCALE
    pr = jax.nn.softmax(s, axis=-1).astype(v.dtype)
    return jnp.einsum('bhqk,bkhd->bqhd', pr, v).reshape(B, L, H * dh)


def hgrn2_inputs(p, lb):
    B, T, _ = p.shape
    q = jax.nn.silu(p[..., :HG_KDIM])
    f = p[..., HG_KDIM:3 * HG_KDIM].reshape(B, T, 2, HG_KDIM).astype(jnp.float32)
    i = p[..., 3 * HG_KDIM:3 * HG_KDIM + HG_VDIM]
    gate = p[..., 3 * HG_KDIM + HG_VDIM:]
    forget = lb + (1 - lb) * jax.nn.sigmoid(f)
    log_f = jnp.log(forget)
    k = 1 - forget
    bh = lambda z, d: z.reshape(B, T, HG_HEADS, d).transpose(0, 2, 1, 3)
    dirs = lambda zf, zb: jnp.stack([zf, jnp.flip(zb, axis=2)])
    qh, ih = bh(q, HG_KEY), bh(i, HG_VAL)
    return (dirs(qh, qh), dirs(bh(k[:, :, 0], HG_KEY), bh(k[:, :, 1], HG_KEY)), dirs(ih, ih),
            dirs(bh(log_f[:, :, 0], HG_KEY), bh(log_f[:, :, 1], HG_KEY)), gate)


def _hgrn2_chunk_step(S, inp):
    qc, kc, vc, gc = inp
    b = jnp.cumsum(gc, axis=-2)
    inter = jnp.einsum('dbhck,dbhkv->dbhcv', qc * jnp.exp(b), S)
    lower_tri = jnp.tril(jnp.ones((HG_CHUNK, HG_CHUNK), dtype=bool))[:, :, None]
    dec = jnp.exp(jnp.where(lower_tri, b[..., :, None, :] - b[..., None, :, :], -jnp.inf))
    att = jnp.einsum('dbhtk,dbhsk,dbhtsk->dbhts', qc, kc, dec)
    o = inter + jnp.einsum('dbhts,dbhsv->dbhtv', att, vc)
    b_last = b[..., -1:, :]
    S = jnp.exp(b_last[..., 0, :])[..., None] * S + jnp.einsum('dbhsk,dbhsv->dbhkv', kc * jnp.exp(b_last - b), vc)
    return S, o


def hgrn2_chunk_scan(S0, q, k, v, g):
    T = q.shape[3]
    nc = T // HG_CHUNK
    chunks = lambda z: jnp.moveaxis(z.reshape(z.shape[:3] + (nc, HG_CHUNK, z.shape[-1])), 3, 0)
    S_fin, o = lax.scan(_hgrn2_chunk_step, S0, (chunks(q), chunks(k), chunks(v), chunks(g)))
    o = jnp.moveaxis(o, 0, 3)
    return o.reshape(o.shape[:3] + (T, o.shape[-1])), S_fin


def hgrn2_final_state(k, v, g):
    b = jnp.cumsum(g, axis=-2)
    return jnp.einsum('dbhtk,dbhtv->dbhkv', k * jnp.exp(b[..., -1:, :] - b), v)


def hgrn2_readout(o, gate, norm_g):
    o = o[0] + jnp.flip(o[1], axis=2)
    B, H, T, V = o.shape
    o = o.transpose(0, 2, 1, 3).reshape(B, T, H * V)
    return (rmsnorm(o, norm_g) * jax.nn.silu(gate)).astype(gate.dtype)


def even_mixer(a_lat, a_ctx, last, w_in, w_out, mu_prev, mu_next, w0, w_up, a0, a_up, g_up,
               k_k, k_a, r_k, ln_g, ln_b, rpb):
    B = a_lat.shape[0]
    p_lat = a_lat @ w_in
    p_ctx = a_ctx @ w_in
    rw = (mu_prev, mu_next, w0, w_up, a0, a_up, g_up, k_k, k_a, r_k, ln_g, ln_b)
    S0 = jnp.zeros((2, B, RW_HEADS, RW_HEAD, RW_HEAD), jnp.float32)
    rw_ctx, S_ctx = rwkv7_mix(p_ctx[..., :RW_COLS], S0, *rw)
    rw_lat, _ = rwkv7_mix(p_lat[..., :RW_COLS], S_ctx, *rw)

    def na_heads(p):
        z = p[..., RW_COLS:]
        sh = z.shape[:2] + (NA_HEADS, NA_HEAD)
        return (z[..., :NA_DIM].reshape(sh), z[..., NA_DIM:2 * NA_DIM].reshape(sh),
                z[..., 2 * NA_DIM:].reshape(sh))

    qc, kc, vc = na_heads(p_ctx)
    ql, kl, vl = na_heads(p_lat)
    na_lat = neighbourhood_attention(ql, kl, vl, kc, vc, rpb)
    y_lat = jnp.concatenate([rw_lat, na_lat], axis=-1) @ w_out
    if last:
        return y_lat, None
    na_ctx = context_attention(qc, kc, vc)
    y_ctx = jnp.concatenate([rw_ctx, na_ctx], axis=-1) @ w_out
    return y_lat, y_ctx


def odd_mixer(a_lat, a_ctx, last, w_in, w_out, lb, norm_g):
    ql, kl, vl, gl, gate_l = hgrn2_inputs(a_lat @ w_in, lb)
    qc, kc, vc, gc, gate_c = hgrn2_inputs(a_ctx @ w_in, lb)
    if last:
        S_ctx = hgrn2_final_state(kc, vc, gc)
    else:
        S0 = jnp.zeros(kc.shape[:3] + (HG_KEY, HG_VAL), jnp.float32)
        o_ctx, S_ctx = hgrn2_chunk_scan(S0, qc, kc, vc, gc)
    o_lat, _ = hgrn2_chunk_scan(S_ctx, ql, kl, vl, gl)
    y_lat = hgrn2_readout(o_lat, gate_l, norm_g) @ w_out
    if last:
        return y_lat, None
    return y_lat, hgrn2_readout(o_ctx, gate_c, norm_g) @ w_out


def setup_inputs(seed: int = 0) -> dict:
    key = jax.random.key(seed)
    ks = iter(jax.random.split(key, 40))
    f32 = jnp.float32
    D = D_MODEL

    def nrm(shape, std):
        return std * jax.random.normal(next(ks), shape, f32)

    def gain(shape, s=0.05):
        return 1.0 + s * jax.random.normal(next(ks), shape, f32)

    return {
        "x": nrm((BATCH, SEQ, D), 1.0),
        "c": nrm((BATCH, D), 1.0),
        "ctx": nrm((BATCH, CTX_LEN, D), 1.0),
        "c_ctx": nrm((D,), 1.0),
        "norm_mix_g": gain((DEPTH, D)),
        "norm_mlp_g": gain((DEPTH, D)),
        "ada_w": nrm((DEPTH, D, 6 * D), 0.5 * D ** -0.5),
        "ada_b": nrm((DEPTH, 6 * D), 0.02),
        "mlp_w1": nrm((DEPTH, D, MLP_HIDDEN), D ** -0.5),
        "mlp_w2": nrm((DEPTH, MLP_HIDDEN, D), MLP_HIDDEN ** -0.5),
        "ev_w_in": nrm((N_EVEN, D, EVEN_COLS), D ** -0.5),
        "ev_w_out": nrm((N_EVEN, RW_DIM + NA_DIM, D), (RW_DIM + NA_DIM) ** -0.5),
        "rw_mu_prev": jax.random.uniform(next(ks), (N_EVEN, RW_COLS), f32, 0.0, 0.5),
        "rw_mu_next": jax.random.uniform(next(ks), (N_EVEN, RW_COLS), f32, 0.0, 0.5),
        "rw_w0": jax.random.uniform(next(ks), (N_EVEN, 2, RW_DIM), f32, -3.0, 1.0),
        "rw_w_up": nrm((N_EVEN, 2, RW_W_LORA, RW_DIM), 0.1),
        "rw_a0": nrm((N_EVEN, 2, RW_DIM), 0.1),
        "rw_a_up": nrm((N_EVEN, 2, RW_A_LORA, RW_DIM), 0.5 * RW_A_LORA ** -0.5),
        "rw_g_up": nrm((N_EVEN, RW_G_LORA, RW_DIM), RW_G_LORA ** -0.5),
        "rw_k_k": 0.85 + nrm((N_EVEN, RW_DIM), 0.05),
        "rw_k_a": gain((N_EVEN, RW_DIM)),
        "rw_r_k": nrm((N_EVEN, RW_HEADS, RW_HEAD), 0.1),
        "rw_ln_g": gain((N_EVEN, RW_DIM)),
        "rw_ln_b": nrm((N_EVEN, RW_DIM), 0.02),
        "na_rpb": nrm((N_EVEN, NA_HEADS, 2 * NA_ROWS - 1, 2 * NA_COLS - 1), 0.1),
        "od_w_in": nrm((N_ODD, D, ODD_COLS), D ** -0.5),
        "od_w_out": nrm((N_ODD, HG_VDIM, D), HG_VDIM ** -0.5),
        "hg_lower": gain((DEPTH, HG_KDIM), 0.1),
        "hg_norm_g": gain((N_ODD, HG_VDIM)),
        "final_norm_g": gain((D,)),
    }


def reference(x, c, ctx, c_ctx, norm_mix_g, norm_mlp_g, ada_w, ada_b, mlp_w1, mlp_w2,
              ev_w_in, ev_w_out, rw_mu_prev, rw_mu_next, rw_w0, rw_w_up, rw_a0, rw_a_up, rw_g_up,
              rw_k_k, rw_k_a, rw_r_k, rw_ln_g, rw_ln_b, na_rpb, od_w_in, od_w_out, hg_lower,
              hg_norm_g, final_norm_g):
    sc = jax.nn.silu(c)
    scc = jax.nn.silu(c_ctx)
    lb_sm = jax.nn.softmax(hg_lower.astype(jnp.float32), axis=0)
    lb_all = jnp.cumsum(lb_sm, axis=0) - lb_sm[0]
    h_lat, h_ctx = x, ctx
    for l in range(DEPTH):
        last = l == DEPTH - 1
        m = sc @ ada_w[l] + ada_b[l]
        mc = scc @ ada_w[l] + ada_b[l]
        sh1, s1, g1, sh2, s2, g2 = jnp.split(m[:, None, :], 6, axis=-1)
        csh1, cs1, cg1, csh2, cs2, cg2 = jnp.split(mc, 6)
        a_lat = modulate(rmsnorm(h_lat, norm_mix_g[l]), sh1, s1)
        a_ctx = modulate(rmsnorm(h_ctx, norm_mix_g[l]), csh1, cs1)
        if l % 2 == 0:
            e = l // 2
            y_lat, y_ctx = even_mixer(a_lat, a_ctx, last, ev_w_in[e], ev_w_out[e], rw_mu_prev[e],
                                      rw_mu_next[e], rw_w0[e], rw_w_up[e], rw_a0[e], rw_a_up[e],
                                      rw_g_up[e], rw_k_k[e], rw_k_a[e], rw_r_k[e], rw_ln_g[e],
                                      rw_ln_b[e], na_rpb[e])
        else:
            o = l // 2
            y_lat, y_ctx = odd_mixer(a_lat, a_ctx, last, od_w_in[o], od_w_out[o], lb_all[l], hg_norm_g[o])
        h_lat = h_lat + g1 * y_lat
        h_lat = h_lat + g2 * sq_relu_mlp(modulate(rmsnorm(h_lat, norm_mlp_g[l]), sh2, s2), mlp_w1[l], mlp_w2[l])
        if not last:
            h_ctx = h_ctx + cg1 * y_ctx
            h_ctx = h_ctx + cg2 * sq_relu_mlp(modulate(rmsnorm(h_ctx, norm_mlp_g[l]), csh2, cs2), mlp_w1[l], mlp_w2[l])
    return rmsnorm(h_lat, final_norm_g)
```

```python
import functools

import jax
import jax.numpy as jnp
from jax import lax
from jax.experimental import pallas as pl
from jax.experimental.pallas import tpu as pltpu

F32 = jnp.float32
BF16 = jnp.bfloat16
HIGHEST = lax.Precision.HIGHEST

NORM_EPS = 1e-6
RW_GN_EPS = 64e-5
HEAD = 64
GRID_W = 64
NA_ROWS = 8
NA_COLS = 16
HG_KEY = 128
CHUNK = 64
MASK_BIAS = -1e30
SUBLANES = 8
VMEM_LIMIT = 52 * 1024 * 1024


def _params(*sem):
    return pltpu.CompilerParams(dimension_semantics=sem, vmem_limit_bytes=VMEM_LIMIT)


def _dot(a, b):
    return jnp.dot(a, b, preferred_element_type=F32)


def _dot_nt(a, b):
    return lax.dot_general(a, b, (((1,), (1,)), ((), ())), preferred_element_type=F32)


def _dot_tn(a, b):
    return lax.dot_general(a, b, (((0,), (0,)), ((), ())), preferred_element_type=F32)


def _split2(x):
    hi = x.astype(BF16)
    lo = (x - hi.astype(F32)).astype(BF16)
    return hi, lo


def _split3(x):
    x0 = x.astype(BF16)
    r = x - x0.astype(F32)
    x1 = r.astype(BF16)
    x2 = (r - x1.astype(F32)).astype(BF16)
    return x0, x1, x2


def _mm3(dot, a, b):
    a0, a1 = _split2(a)
    b0, b1 = _split2(b)
    return dot(a0, b0) + (dot(a0, b1) + dot(a1, b0))


def _dot3(a, b):
    return _mm3(_dot, a, b)


def _dot3_nt(a, b):
    return _mm3(_dot_nt, a, b)


def _dot3_tn(a, b):
    return _mm3(_dot_tn, a, b)


def _dot_sel_r(x, sel):
    x0, x1, x2 = _split3(x)
    return _dot(x0, sel) + (_dot(x1, sel) + _dot(x2, sel))


def _dot_sel_l(sel, x):
    x0, x1, x2 = _split3(x)
    return _dot(sel, x0) + (_dot(sel, x1) + _dot(sel, x2))


def _sigmoid(x):
    return 1.0 / (1.0 + jnp.exp(-x))


def _silu(x):
    return x * _sigmoid(x)


def _softplus(x):
    return jnp.maximum(x, 0.0) + jnp.log(1.0 + jnp.exp(-jnp.abs(x)))


def _rms(x, g):
    return x * lax.rsqrt(jnp.mean(x * x, axis=-1, keepdims=True) + NORM_EPS) * g


def _order_masks(n, reverse):
    t = lax.broadcasted_iota(jnp.int32, (n, n), 0)
    i = lax.broadcasted_iota(jnp.int32, (n, n), 1)
    if reverse:
        return i >= t, i > t
    return i <= t, i < t


def _ada_kernel(c_ref, w_ref, b_ref, o_ref):
    s = _silu(c_ref[...])
    o_ref[0] = jnp.dot(s, w_ref[0], precision=HIGHEST, preferred_element_type=F32) + b_ref[0]


def _ada(c_rows, ada_w, ada_b):
    depth, d, n = ada_w.shape
    rows = c_rows.shape[0]
    tn = 512
    return pl.pallas_call(
        _ada_kernel,
        grid=(depth, n // tn),
        in_specs=[pl.BlockSpec((rows, d), lambda l, j: (0, 0)),
                  pl.BlockSpec((1, d, tn), lambda l, j: (l, 0, j)),
                  pl.BlockSpec((1, 1, tn), lambda l, j: (l, 0, j))],
        out_specs=pl.BlockSpec((1, rows, tn), lambda l, j: (l, 0, j)),
        out_shape=jax.ShapeDtypeStruct((depth, rows, n), F32),
        compiler_params=_params("parallel", "parallel"),
        name="ada",
    )(c_rows, ada_w, ada_b.reshape(depth, 1, n))


def _nmm_kernel(x_ref, g_ref, sh_ref, sc_ref, w_ref, o_ref, a_scr):
    @pl.when(pl.program_id(2) == 0)
    def _():
        a = _rms(x_ref[0], g_ref[...]) * (1.0 + sc_ref[0]) + sh_ref[0]
        a_scr[...] = a.astype(BF16)

    o_ref[0] = _dot(a_scr[...], w_ref[...])


def _norm_mod_matmul(x, g, shift, scale, w, tm, tn):
    b, t, d = x.shape
    n = w.shape[1]
    return pl.pallas_call(
        _nmm_kernel,
        grid=(b, t // tm, n // tn),
        in_specs=[pl.BlockSpec((1, tm, d), lambda bi, i, j: (bi, i, 0)),
                  pl.BlockSpec((1, d), lambda bi, i, j: (0, 0)),
                  pl.BlockSpec((1, 1, d), lambda bi, i, j: (bi, 0, 0)),
                  pl.BlockSpec((1, 1, d), lambda bi, i, j: (bi, 0, 0)),
                  pl.BlockSpec((d, tn), lambda bi, i, j: (0, j))],
        out_specs=pl.BlockSpec((1, tm, tn), lambda bi, i, j: (bi, i, j)),
        out_shape=jax.ShapeDtypeStruct((b, t, n), F32),
        scratch_shapes=[pltpu.VMEM((tm, d), BF16)],
        compiler_params=_params("parallel", "parallel", "arbitrary"),
        name="norm_mod_matmul",
    )(x, g.reshape(1, d), shift, scale, w)


def _mlp_kernel(h_ref, g_ref, sh_ref, sc_ref, gate_ref, w1_ref, w2_ref, fg_ref, o_ref,
                a_scr, acc_scr, *, final_norm):
    j = pl.program_id(2)

    @pl.when(j == 0)
    def _():
        a = _rms(h_ref[0], g_ref[...]) * (1.0 + sc_ref[0]) + sh_ref[0]
        a_scr[...] = a.astype(BF16)
        acc_scr[...] = jnp.zeros_like(acc_scr)

    hid = jnp.maximum(_dot(a_scr[...], w1_ref[...]), 0.0)
    acc_scr[...] += _dot((hid * hid).astype(BF16), w2_ref[...])

    @pl.when(j == pl.num_programs(2) - 1)
    def _():
        out = h_ref[0] + gate_ref[0] * acc_scr[...]
        if final_norm:
            out = _rms(out, fg_ref[...])
        o_ref[0] = out


def _mlp(h, g, shift, scale, gate, w1, w2, final_g, tm, th, final_norm):
    b, t, d = h.shape
    hid = w1.shape[1]
    vec = pl.BlockSpec((1, 1, d), lambda bi, i, j: (bi, 0, 0))
    par = pl.BlockSpec((1, d), lambda bi, i, j: (0, 0))
    return pl.pallas_call(
        functools.partial(_mlp_kernel, final_norm=final_norm),
        grid=(b, t // tm, hid // th),
        in_specs=[pl.BlockSpec((1, tm, d), lambda bi, i, j: (bi, i, 0)),
                  par, vec, vec, vec,
                  pl.BlockSpec((d, th), lambda bi, i, j: (0, j)),
                  pl.BlockSpec((th, d), lambda bi, i, j: (j, 0)),
                  par],
        out_specs=pl.BlockSpec((1, tm, d), lambda bi, i, j: (bi, i, 0)),
        out_shape=jax.ShapeDtypeStruct((b, t, d), F32),
        scratch_shapes=[pltpu.VMEM((tm, d), BF16), pltpu.VMEM((tm, d), F32)],
        compiler_params=_params("parallel", "parallel", "arbitrary"),
        name="mlp",
    )(h, g.reshape(1, d), shift, scale, gate, w1, w2, final_g.reshape(1, d))


def _rwkv_prep_kernel(p_ref, pp_ref, pn_ref, mup_ref, mun_ref, w0_ref, wup_ref, a0_ref, aup_ref,
                      gup_ref, kk_ref, ka_ref, rk_ref, bd_ref,
                      r_o, v_o, kkn_o, lw0_o, lw1_o, kd0_o, kd1_o, b0_o, b1_o, g_o, bonus_o,
                      *, rw_dim, w_lora, a_lora):
    i = pl.program_id(1)
    n = pl.num_programs(1)
    p = p_ref[0]
    tm = p.shape[0]
    row = lax.broadcasted_iota(jnp.int32, (tm, 1), 0)
    first = jnp.where(i > 0, pp_ref[0][SUBLANES - 1:SUBLANES, :], 0.0)
    last = jnp.where(i < n - 1, pn_ref[0][0:1, :], 0.0)
    prev = jnp.where(row == 0, first, pltpu.roll(p, 1, axis=0))
    nxt = jnp.where(row == tm - 1, last, pltpu.roll(p, tm - 1, axis=0))
    ps = p + (prev - p) * mup_ref[...] + (nxt - p) * mun_ref[...]

    r = ps[:, :rw_dim]
    k = ps[:, rw_dim:2 * rw_dim]
    v = ps[:, 2 * rw_dim:3 * rw_dim]
    off = 3 * rw_dim
    w_lo = ps[:, off:off + w_lora]
    a_lo = ps[:, off + w_lora:off + w_lora + a_lora]
    g_lo = ps[:, off + w_lora + a_lora:]

    w_pre = w0_ref[...] + _dot3(jnp.tanh(w_lo), wup_ref[...])
    w_log = -_softplus(-w_pre) - 0.5
    log_decay = -jnp.exp(w_log)
    a = _sigmoid(a0_ref[...] + _dot3(a_lo, aup_ref[...]))
    g = _dot3(_sigmoid(g_lo), gup_ref[...])

    bd = bd_ref[...]
    kk = k * kk_ref[...]
    kk = kk / jnp.maximum(jnp.sqrt(_dot_sel_r(kk * kk, bd)), 1e-12)
    a_f = a[:, :rw_dim]
    a_b = a[:, rw_dim:]
    ka = ka_ref[...]
    kd_f = k * (1.0 + (a_f - 1.0) * ka)
    kd_b = k * (1.0 + (a_b - 1.0) * ka)
    bonus = _dot_sel_r(r * (kd_f + kd_b) * rk_ref[...], bd) * v

    r_o[0] = r
    v_o[0] = v
    kkn_o[0] = kk
    lw0_o[0] = log_decay[:, :rw_dim]
    lw1_o[0] = log_decay[:, rw_dim:]
    kd0_o[0] = kd_f
    kd1_o[0] = kd_b
    b0_o[0] = kk * a_f
    b1_o[0] = kk * a_b
    g_o[0] = g
    bonus_o[0] = bonus


def _rwkv_prep(p, prm, tm):
    b, t, _ = p.shape
    rw_dim = prm["k_k"].shape[1]
    w_lora = prm["w_up"].shape[0]
    a_lora = prm["a_up"].shape[0]
    g_lora = prm["g_up"].shape[0]
    cols = 3 * rw_dim + w_lora + a_lora + g_lora
    nb8 = t // 8
    tb8 = tm // 8

    def full(a):
        return pl.BlockSpec(a.shape, lambda bi, i: (0,) * a.ndim)

    consts = [prm["mu_prev"], prm["mu_next"], prm["w0"], prm["w_up"], prm["a0"], prm["a_up"],
              prm["g_up"], prm["k_k"], prm["k_a"], prm["r_k"], prm["bd"]]
    out_spec = pl.BlockSpec((1, tm, rw_dim), lambda bi, i: (bi, i, 0))
    out_sds = jax.ShapeDtypeStruct((b, t, rw_dim), F32)
    return pl.pallas_call(
        functools.partial(_rwkv_prep_kernel, rw_dim=rw_dim, w_lora=w_lora, a_lora=a_lora),
        grid=(b, t // tm),
        in_specs=[pl.BlockSpec((1, tm, cols), lambda bi, i: (bi, i, 0)),
                  pl.BlockSpec((1, 8, cols), lambda bi, i: (bi, jnp.maximum(i * tb8 - 1, 0), 0)),
                  pl.BlockSpec((1, 8, cols), lambda bi, i: (bi, jnp.minimum((i + 1) * tb8, nb8 - 1), 0))]
                 + [full(a) for a in consts],
        out_specs=[out_spec] * 11,
        out_shape=[out_sds] * 11,
        compiler_params=_params("parallel", "parallel"),
        name="rwkv_prep",
    )(p, p, p, *consts)


def _rwkv_head_chunk(rd, kkd, binv, kinv, bfin, kfin, vh, s, wtot, incl, strict):
    c = rd.shape[0]
    x = jnp.concatenate([kkd, rd], axis=0)
    y = jnp.concatenate([binv, kinv], axis=0)
    gm = _dot3_nt(x, y)
    h0 = _dot3_nt(x, s)
    a_bk = jnp.where(strict, gm[:c, :c], 0.0)
    a_kk = jnp.where(strict, gm[:c, c:], 0.0)
    bmat = jnp.concatenate([jnp.where(incl, gm[c:, :c], 0.0),
                            jnp.where(incl, gm[c:, c:], 0.0)], axis=1)
    u = -(h0[:c] + _dot3(a_kk, vh))
    nm = -a_bk
    u = u + _dot3(nm, u)
    steps = c.bit_length() - 2
    for _ in range(steps):
        nm = _dot3(nm, nm)
        u = u + _dot3(nm, u)
    uv = jnp.concatenate([u, vh], axis=0)
    o = h0[c:] + _dot3(bmat, uv)
    s_new = s * wtot + _dot3_tn(uv, jnp.concatenate([bfin, kfin], axis=0))
    return o, s_new


def _rwkv_dir_chunk(r, v, kk, lw, kd, bm, s_scr, d, reverse, heads):
    c = r.shape[0]
    incl, strict = _order_masks(c, reverse)
    tri = incl.astype(BF16)
    cum = _dot_sel_l(tri, lw)
    last = 0 if reverse else c - 1
    tot = cum[last:last + 1, :]
    e_neg = jnp.exp(-cum)
    e_fin = jnp.exp(tot - cum)
    rd = r * jnp.exp(cum)
    kkd = kk * jnp.exp(cum - lw)
    kinv = kd * e_neg
    binv = bm * e_neg
    kfin = kd * e_fin
    bfin = bm * e_fin
    wtot = jnp.exp(tot)
    outs = []
    for h in range(heads):
        sl = slice(h * HEAD, (h + 1) * HEAD)
        o, s_new = _rwkv_head_chunk(rd[:, sl], kkd[:, sl], binv[:, sl], kinv[:, sl], bfin[:, sl],
                                    kfin[:, sl], v[:, sl], s_scr[d, h], wtot[:, sl], incl, strict)
        s_scr[d, h] = s_new
        outs.append(o)
    return jnp.concatenate(outs, axis=1)


def _rwkv_scan_kernel(rf, vf, kf, lwf, kdf, bf, rb, vb, kb, lwb, kdb, bb, s0_ref,
                      of_ref, ob_ref, sfin_ref, s_scr, *, heads):
    i = pl.program_id(1)

    @pl.when(i == 0)
    def _():
        s_scr[...] = s0_ref[0]

    of_ref[0] = _rwkv_dir_chunk(rf[0], vf[0], kf[0], lwf[0], kdf[0], bf[0], s_scr, 0, False, heads)
    ob_ref[0] = _rwkv_dir_chunk(rb[0], vb[0], kb[0], lwb[0], kdb[0], bb[0], s_scr, 1, True, heads)

    @pl.when(i == pl.num_programs(1) - 1)
    def _():
        sfin_ref[0] = s_scr[...]


def _rwkv_scan(prep, s0):
    r, v, kk, lw0, lw1, kd0, kd1, b0, b1 = prep
    b, t, rw_dim = r.shape
    heads = rw_dim // HEAD
    n = t // CHUNK
    fwd = pl.BlockSpec((1, CHUNK, rw_dim), lambda bi, i: (bi, i, 0))
    bwd = pl.BlockSpec((1, CHUNK, rw_dim), lambda bi, i: (bi, n - 1 - i, 0))
    st = pl.BlockSpec((1, 2, heads, HEAD, HEAD), lambda bi, i: (bi, 0, 0, 0, 0))
    o_sds = jax.ShapeDtypeStruct((b, t, rw_dim), F32)
    return pl.pallas_call(
        functools.partial(_rwkv_scan_kernel, heads=heads),
        grid=(b, n),
        in_specs=[fwd] * 6 + [bwd] * 6 + [st],
        out_specs=[fwd, bwd, st],
        out_shape=[o_sds, o_sds, jax.ShapeDtypeStruct((b, 2, heads, HEAD, HEAD), F32)],
        scratch_shapes=[pltpu.VMEM((2, heads, HEAD, HEAD), F32)],
        compiler_params=_params("parallel", "arbitrary"),
        name="rwkv_scan",
    )(r, v, kk, lw0, kd0, b0, r, v, kk, lw1, kd1, b1, s0)


def _softmax_pv(s_list, v_list):
    m = s_list[0].max(axis=-1, keepdims=True)
    for s in s_list[1:]:
        m = jnp.maximum(m, s.max(axis=-1, keepdims=True))
    num = None
    den = None
    for s, v in zip(s_list, v_list):
        p = jnp.exp(s - m)
        d = p.sum(axis=-1, keepdims=True)
        o = _dot(p.astype(BF16), v)
        num = o if num is None else num + o
        den = d if den is None else den + d
    return num / den


def _natten_kernel(q_ref, k_ref, v_ref, kc_ref, vc_ref, bias_ref, o_ref, *, rows, scale):
    kc = kc_ref[0].astype(BF16)
    vc = vc_ref[0].astype(BF16)
    nwin = NA_ROWS * GRID_W

    def body(r, carry):
        r0 = jnp.clip(r - NA_ROWS // 2, 0, rows - NA_ROWS)
        d = r - r0
        q = (q_ref[0, pl.ds(pl.multiple_of(r * GRID_W, GRID_W), GRID_W), :] * scale).astype(BF16)
        kw = k_ref[0, pl.ds(pl.multiple_of(r0 * GRID_W, GRID_W), nwin), :].astype(BF16)
        vw = v_ref[0, pl.ds(pl.multiple_of(r0 * GRID_W, GRID_W), nwin), :].astype(BF16)
        outs = []
        for hh in range(2):
            sl = slice(hh * HEAD, (hh + 1) * HEAD)
            s_win = _dot_nt(q[:, sl], kw[:, sl]) + bias_ref[hh, d]
            s_ctx = _dot_nt(q[:, sl], kc[:, sl])
            outs.append(_softmax_pv([s_win, s_ctx], [vw[:, sl], vc[:, sl]]))
        o_ref[0, pl.ds(pl.multiple_of(r * GRID_W, GRID_W), GRID_W), :] = jnp.concatenate(outs, axis=1)
        return carry

    lax.fori_loop(0, rows, body, 0)


def _natten(p_lat, p_ctx, bias, rw_cols, na_dim):
    b, t, _ = p_lat.shape
    l = p_ctx.shape[1]
    pairs = na_dim // (2 * HEAD)
    qb = rw_cols // (2 * HEAD)
    kb = qb + pairs
    vb = kb + pairs
    rows = t // GRID_W

    def lat(off):
        return pl.BlockSpec((1, t, 2 * HEAD), lambda bi, hp: (bi, 0, off + hp))

    def ctx(off):
        return pl.BlockSpec((1, l, 2 * HEAD), lambda bi, hp: (bi, 0, off + hp))

    return pl.pallas_call(
        functools.partial(_natten_kernel, rows=rows, scale=HEAD ** -0.5),
        grid=(b, pairs),
        in_specs=[lat(qb), lat(kb), lat(vb), ctx(kb), ctx(vb),
                  pl.BlockSpec((2, NA_ROWS, GRID_W, NA_ROWS * GRID_W), lambda bi, hp: (hp, 0, 0, 0))],
        out_specs=pl.BlockSpec((1, t, 2 * HEAD), lambda bi, hp: (bi, 0, hp)),
        out_shape=jax.ShapeDtypeStruct((b, t, na_dim), F32),
        compiler_params=_params("parallel", "parallel"),
        name="natten",
    )(p_lat, p_lat, p_lat, p_ctx, p_ctx, bias)


def _ctx_attn_kernel(q_ref, k_ref, v_ref, o_ref, *, scale):
    q = (q_ref[0] * scale).astype(BF16)
    k = k_ref[0].astype(BF16)
    v = v_ref[0].astype(BF16)
    outs = []
    for hh in range(2):
        sl = slice(hh * HEAD, (hh + 1) * HEAD)
        outs.append(_softmax_pv([_dot_nt(q[:, sl], k[:, sl])], [v[:, sl]]))
    o_ref[0] = jnp.concatenate(outs, axis=1)


def _ctx_attn(p_ctx, rw_cols, na_dim):
    b, l, _ = p_ctx.shape
    pairs = na_dim // (2 * HEAD)
    qb = rw_cols // (2 * HEAD)

    def blk(off):
        return pl.BlockSpec((1, l, 2 * HEAD), lambda bi, hp: (bi, 0, off + hp))

    return pl.pallas_call(
        functools.partial(_ctx_attn_kernel, scale=HEAD ** -0.5),
        grid=(b, pairs),
        in_specs=[blk(qb), blk(qb + pairs), blk(qb + 2 * pairs)],
        out_specs=pl.BlockSpec((1, l, 2 * HEAD), lambda bi, hp: (bi, 0, hp)),
        out_shape=jax.ShapeDtypeStruct((b, l, na_dim), F32),
        compiler_params=_params("parallel", "parallel"),
        name="ctx_attn",
    )(p_ctx, p_ctx, p_ctx)


def _natten_bias(rpb):
    cols = jnp.arange(GRID_W)
    c0 = jnp.clip(cols - NA_COLS // 2, 0, GRID_W - NA_COLS)
    key = jnp.arange(GRID_W)
    valid = (key[None, :] >= c0[:, None]) & (key[None, :] < c0[:, None] + NA_COLS)
    col_rel = jnp.clip(key[None, :] - cols[:, None] + NA_COLS - 1, 0, 2 * NA_COLS - 2)
    d = jnp.arange(NA_ROWS)
    i = jnp.arange(NA_ROWS)
    row_rel = i[None, :] - d[:, None] + NA_ROWS - 1
    tab = rpb[:, row_rel[:, None, :, None], col_rel[None, :, None, :]]
    tab = jnp.where(valid[None, None, :, None, :], tab, MASK_BIAS)
    return tab.reshape(rpb.shape[0], NA_ROWS, GRID_W, NA_ROWS * GRID_W).astype(F32)


def _even_out_kernel(h_ref, of_ref, ob_ref, bonus_ref, g_ref, na_ref, w_ref, gate_ref,
                     lng_ref, lnb_ref, bd_ref, o_ref):
    bd = bd_ref[...]
    o = of_ref[0] + ob_ref[0]
    inv = 1.0 / HEAD
    mu = _dot_sel_r(o, bd) * inv
    xc = o - mu
    var = _dot_sel_r(xc * xc, bd) * inv
    y = xc * lax.rsqrt(var + RW_GN_EPS) * lng_ref[...] + lnb_ref[...]
    rw = (y + bonus_ref[0]) * g_ref[0]
    cat = jnp.concatenate([rw, na_ref[0]], axis=1).astype(BF16)
    o_ref[0] = h_ref[0] + gate_ref[0] * _dot(cat, w_ref[...])


def _even_out(h, o_f, o_b, bonus, g, na, w_out, gate, ln_g, ln_b, bd, tm):
    b, t, d = h.shape
    rw_dim = o_f.shape[2]
    na_dim = na.shape[2]
    big = pl.BlockSpec((1, tm, d), lambda bi, i: (bi, i, 0))
    half = pl.BlockSpec((1, tm, rw_dim), lambda bi, i: (bi, i, 0))
    par = pl.BlockSpec((1, rw_dim), lambda bi, i: (0, 0))
    return pl.pallas_call(
        _even_out_kernel,
        grid=(b, t // tm),
        in_specs=[big, half, half, half, half,
                  pl.BlockSpec((1, tm, na_dim), lambda bi, i: (bi, i, 0)),
                  pl.BlockSpec(w_out.shape, lambda bi, i: (0, 0)),
                  pl.BlockSpec((1, 1, d), lambda bi, i: (bi, 0, 0)),
                  par, par,
                  pl.BlockSpec(bd.shape, lambda bi, i: (0, 0))],
        out_specs=big,
        out_shape=jax.ShapeDtypeStruct((b, t, d), F32),
        compiler_params=_params("parallel", "parallel"),
        name="even_out",
    )(h, o_f, o_b, bonus, g, na, w_out, gate, ln_g.reshape(1, rw_dim), ln_b.reshape(1, rw_dim), bd)


def _odd_out_kernel(h_ref, of_ref, ob_ref, gate_in_ref, w_ref, gate_ref, ng_ref, o_ref):
    y = _rms(of_ref[0] + ob_ref[0], ng_ref[...]) * _silu(gate_in_ref[0])
    o_ref[0] = h_ref[0] + gate_ref[0] * _dot(y.astype(BF16), w_ref[...])


def _odd_out(h, o_f, o_b, p, gate_block, w_out, gate, norm_g, tm):
    b, t, d = h.shape
    vd = o_f.shape[2]
    big = pl.BlockSpec((1, tm, d), lambda bi, i: (bi, i, 0))
    val = pl.BlockSpec((1, tm, vd), lambda bi, i: (bi, i, 0))
    return pl.pallas_call(
        _odd_out_kernel,
        grid=(b, t // tm),
        in_specs=[big, val, val,
                  pl.BlockSpec((1, tm, vd), lambda bi, i: (bi, i, gate_block)),
                  pl.BlockSpec(w_out.shape, lambda bi, i: (0, 0)),
                  pl.BlockSpec((1, 1, d), lambda bi, i: (bi, 0, 0)),
                  pl.BlockSpec((1, vd), lambda bi, i: (0, 0))],
        out_specs=big,
        out_shape=jax.ShapeDtypeStruct((b, t, d), F32),
        compiler_params=_params("parallel", "parallel"),
        name="odd_out",
    )(h, o_f, o_b, p, w_out, gate, norm_g.reshape(1, vd))


def _hgrn_dir_chunk(qp, fp, ip, lb, st, reverse):
    c = qp.shape[0]
    incl, _ = _order_masks(c, reverse)
    q = _silu(qp)
    forget = lb + (1.0 - lb) * _sigmoid(fp)
    k = 1.0 - forget
    cum = _dot_sel_l(incl.astype(BF16), jnp.log(forget))
    last = 0 if reverse else c - 1
    tot = cum[last:last + 1, :]
    mid = cum[c // 2:c // 2 + 1, :]
    att = jnp.where(incl, _dot3_nt(q * jnp.exp(cum - mid), k * jnp.exp(mid - cum)), 0.0)
    o = _dot3(att, ip) + _dot3_nt(q * jnp.exp(cum), st)
    st_new = st * jnp.exp(tot) + _dot3_tn(ip, k * jnp.exp(tot - cum))
    return o, st_new


def _hgrn_scan_kernel(qf, ff, vf, qb, fb, vb, hl_ref, s0_ref, of_ref, ob_ref, sfin_ref, s_scr,
                      *, layer, chunks):
    i = pl.program_id(2)

    @pl.when(i == 0)
    def _():
        s_scr[...] = s0_ref[0, :, 0]

    hl = hl_ref[...]
    e = jnp.exp(hl - hl.max(axis=0, keepdims=True))
    sm = e / e.sum(axis=0, keepdims=True)
    lb = jnp.zeros_like(sm[0:1])
    for j in range(1, layer + 1):
        lb = lb + sm[j:j + 1]

    for cidx in range(chunks):
        sl = pl.ds(cidx * CHUNK, CHUNK)
        o, s_new = _hgrn_dir_chunk(qf[0, sl, :], ff[0, sl, :], vf[0, sl, :], lb, s_scr[0], False)
        s_scr[0] = s_new
        of_ref[0, sl, :] = o
    for cidx in reversed(range(chunks)):
        sl = pl.ds(cidx * CHUNK, CHUNK)
        o, s_new = _hgrn_dir_chunk(qb[0, sl, :], fb[0, sl, :], vb[0, sl, :], lb, s_scr[1], True)
        s_scr[1] = s_new
        ob_ref[0, sl, :] = o

    @pl.when(i == pl.num_programs(2) - 1)
    def _():
        sfin_ref[0, :, 0] = s_scr[...]


def _hgrn_scan(p, hg_lower, s0, layer, heads, tb):
    b, t, _ = p.shape
    n = t // tb
    depth = hg_lower.shape[0]

    def fwd(off):
        return pl.BlockSpec((1, tb, HG_KEY), lambda bi, h, i: (bi, i, off + h))

    def bwd(off):
        return pl.BlockSpec((1, tb, HG_KEY), lambda bi, h, i: (bi, n - 1 - i, off + h))

    st = pl.BlockSpec((1, 2, 1, HG_KEY, HG_KEY), lambda bi, h, i: (bi, 0, h, 0, 0))
    o_sds = jax.ShapeDtypeStruct((b, t, heads * HG_KEY), F32)
    return pl.pallas_call(
        functools.partial(_hgrn_scan_kernel, layer=layer, chunks=tb // CHUNK),
        grid=(b, heads, n),
        in_specs=[fwd(0), fwd(heads), fwd(3 * heads), bwd(0), bwd(2 * heads), bwd(3 * heads),
                  pl.BlockSpec((depth, HG_KEY), lambda bi, h, i: (0, h)), st],
        out_specs=[fwd(0), bwd(0), st],
        out_shape=[o_sds, o_sds, jax.ShapeDtypeStruct((b, 2, heads, HG_KEY, HG_KEY), F32)],
        scratch_shapes=[pltpu.VMEM((2, HG_KEY, HG_KEY), F32)],
        compiler_params=_params("parallel", "parallel", "arbitrary"),
        name="hgrn_scan",
    )(p, p, p, p, p, p, hg_lower, s0)


def _row_tile(t, target):
    return target if t % target == 0 else t


def kernel(x, c, ctx, c_ctx, norm_mix_g, norm_mlp_g, ada_w, ada_b, mlp_w1, mlp_w2, ev_w_in,
           ev_w_out, rw_mu_prev, rw_mu_next, rw_w0, rw_w_up, rw_a0, rw_a_up, rw_g_up, rw_k_k,
           rw_k_a, rw_r_k, rw_ln_g, rw_ln_b, na_rpb, od_w_in, od_w_out, hg_lower, hg_norm_g,
           final_norm_g):
    b, t, d = x.shape
    l = ctx.shape[1]
    depth = ada_w.shape[0]
    rw_dim = rw_k_k.shape[1]
    rw_heads = rw_dim // HEAD
    rw_cols = rw_mu_prev.shape[1]
    na_dim = d - rw_dim
    hg_heads = d // HG_KEY

    pad_rows = -(b + 1) % SUBLANES
    c_rows = jnp.concatenate([c, c_ctx[None], jnp.zeros((pad_rows, d), F32)], axis=0)
    mods = _ada(c_rows, ada_w, ada_b)

    def mod_vecs(layer):
        m = mods[layer]
        lat = [m[:b, j * d:(j + 1) * d].reshape(b, 1, d) for j in range(6)]
        cx = [jnp.broadcast_to(m[b, j * d:(j + 1) * d].reshape(1, 1, d), (b, 1, d)) for j in range(6)]
        return lat, cx

    eye = jnp.arange(rw_dim) // HEAD
    bd = (eye[:, None] == eye[None, :]).astype(BF16)

    tm_lat = _row_tile(t, 512)
    tm_ctx = _row_tile(l, 256)
    h_lat, h_ctx = x, ctx
    for layer in range(depth):
        last = layer == depth - 1
        (sh1, s1, g1, sh2, s2, g2), (csh1, cs1, cg1, csh2, cs2, cg2) = mod_vecs(layer)
        if layer % 2 == 0:
            e = layer // 2
            w_in = ev_w_in[e].astype(BF16)
            n_in = w_in.shape[1]
            p_lat = _norm_mod_matmul(h_lat, norm_mix_g[layer], sh1, s1, w_in, tm_lat, n_in)
            p_ctx = _norm_mod_matmul(h_ctx, norm_mix_g[layer], csh1, cs1, w_in, tm_ctx, n_in)
            prm = dict(
                mu_prev=rw_mu_prev[e][None], mu_next=rw_mu_next[e][None],
                w0=rw_w0[e].reshape(1, 2 * rw_dim),
                w_up=jnp.concatenate([rw_w_up[e, 0], rw_w_up[e, 1]], axis=1),
                a0=rw_a0[e].reshape(1, 2 * rw_dim),
                a_up=jnp.concatenate([rw_a_up[e, 0], rw_a_up[e, 1]], axis=1),
                g_up=rw_g_up[e], k_k=rw_k_k[e][None], k_a=rw_k_a[e][None],
                r_k=rw_r_k[e].reshape(1, rw_dim), bd=bd)
            s_zero = jnp.zeros((b, 2, rw_heads, HEAD, HEAD), F32)
            prep_c = _rwkv_prep(p_ctx, prm, tm_ctx)
            oc_f, oc_b, s_ctx = _rwkv_scan(prep_c[:9], s_zero)
            prep_l = _rwkv_prep(p_lat, prm, tm_lat)
            ol_f, ol_b, _ = _rwkv_scan(prep_l[:9], s_ctx)
            na_lat = _natten(p_lat, p_ctx, _natten_bias(na_rpb[e]), rw_cols, na_dim)
            w_out = ev_w_out[e].astype(BF16)
            h_lat = _even_out(h_lat, ol_f, ol_b, prep_l[10], prep_l[9], na_lat, w_out, g1,
                              rw_ln_g[e], rw_ln_b[e], bd, tm_lat)
            if not last:
                na_ctx = _ctx_attn(p_ctx, rw_cols, na_dim)
                h_ctx = _even_out(h_ctx, oc_f, oc_b, prep_c[10], prep_c[9], na_ctx, w_out, cg1,
                                  rw_ln_g[e], rw_ln_b[e], bd, tm_ctx)
        else:
            o = layer // 2
            w_in = od_w_in[o].astype(BF16)
            n_in = w_in.shape[1]
            tn = n_in // 2
            p_lat = _norm_mod_matmul(h_lat, norm_mix_g[layer], sh1, s1, w_in, tm_lat, tn)
            p_ctx = _norm_mod_matmul(h_ctx, norm_mix_g[layer], csh1, cs1, w_in, tm_ctx, tn)
            s_zero = jnp.zeros((b, 2, hg_heads, HG_KEY, HG_KEY), F32)
            oc_f, oc_b, s_ctx = _hgrn_scan(p_ctx, hg_lower, s_zero, layer, hg_heads, _row_tile(l, 256))
            ol_f, ol_b, _ = _hgrn_scan(p_lat, hg_lower, s_ctx, layer, hg_heads, _row_tile(t, 256))
            w_out = od_w_out[o].astype(BF16)
            gate_block = (n_in - d) // d
            h_lat = _odd_out(h_lat, ol_f, ol_b, p_lat, gate_block, w_out, g1, hg_norm_g[o], tm_lat)
            if not last:
                h_ctx = _odd_out(h_ctx, oc_f, oc_b, p_ctx, gate_block, w_out, cg1, hg_norm_g[o], tm_ctx)
        w1 = mlp_w1[layer].astype(BF16)
        w2 = mlp_w2[layer].astype(BF16)
        h_lat = _mlp(h_lat, norm_mlp_g[layer], sh2, s2, g2, w1, w2, final_norm_g, tm_lat, 512, last)
        if not last:
            h_ctx = _mlp(h_ctx, norm_mlp_g[layer], csh2, cs2, cg2, w1, w2, final_norm_g, tm_ctx, 512, False)
    return h_lat
```

```python
import functools

import jax
import jax.numpy as jnp
import numpy as np
from jax import lax
from jax.experimental import pallas as pl
from jax.experimental.pallas import tpu as pltpu

F32 = jnp.float32
BF16 = jnp.bfloat16
HIGHEST = lax.Precision.HIGHEST

NORM_EPS = 1e-6
RW_GN_EPS = 64e-5
HEAD = 64
GRID_W = 64
NA_ROWS = 8
NA_COLS = 16
HG_KEY = 128
CHUNK = 64
MASK_BIAS = -1e30
SUBLANES = 8
VMEM_LIMIT = 52 * 1024 * 1024


def _params(*sem):
    return pltpu.CompilerParams(dimension_semantics=sem, vmem_limit_bytes=VMEM_LIMIT)


def _dot(a, b):
    return jnp.dot(a, b, preferred_element_type=F32)


def _dot_nt(a, b):
    return lax.dot_general(a, b, (((1,), (1,)), ((), ())), preferred_element_type=F32)


def _dot_tn(a, b):
    return lax.dot_general(a, b, (((0,), (0,)), ((), ())), preferred_element_type=F32)


def _mm(a, b):
    return _dot(a.astype(BF16), b.astype(BF16))


def _mm_nt(a, b):
    return _dot_nt(a.astype(BF16), b.astype(BF16))


def _mm_tn(a, b):
    return _dot_tn(a.astype(BF16), b.astype(BF16))


def _bmm(a, b):
    return lax.dot_general(a.astype(BF16), b.astype(BF16), (((2,), (1,)), ((0,), (0,))),
                           preferred_element_type=F32)


def _bmm_nt(a, b):
    return lax.dot_general(a.astype(BF16), b.astype(BF16), (((2,), (2,)), ((0,), (0,))),
                           preferred_element_type=F32)


def _bmm_tn(a, b):
    return lax.dot_general(a.astype(BF16), b.astype(BF16), (((1,), (1,)), ((0,), (0,))),
                           preferred_element_type=F32)


def _split3(x):
    x0 = x.astype(BF16)
    r = x - x0.astype(F32)
    x1 = r.astype(BF16)
    x2 = (r - x1.astype(F32)).astype(BF16)
    return x0, x1, x2


def _dot_sel_r(x, sel):
    x0, x1, x2 = _split3(x)
    return _dot(x0, sel) + (_dot(x1, sel) + _dot(x2, sel))


def _dot_sel_l(sel, x):
    x0, x1, x2 = _split3(x)
    return _dot(sel, x0) + (_dot(sel, x1) + _dot(sel, x2))


def _sigmoid(x):
    return 1.0 / (1.0 + jnp.exp(-x))


def _silu(x):
    return x * _sigmoid(x)


def _softplus(x):
    return jnp.maximum(x, 0.0) + jnp.log(1.0 + jnp.exp(-jnp.abs(x)))


def _rms(x, g):
    return x * lax.rsqrt(jnp.mean(x * x, axis=-1, keepdims=True) + NORM_EPS) * g


def _order_masks(n, reverse):
    t = lax.broadcasted_iota(jnp.int32, (n, n), 0)
    i = lax.broadcasted_iota(jnp.int32, (n, n), 1)
    if reverse:
        return i >= t, i > t
    return i <= t, i < t


def _ada_kernel(c_ref, w_ref, b_ref, o_ref):
    s = _silu(c_ref[...])
    o_ref[0] = jnp.dot(s, w_ref[0], precision=HIGHEST, preferred_element_type=F32) + b_ref[0]


def _ada(c_rows, ada_w, ada_b):
    depth, d, n = ada_w.shape
    rows = c_rows.shape[0]
    tn = 512
    return pl.pallas_call(
        _ada_kernel,
        grid=(depth, n // tn),
        in_specs=[pl.BlockSpec((rows, d), lambda l, j: (0, 0)),
                  pl.BlockSpec((1, d, tn), lambda l, j: (l, 0, j)),
                  pl.BlockSpec((1, 1, tn), lambda l, j: (l, 0, j))],
        out_specs=pl.BlockSpec((1, rows, tn), lambda l, j: (l, 0, j)),
        out_shape=jax.ShapeDtypeStruct((depth, rows, n), F32),
        compiler_params=_params("parallel", "parallel"),
        name="ada",
    )(c_rows, ada_w, ada_b.reshape(depth, 1, n))


def _nmm_kernel(x_ref, g_ref, sh_ref, sc_ref, w_ref, o_ref, a_scr):
    @pl.when(pl.program_id(2) == 0)
    def _():
        a = _rms(x_ref[0], g_ref[...]) * (1.0 + sc_ref[0]) + sh_ref[0]
        a_scr[...] = a.astype(BF16)

    o_ref[0] = _dot(a_scr[...], w_ref[...])


def _norm_mod_matmul(x, g, shift, scale, w, tm, tn):
    b, t, d = x.shape
    n = w.shape[1]
    return pl.pallas_call(
        _nmm_kernel,
        grid=(b, t // tm, n // tn),
        in_specs=[pl.BlockSpec((1, tm, d), lambda bi, i, j: (bi, i, 0)),
                  pl.BlockSpec((1, d), lambda bi, i, j: (0, 0)),
                  pl.BlockSpec((1, 1, d), lambda bi, i, j: (bi, 0, 0)),
                  pl.BlockSpec((1, 1, d), lambda bi, i, j: (bi, 0, 0)),
                  pl.BlockSpec((d, tn), lambda bi, i, j: (0, j))],
        out_specs=pl.BlockSpec((1, tm, tn), lambda bi, i, j: (bi, i, j)),
        out_shape=jax.ShapeDtypeStruct((b, t, n), F32),
        scratch_shapes=[pltpu.VMEM((tm, d), BF16)],
        compiler_params=_params("parallel", "parallel", "arbitrary"),
        name="norm_mod_matmul",
    )(x, g.reshape(1, d), shift, scale, w)


def _mlp_kernel(h_ref, g_ref, sh_ref, sc_ref, gate_ref, w1_ref, w2_ref, fg_ref, o_ref,
                a_scr, acc_scr, *, final_norm):
    j = pl.program_id(2)

    @pl.when(j == 0)
    def _():
        a = _rms(h_ref[0], g_ref[...]) * (1.0 + sc_ref[0]) + sh_ref[0]
        a_scr[...] = a.astype(BF16)
        acc_scr[...] = jnp.zeros_like(acc_scr)

    hid = jnp.maximum(_dot(a_scr[...], w1_ref[...]), 0.0)
    acc_scr[...] += _dot((hid * hid).astype(BF16), w2_ref[...])

    @pl.when(j == pl.num_programs(2) - 1)
    def _():
        out = h_ref[0] + gate_ref[0] * acc_scr[...]
        if final_norm:
            out = _rms(out, fg_ref[...])
        o_ref[0] = out


def _mlp(h, g, shift, scale, gate, w1, w2, final_g, tm, th, final_norm):
    b, t, d = h.shape
    hid = w1.shape[1]
    vec = pl.BlockSpec((1, 1, d), lambda bi, i, j: (bi, 0, 0))
    par = pl.BlockSpec((1, d), lambda bi, i, j: (0, 0))
    return pl.pallas_call(
        functools.partial(_mlp_kernel, final_norm=final_norm),
        grid=(b, t // tm, hid // th),
        in_specs=[pl.BlockSpec((1, tm, d), lambda bi, i, j: (bi, i, 0)),
                  par, vec, vec, vec,
                  pl.BlockSpec((d, th), lambda bi, i, j: (0, j)),
                  pl.BlockSpec((th, d), lambda bi, i, j: (j, 0)),
                  par],
        out_specs=pl.BlockSpec((1, tm, d), lambda bi, i, j: (bi, i, 0)),
        out_shape=jax.ShapeDtypeStruct((b, t, d), F32),
        scratch_shapes=[pltpu.VMEM((tm, d), BF16), pltpu.VMEM((tm, d), F32)],
        compiler_params=_params("parallel", "parallel", "arbitrary"),
        name="mlp",
    )(h, g.reshape(1, d), shift, scale, gate, w1, w2, final_g.reshape(1, d))


def _rwkv_prep_kernel(p_ref, pp_ref, pn_ref, mup_ref, mun_ref, w0_ref, wup_ref, a0_ref, aup_ref,
                      gup_ref, kk_ref, ka_ref, rk_ref, bd_ref,
                      r_o, v_o, kkn_o, lw0_o, lw1_o, kd0_o, kd1_o, b0_o, b1_o, g_o, bonus_o,
                      *, rw_dim, w_lora, a_lora):
    i = pl.program_id(1)
    n = pl.num_programs(1)
    p = p_ref[0]
    tm = p.shape[0]
    row = lax.broadcasted_iota(jnp.int32, (tm, 1), 0)
    first = jnp.where(i > 0, pp_ref[0][SUBLANES - 1:SUBLANES, :], 0.0)
    last = jnp.where(i < n - 1, pn_ref[0][0:1, :], 0.0)
    prev = jnp.where(row == 0, first, pltpu.roll(p, 1, axis=0))
    nxt = jnp.where(row == tm - 1, last, pltpu.roll(p, tm - 1, axis=0))
    ps = p + (prev - p) * mup_ref[...] + (nxt - p) * mun_ref[...]

    r = ps[:, :rw_dim]
    k = ps[:, rw_dim:2 * rw_dim]
    v = ps[:, 2 * rw_dim:3 * rw_dim]
    off = 3 * rw_dim
    w_lo = ps[:, off:off + w_lora]
    a_lo = ps[:, off + w_lora:off + w_lora + a_lora]
    g_lo = ps[:, off + w_lora + a_lora:]

    w_pre = w0_ref[...] + _mm(jnp.tanh(w_lo), wup_ref[...])
    w_log = -_softplus(-w_pre) - 0.5
    log_decay = -jnp.exp(w_log)
    a = _sigmoid(a0_ref[...] + _mm(a_lo, aup_ref[...]))
    g = _mm(_sigmoid(g_lo), gup_ref[...])

    bd = bd_ref[...]
    kk = k * kk_ref[...]
    kk = kk / jnp.maximum(jnp.sqrt(_dot_sel_r(kk * kk, bd)), 1e-12)
    a_f = a[:, :rw_dim]
    a_b = a[:, rw_dim:]
    ka = ka_ref[...]
    kd_f = k * (1.0 + (a_f - 1.0) * ka)
    kd_b = k * (1.0 + (a_b - 1.0) * ka)
    bonus = _dot_sel_r(r * (kd_f + kd_b) * rk_ref[...], bd) * v

    r_o[0] = r
    v_o[0] = v
    kkn_o[0] = kk
    lw0_o[0] = log_decay[:, :rw_dim]
    lw1_o[0] = log_decay[:, rw_dim:]
    kd0_o[0] = kd_f
    kd1_o[0] = kd_b
    b0_o[0] = kk * a_f
    b1_o[0] = kk * a_b
    g_o[0] = g
    bonus_o[0] = bonus


def _rwkv_prep(p, prm, tm):
    b, t, _ = p.shape
    rw_dim = prm["k_k"].shape[1]
    w_lora = prm["w_up"].shape[0]
    a_lora = prm["a_up"].shape[0]
    g_lora = prm["g_up"].shape[0]
    cols = 3 * rw_dim + w_lora + a_lora + g_lora
    nb8 = t // 8
    tb8 = tm // 8

    def full(a):
        return pl.BlockSpec(a.shape, lambda bi, i: (0,) * a.ndim)

    consts = [prm["mu_prev"], prm["mu_next"], prm["w0"], prm["w_up"], prm["a0"], prm["a_up"],
              prm["g_up"], prm["k_k"], prm["k_a"], prm["r_k"], prm["bd"]]
    out_spec = pl.BlockSpec((1, tm, rw_dim), lambda bi, i: (bi, i, 0))
    out_sds = jax.ShapeDtypeStruct((b, t, rw_dim), F32)
    return pl.pallas_call(
        functools.partial(_rwkv_prep_kernel, rw_dim=rw_dim, w_lora=w_lora, a_lora=a_lora),
        grid=(b, t // tm),
        in_specs=[pl.BlockSpec((1, tm, cols), lambda bi, i: (bi, i, 0)),
                  pl.BlockSpec((1, 8, cols), lambda bi, i: (bi, jnp.maximum(i * tb8 - 1, 0), 0)),
                  pl.BlockSpec((1, 8, cols), lambda bi, i: (bi, jnp.minimum((i + 1) * tb8, nb8 - 1), 0))]
                 + [full(a) for a in consts],
        out_specs=[out_spec] * 11,
        out_shape=[out_sds] * 11,
        compiler_params=_params("parallel", "parallel"),
        name="rwkv_prep",
    )(p, p, p, *consts)


def _rwkv_chunk(rd, kkd, binv, kinv, bfin, kfin, v, s, wtot, incl, strict):
    c = rd.shape[1]
    x = jnp.concatenate([kkd, rd], axis=1)
    y = jnp.concatenate([binv, kinv], axis=1)
    gm = _bmm_nt(x, y)
    h0 = _bmm_nt(x, s)
    a_bk = jnp.where(strict, gm[:, :c, :c], 0.0)
    a_kk = jnp.where(strict, gm[:, :c, c:], 0.0)
    bmat = jnp.concatenate([jnp.where(incl, gm[:, c:, :c], 0.0),
                            jnp.where(incl, gm[:, c:, c:], 0.0)], axis=2)
    u = -(h0[:, :c] + _bmm(a_kk, v))
    nm = -a_bk
    u = u + _bmm(nm, u)
    for _ in range(c.bit_length() - 2):
        nm = _bmm(nm, nm)
        u = u + _bmm(nm, u)
    uv = jnp.concatenate([u, v], axis=1)
    o = h0[:, c:] + _bmm(bmat, uv)
    s_new = s * wtot + _bmm_tn(uv, jnp.concatenate([bfin, kfin], axis=1))
    return o, s_new


def _rwkv_decay_terms(r, v, kk, lw, kd, bm, reverse):
    c = r.shape[0]
    incl, _ = _order_masks(c, reverse)
    cum = _dot_sel_l(incl.astype(BF16), lw)
    last = 0 if reverse else c - 1
    tot = cum[last:last + 1, :]
    e_neg = jnp.exp(-cum)
    e_fin = jnp.exp(tot - cum)
    return (r * jnp.exp(cum), kk * jnp.exp(cum - lw), bm * e_neg, kd * e_neg, bm * e_fin,
            kd * e_fin, v, jnp.exp(tot))


def _rwkv_scan_kernel(rf, vf, kf, lwf, kdf, bf, rb, vb, kb, lwb, kdb, bb, s0_ref,
                      of_ref, ob_ref, sfin_ref, s_scr, *, heads):
    i = pl.program_id(1)

    @pl.when(i == 0)
    def _():
        s_scr[...] = s0_ref[0]

    c = rf.shape[1]
    fw = _rwkv_decay_terms(rf[0], vf[0], kf[0], lwf[0], kdf[0], bf[0], False)
    bw = _rwkv_decay_terms(rb[0], vb[0], kb[0], lwb[0], kdb[0], bb[0], True)
    ops = [jnp.stack([z[:, h * HEAD:(h + 1) * HEAD] for z in (zf, zb) for h in range(heads)])
           for zf, zb in zip(fw, bw)]
    t = lax.broadcasted_iota(jnp.int32, (2 * heads, c, c), 1)
    j = lax.broadcasted_iota(jnp.int32, (2 * heads, c, c), 2)
    rev = lax.broadcasted_iota(jnp.int32, (2 * heads, c, c), 0) >= heads
    ahead = jnp.where(rev, t - j, j - t)
    incl = ahead <= 0
    strict = ahead < 0
    o, s_new = _rwkv_chunk(*ops[:7], s_scr[...], ops[7], incl, strict)
    s_scr[...] = s_new
    of_ref[0] = jnp.concatenate([o[h] for h in range(heads)], axis=1)
    ob_ref[0] = jnp.concatenate([o[heads + h] for h in range(heads)], axis=1)

    @pl.when(i == pl.num_programs(1) - 1)
    def _():
        sfin_ref[0] = s_scr[...]


def _rwkv_scan(prep, s0):
    r, v, kk, lw0, lw1, kd0, kd1, b0, b1 = prep
    b, t, rw_dim = r.shape
    heads = rw_dim // HEAD
    n = t // CHUNK
    fwd = pl.BlockSpec((1, CHUNK, rw_dim), lambda bi, i: (bi, i, 0))
    bwd = pl.BlockSpec((1, CHUNK, rw_dim), lambda bi, i: (bi, n - 1 - i, 0))
    st = pl.BlockSpec((1, 2 * heads, HEAD, HEAD), lambda bi, i: (bi, 0, 0, 0))
    o_sds = jax.ShapeDtypeStruct((b, t, rw_dim), F32)
    return pl.pallas_call(
        functools.partial(_rwkv_scan_kernel, heads=heads),
        grid=(b, n),
        in_specs=[fwd] * 6 + [bwd] * 6 + [st],
        out_specs=[fwd, bwd, st],
        out_shape=[o_sds, o_sds, jax.ShapeDtypeStruct((b, 2 * heads, HEAD, HEAD), F32)],
        scratch_shapes=[pltpu.VMEM((2 * heads, HEAD, HEAD), F32)],
        compiler_params=_params("parallel", "arbitrary"),
        name="rwkv_scan",
    )(r, v, kk, lw0, kd0, b0, r, v, kk, lw1, kd1, b1, s0)


def _softmax_pv(s_list, v_list):
    m = s_list[0].max(axis=-1, keepdims=True)
    for s in s_list[1:]:
        m = jnp.maximum(m, s.max(axis=-1, keepdims=True))
    num = None
    den = None
    for s, v in zip(s_list, v_list):
        p = jnp.exp(s - m)
        d = p.sum(axis=-1, keepdims=True)
        o = _dot(p.astype(BF16), v)
        num = o if num is None else num + o
        den = d if den is None else den + d
    return num / den


def _natten_kernel(q_ref, k_ref, v_ref, kc_ref, vc_ref, bias_ref, o_ref, *, rows, scale):
    kc = kc_ref[0].astype(BF16)
    vc = vc_ref[0].astype(BF16)
    nwin = NA_ROWS * GRID_W

    def body(r, carry):
        r0 = jnp.clip(r - NA_ROWS // 2, 0, rows - NA_ROWS)
        d = r - r0
        q = (q_ref[0, pl.ds(pl.multiple_of(r * GRID_W, GRID_W), GRID_W), :] * scale).astype(BF16)
        kw = k_ref[0, pl.ds(pl.multiple_of(r0 * GRID_W, GRID_W), nwin), :].astype(BF16)
        vw = v_ref[0, pl.ds(pl.multiple_of(r0 * GRID_W, GRID_W), nwin), :].astype(BF16)
        outs = []
        for hh in range(2):
            sl = slice(hh * HEAD, (hh + 1) * HEAD)
            s_win = _dot_nt(q[:, sl], kw[:, sl]) + bias_ref[hh, d]
            s_ctx = _dot_nt(q[:, sl], kc[:, sl])
            outs.append(_softmax_pv([s_win, s_ctx], [vw[:, sl], vc[:, sl]]))
        o_ref[0, pl.ds(pl.multiple_of(r * GRID_W, GRID_W), GRID_W), :] = jnp.concatenate(outs, axis=1)
        return carry

    lax.fori_loop(0, rows, body, 0, unroll=2)


def _natten(p_lat, p_ctx, bias, rw_cols, na_dim):
    b, t, _ = p_lat.shape
    l = p_ctx.shape[1]
    pairs = na_dim // (2 * HEAD)
    qb = rw_cols // (2 * HEAD)
    kb = qb + pairs
    vb = kb + pairs
    rows = t // GRID_W

    def lat(off):
        return pl.BlockSpec((1, t, 2 * HEAD), lambda bi, hp: (bi, 0, off + hp))

    def ctx(off):
        return pl.BlockSpec((1, l, 2 * HEAD), lambda bi, hp: (bi, 0, off + hp))

    return pl.pallas_call(
        functools.partial(_natten_kernel, rows=rows, scale=HEAD ** -0.5),
        grid=(b, pairs),
        in_specs=[lat(qb), lat(kb), lat(vb), ctx(kb), ctx(vb),
                  pl.BlockSpec((2, NA_ROWS, GRID_W, NA_ROWS * GRID_W), lambda bi, hp: (hp, 0, 0, 0))],
        out_specs=pl.BlockSpec((1, t, 2 * HEAD), lambda bi, hp: (bi, 0, hp)),
        out_shape=jax.ShapeDtypeStruct((b, t, na_dim), F32),
        compiler_params=_params("parallel", "parallel"),
        name="natten",
    )(p_lat, p_lat, p_lat, p_ctx, p_ctx, bias)


def _ctx_attn_kernel(q_ref, k_ref, v_ref, o_ref, *, scale):
    q = (q_ref[0] * scale).astype(BF16)
    k = k_ref[0].astype(BF16)
    v = v_ref[0].astype(BF16)
    outs = []
    for hh in range(2):
        sl = slice(hh * HEAD, (hh + 1) * HEAD)
        outs.append(_softmax_pv([_dot_nt(q[:, sl], k[:, sl])], [v[:, sl]]))
    o_ref[0] = jnp.concatenate(outs, axis=1)


def _ctx_attn(p_ctx, rw_cols, na_dim):
    b, l, _ = p_ctx.shape
    pairs = na_dim // (2 * HEAD)
    qb = rw_cols // (2 * HEAD)

    def blk(off):
        return pl.BlockSpec((1, l, 2 * HEAD), lambda bi, hp: (bi, 0, off + hp))

    return pl.pallas_call(
        functools.partial(_ctx_attn_kernel, scale=HEAD ** -0.5),
        grid=(b, pairs),
        in_specs=[blk(qb), blk(qb + pairs), blk(qb + 2 * pairs)],
        out_specs=pl.BlockSpec((1, l, 2 * HEAD), lambda bi, hp: (bi, 0, hp)),
        out_shape=jax.ShapeDtypeStruct((b, l, na_dim), F32),
        compiler_params=_params("parallel", "parallel"),
        name="ctx_attn",
    )(p_ctx, p_ctx, p_ctx)


def _natten_bias(rpb):
    cols = np.arange(GRID_W)
    c0 = np.clip(cols - NA_COLS // 2, 0, GRID_W - NA_COLS)
    valid = (cols[None, :] >= c0[:, None]) & (cols[None, :] < c0[:, None] + NA_COLS)
    col_rel = cols[None, :] - cols[:, None] + NA_COLS - 1
    onehot = (col_rel[:, :, None] == np.arange(2 * NA_COLS - 1)).astype(np.float32)
    toe = jnp.einsum("hrj,qkj->hrqk", rpb.astype(F32), onehot, precision=HIGHEST)
    toe = jnp.where(valid[None, None], toe, MASK_BIAS)
    tab = jnp.stack([toe[:, NA_ROWS - 1 - d:2 * NA_ROWS - 1 - d] for d in range(NA_ROWS)], axis=1)
    tab = tab.transpose(0, 1, 3, 2, 4)
    return tab.reshape(rpb.shape[0], NA_ROWS, GRID_W, NA_ROWS * GRID_W)


def _even_out_kernel(h_ref, of_ref, ob_ref, bonus_ref, g_ref, na_ref, w_ref, gate_ref,
                     lng_ref, lnb_ref, bd_ref, o_ref):
    bd = bd_ref[...]
    o = of_ref[0] + ob_ref[0]
    inv = 1.0 / HEAD
    mu = _dot_sel_r(o, bd) * inv
    xc = o - mu
    var = _dot_sel_r(xc * xc, bd) * inv
    y = xc * lax.rsqrt(var + RW_GN_EPS) * lng_ref[...] + lnb_ref[...]
    rw = (y + bonus_ref[0]) * g_ref[0]
    cat = jnp.concatenate([rw, na_ref[0]], axis=1).astype(BF16)
    o_ref[0] = h_ref[0] + gate_ref[0] * _dot(cat, w_ref[...])


def _even_out(h, o_f, o_b, bonus, g, na, w_out, gate, ln_g, ln_b, bd, tm):
    b, t, d = h.shape
    rw_dim = o_f.shape[2]
    na_dim = na.shape[2]
    big = pl.BlockSpec((1, tm, d), lambda bi, i: (bi, i, 0))
    half = pl.BlockSpec((1, tm, rw_dim), lambda bi, i: (bi, i, 0))
    par = pl.BlockSpec((1, rw_dim), lambda bi, i: (0, 0))
    return pl.pallas_call(
        _even_out_kernel,
        grid=(b, t // tm),
        in_specs=[big, half, half, half, half,
                  pl.BlockSpec((1, tm, na_dim), lambda bi, i: (bi, i, 0)),
                  pl.BlockSpec(w_out.shape, lambda bi, i: (0, 0)),
                  pl.BlockSpec((1, 1, d), lambda bi, i: (bi, 0, 0)),
                  par, par,
                  pl.BlockSpec(bd.shape, lambda bi, i: (0, 0))],
        out_specs=big,
        out_shape=jax.ShapeDtypeStruct((b, t, d), F32),
        compiler_params=_params("parallel", "parallel"),
        name="even_out",
    )(h, o_f, o_b, bonus, g, na, w_out, gate, ln_g.reshape(1, rw_dim), ln_b.reshape(1, rw_dim), bd)


def _odd_out_kernel(h_ref, of_ref, ob_ref, gate_in_ref, w_ref, gate_ref, ng_ref, o_ref):
    y = _rms(of_ref[0] + ob_ref[0], ng_ref[...]) * _silu(gate_in_ref[0])
    o_ref[0] = h_ref[0] + gate_ref[0] * _dot(y.astype(BF16), w_ref[...])


def _odd_out(h, o_f, o_b, p, gate_block, w_out, gate, norm_g, tm):
    b, t, d = h.shape
    vd = o_f.shape[2]
    big = pl.BlockSpec((1, tm, d), lambda bi, i: (bi, i, 0))
    val = pl.BlockSpec((1, tm, vd), lambda bi, i: (bi, i, 0))
    return pl.pallas_call(
        _odd_out_kernel,
        grid=(b, t // tm),
        in_specs=[big, val, val,
                  pl.BlockSpec((1, tm, vd), lambda bi, i: (bi, i, gate_block)),
                  pl.BlockSpec(w_out.shape, lambda bi, i: (0, 0)),
                  pl.BlockSpec((1, 1, d), lambda bi, i: (bi, 0, 0)),
                  pl.BlockSpec((1, vd), lambda bi, i: (0, 0))],
        out_specs=big,
        out_shape=jax.ShapeDtypeStruct((b, t, d), F32),
        compiler_params=_params("parallel", "parallel"),
        name="odd_out",
    )(h, o_f, o_b, p, w_out, gate, norm_g.reshape(1, vd))


def _hgrn_dir_chunk(qp, fp, ip, lb, st, reverse):
    c = qp.shape[0]
    incl, _ = _order_masks(c, reverse)
    q = _silu(qp)
    forget = lb + (1.0 - lb) * _sigmoid(fp)
    k = 1.0 - forget
    cum = _dot_sel_l(incl.astype(BF16), jnp.log(forget))
    last = 0 if reverse else c - 1
    tot = cum[last:last + 1, :]
    mid = cum[c // 2:c // 2 + 1, :]
    att = jnp.where(incl, _mm_nt(q * jnp.exp(cum - mid), k * jnp.exp(mid - cum)), 0.0)
    o = _mm(att, ip) + _mm_nt(q * jnp.exp(cum), st)
    st_new = st * jnp.exp(tot) + _mm_tn(ip, k * jnp.exp(tot - cum))
    return o, st_new


def _hgrn_scan_kernel(qf, ff, vf, qb, fb, vb, hl_ref, s0_ref, of_ref, ob_ref, sfin_ref, s_scr,
                      *, layer, chunks):
    i = pl.program_id(2)

    @pl.when(i == 0)
    def _():
        s_scr[...] = s0_ref[0, :, 0]

    hl = hl_ref[...]
    e = jnp.exp(hl - hl.max(axis=0, keepdims=True))
    sm = e / e.sum(axis=0, keepdims=True)
    lb = jnp.zeros_like(sm[0:1])
    for j in range(1, layer + 1):
        lb = lb + sm[j:j + 1]

    s_f = s_scr[0]
    s_b = s_scr[1]
    for cidx in range(chunks):
        sl = pl.ds(cidx * CHUNK, CHUNK)
        o, s_f = _hgrn_dir_chunk(qf[0, sl, :], ff[0, sl, :], vf[0, sl, :], lb, s_f, False)
        of_ref[0, sl, :] = o
        sl = pl.ds((chunks - 1 - cidx) * CHUNK, CHUNK)
        o, s_b = _hgrn_dir_chunk(qb[0, sl, :], fb[0, sl, :], vb[0, sl, :], lb, s_b, True)
        ob_ref[0, sl, :] = o
    s_scr[0] = s_f
    s_scr[1] = s_b

    @pl.when(i == pl.num_programs(2) - 1)
    def _():
        sfin_ref[0, :, 0] = s_scr[...]


def _hgrn_scan(p, hg_lower, s0, layer, heads, tb):
    b, t, _ = p.shape
    n = t // tb
    depth = hg_lower.shape[0]

    def fwd(off):
        return pl.BlockSpec((1, tb, HG_KEY), lambda bi, h, i: (bi, i, off + h))

    def bwd(off):
        return pl.BlockSpec((1, tb, HG_KEY), lambda bi, h, i: (bi, n - 1 - i, off + h))

    st = pl.BlockSpec((1, 2, 1, HG_KEY, HG_KEY), lambda bi, h, i: (bi, 0, h, 0, 0))
    o_sds = jax.ShapeDtypeStruct((b, t, heads * HG_KEY), F32)
    return pl.pallas_call(
        functools.partial(_hgrn_scan_kernel, layer=layer, chunks=tb // CHUNK),
        grid=(b, heads, n),
        in_specs=[fwd(0), fwd(heads), fwd(3 * heads), bwd(0), bwd(2 * heads), bwd(3 * heads),
                  pl.BlockSpec((depth, HG_KEY), lambda bi, h, i: (0, h)), st],
        out_specs=[fwd(0), bwd(0), st],
        out_shape=[o_sds, o_sds, jax.ShapeDtypeStruct((b, 2, heads, HG_KEY, HG_KEY), F32)],
        scratch_shapes=[pltpu.VMEM((2, HG_KEY, HG_KEY), F32)],
        compiler_params=_params("parallel", "parallel", "arbitrary"),
        name="hgrn_scan",
    )(p, p, p, p, p, p, hg_lower, s0)


def _row_tile(t, target):
    return target if t % target == 0 else t


def kernel(x, c, ctx, c_ctx, norm_mix_g, norm_mlp_g, ada_w, ada_b, mlp_w1, mlp_w2, ev_w_in,
           ev_w_out, rw_mu_prev, rw_mu_next, rw_w0, rw_w_up, rw_a0, rw_a_up, rw_g_up, rw_k_k,
           rw_k_a, rw_r_k, rw_ln_g, rw_ln_b, na_rpb, od_w_in, od_w_out, hg_lower, hg_norm_g,
           final_norm_g):
    b, t, d = x.shape
    l = ctx.shape[1]
    depth = ada_w.shape[0]
    rw_dim = rw_k_k.shape[1]
    rw_heads = rw_dim // HEAD
    rw_cols = rw_mu_prev.shape[1]
    na_dim = d - rw_dim
    hg_heads = d // HG_KEY

    pad_rows = -(b + 1) % SUBLANES
    c_rows = jnp.concatenate([c, c_ctx[None], jnp.zeros((pad_rows, d), F32)], axis=0)
    mods = _ada(c_rows, ada_w, ada_b)

    def mod_vecs(layer):
        m = mods[layer]
        lat = [m[:b, j * d:(j + 1) * d].reshape(b, 1, d) for j in range(6)]
        cx = [jnp.broadcast_to(m[b, j * d:(j + 1) * d].reshape(1, 1, d), (b, 1, d)) for j in range(6)]
        return lat, cx

    eye = jnp.arange(rw_dim) // HEAD
    bd = (eye[:, None] == eye[None, :]).astype(BF16)

    tm_lat = _row_tile(t, 512)
    tm_ctx = _row_tile(l, 256)
    h_lat, h_ctx = x, ctx
    for layer in range(depth):
        last = layer == depth - 1
        (sh1, s1, g1, sh2, s2, g2), (csh1, cs1, cg1, csh2, cs2, cg2) = mod_vecs(layer)
        if layer % 2 == 0:
            e = layer // 2
            w_in = ev_w_in[e].astype(BF16)
            n_in = w_in.shape[1]
            p_lat = _norm_mod_matmul(h_lat, norm_mix_g[layer], sh1, s1, w_in, tm_lat, n_in)
            p_ctx = _norm_mod_matmul(h_ctx, norm_mix_g[layer], csh1, cs1, w_in, tm_ctx, n_in)
            prm = dict(
                mu_prev=rw_mu_prev[e][None], mu_next=rw_mu_next[e][None],
                w0=rw_w0[e].reshape(1, 2 * rw_dim),
                w_up=jnp.concatenate([rw_w_up[e, 0], rw_w_up[e, 1]], axis=1),
                a0=rw_a0[e].reshape(1, 2 * rw_dim),
                a_up=jnp.concatenate([rw_a_up[e, 0], rw_a_up[e, 1]], axis=1),
                g_up=rw_g_up[e], k_k=rw_k_k[e][None], k_a=rw_k_a[e][None],
                r_k=rw_r_k[e].reshape(1, rw_dim), bd=bd)
            s_zero = jnp.zeros((b, 2 * rw_heads, HEAD, HEAD), F32)
            prep_c = _rwkv_prep(p_ctx, prm, tm_ctx)
            oc_f, oc_b, s_ctx = _rwkv_scan(prep_c[:9], s_zero)
            prep_l = _rwkv_prep(p_lat, prm, tm_lat)
            ol_f, ol_b, _ = _rwkv_scan(prep_l[:9], s_ctx)
            na_lat = _natten(p_lat, p_ctx, _natten_bias(na_rpb[e]), rw_cols, na_dim)
            w_out = ev_w_out[e].astype(BF16)
            h_lat = _even_out(h_lat, ol_f, ol_b, prep_l[10], prep_l[9], na_lat, w_out, g1,
                              rw_ln_g[e], rw_ln_b[e], bd, tm_lat)
            if not last:
                na_ctx = _ctx_attn(p_ctx, rw_cols, na_dim)
                h_ctx = _even_out(h_ctx, oc_f, oc_b, prep_c[10], prep_c[9], na_ctx, w_out, cg1,
                                  rw_ln_g[e], rw_ln_b[e], bd, tm_ctx)
        else:
            o = layer // 2
            w_in = od_w_in[o].astype(BF16)
            n_in = w_in.shape[1]
            tn = n_in // 2
            p_lat = _norm_mod_matmul(h_lat, norm_mix_g[layer], sh1, s1, w_in, tm_lat, tn)
            p_ctx = _norm_mod_matmul(h_ctx, norm_mix_g[layer], csh1, cs1, w_in, tm_ctx, tn)
            s_zero = jnp.zeros((b, 2, hg_heads, HG_KEY, HG_KEY), F32)
            oc_f, oc_b, s_ctx = _hgrn_scan(p_ctx, hg_lower, s_zero, layer, hg_heads, _row_tile(l, 256))
            ol_f, ol_b, _ = _hgrn_scan(p_lat, hg_lower, s_ctx, layer, hg_heads, _row_tile(t, 256))
            w_out = od_w_out[o].astype(BF16)
            gate_block = (n_in - d) // d
            h_lat = _odd_out(h_lat, ol_f, ol_b, p_lat, gate_block, w_out, g1, hg_norm_g[o], tm_lat)
            if not last:
                h_ctx = _odd_out(h_ctx, oc_f, oc_b, p_ctx, gate_block, w_out, cg1, hg_norm_g[o], tm_ctx)
        w1 = mlp_w1[layer].astype(BF16)
        w2 = mlp_w2[layer].astype(BF16)
        h_lat = _mlp(h_lat, norm_mlp_g[layer], sh2, s2, g2, w1, w2, final_norm_g, tm_lat, 512, last)
        if not last:
            h_ctx = _mlp(h_ctx, norm_mlp_g[layer], csh2, cs2, cg2, w1, w2, final_norm_g, tm_ctx, 512, False)
    return h_lat
```

```python
import functools

import jax
import jax.numpy as jnp
import numpy as np
from jax import lax
from jax.experimental import pallas as pl
from jax.experimental.pallas import tpu as pltpu

F32 = jnp.float32
BF16 = jnp.bfloat16
HIGHEST = lax.Precision.HIGHEST

NORM_EPS = 1e-6
RW_GN_EPS = 64e-5
HEAD = 64
GRID_W = 64
NA_ROWS = 8
NA_COLS = 16
HG_KEY = 128
CHUNK = 64
MASK_BIAS = -1e30
SUBLANES = 8
HALO_ROWS = 16
VMEM_LIMIT = 52 * 1024 * 1024


def _params(*sem):
    return pltpu.CompilerParams(dimension_semantics=sem, vmem_limit_bytes=VMEM_LIMIT)


def _dot(a, b):
    return jnp.dot(a, b, preferred_element_type=F32)


def _dot_nt(a, b):
    return lax.dot_general(a, b, (((1,), (1,)), ((), ())), preferred_element_type=F32)


def _dot_tn(a, b):
    return lax.dot_general(a, b, (((0,), (0,)), ((), ())), preferred_element_type=F32)


def _mm(a, b):
    return _dot(a.astype(BF16), b.astype(BF16))


def _mm_nt(a, b):
    return _dot_nt(a.astype(BF16), b.astype(BF16))


def _mm_tn(a, b):
    return _dot_tn(a.astype(BF16), b.astype(BF16))


def _bmm(a, b):
    return lax.dot_general(a.astype(BF16), b.astype(BF16), (((2,), (1,)), ((0,), (0,))),
                           preferred_element_type=F32)


def _bmm_nt(a, b):
    return lax.dot_general(a.astype(BF16), b.astype(BF16), (((2,), (2,)), ((0,), (0,))),
                           preferred_element_type=F32)


def _bmm_tn(a, b):
    return lax.dot_general(a.astype(BF16), b.astype(BF16), (((1,), (1,)), ((0,), (0,))),
                           preferred_element_type=F32)


def _split3(x):
    x0 = x.astype(BF16)
    r = x - x0.astype(F32)
    x1 = r.astype(BF16)
    x2 = (r - x1.astype(F32)).astype(BF16)
    return x0, x1, x2


def _dot_sel_r(x, sel):
    x0, x1, x2 = _split3(x)
    return _dot(x0, sel) + (_dot(x1, sel) + _dot(x2, sel))


def _dot_sel_l(sel, x):
    x0, x1, x2 = _split3(x)
    return _dot(sel, x0) + (_dot(sel, x1) + _dot(sel, x2))


def _sigmoid(x):
    return 1.0 / (1.0 + jnp.exp(-x))


def _silu(x):
    return x * _sigmoid(x)


def _softplus(x):
    return jnp.maximum(x, 0.0) + jnp.log(1.0 + jnp.exp(-jnp.abs(x)))


def _rms(x, g):
    return x * lax.rsqrt(jnp.mean(x * x, axis=-1, keepdims=True) + NORM_EPS) * g


def _order_masks(n, reverse):
    t = lax.broadcasted_iota(jnp.int32, (n, n), 0)
    i = lax.broadcasted_iota(jnp.int32, (n, n), 1)
    if reverse:
        return i >= t, i > t
    return i <= t, i < t


def _ada_kernel(c_ref, w_ref, b_ref, o_ref):
    s = _silu(c_ref[...])
    o_ref[0] = jnp.dot(s, w_ref[0], precision=HIGHEST, preferred_element_type=F32) + b_ref[0]


def _ada(c_rows, ada_w, ada_b):
    depth, d, n = ada_w.shape
    rows = c_rows.shape[0]
    tn = 512
    return pl.pallas_call(
        _ada_kernel,
        grid=(depth, n // tn),
        in_specs=[pl.BlockSpec((rows, d), lambda l, j: (0, 0)),
                  pl.BlockSpec((1, d, tn), lambda l, j: (l, 0, j)),
                  pl.BlockSpec((1, 1, tn), lambda l, j: (l, 0, j))],
        out_specs=pl.BlockSpec((1, rows, tn), lambda l, j: (l, 0, j)),
        out_shape=jax.ShapeDtypeStruct((depth, rows, n), F32),
        compiler_params=_params("parallel", "parallel"),
        name="ada",
    )(c_rows, ada_w, ada_b.reshape(depth, 1, n))


def _nmm_kernel(x_ref, g_ref, sh_ref, sc_ref, w_ref, o_ref, a_scr):
    @pl.when(pl.program_id(2) == 0)
    def _():
        a = _rms(x_ref[0], g_ref[...]) * (1.0 + sc_ref[0]) + sh_ref[0]
        a_scr[...] = a.astype(BF16)

    o_ref[0] = _dot(a_scr[...], w_ref[...]).astype(o_ref.dtype)


def _norm_mod_matmul(x, g, shift, scale, w, tm, tn):
    b, t, d = x.shape
    n = w.shape[1]
    return pl.pallas_call(
        _nmm_kernel,
        grid=(b, t // tm, n // tn),
        in_specs=[pl.BlockSpec((1, tm, d), lambda bi, i, j: (bi, i, 0)),
                  pl.BlockSpec((1, d), lambda bi, i, j: (0, 0)),
                  pl.BlockSpec((1, 1, d), lambda bi, i, j: (bi, 0, 0)),
                  pl.BlockSpec((1, 1, d), lambda bi, i, j: (bi, 0, 0)),
                  pl.BlockSpec((d, tn), lambda bi, i, j: (0, j))],
        out_specs=pl.BlockSpec((1, tm, tn), lambda bi, i, j: (bi, i, j)),
        out_shape=jax.ShapeDtypeStruct((b, t, n), BF16),
        scratch_shapes=[pltpu.VMEM((tm, d), BF16)],
        compiler_params=_params("parallel", "parallel", "arbitrary"),
        name="norm_mod_matmul",
    )(x, g.reshape(1, d), shift, scale, w)


def _mlp_kernel(h_ref, g_ref, sh_ref, sc_ref, gate_ref, w1_ref, w2_ref, fg_ref, o_ref,
                a_scr, acc_scr, *, final_norm):
    j = pl.program_id(2)

    @pl.when(j == 0)
    def _():
        a = _rms(h_ref[0], g_ref[...]) * (1.0 + sc_ref[0]) + sh_ref[0]
        a_scr[...] = a.astype(BF16)
        acc_scr[...] = jnp.zeros_like(acc_scr)

    hid = jnp.maximum(_dot(a_scr[...], w1_ref[...]), 0.0)
    acc_scr[...] += _dot((hid * hid).astype(BF16), w2_ref[...])

    @pl.when(j == pl.num_programs(2) - 1)
    def _():
        out = h_ref[0] + gate_ref[0] * acc_scr[...]
        if final_norm:
            out = _rms(out, fg_ref[...])
        o_ref[0] = out


def _mlp(h, g, shift, scale, gate, w1, w2, final_g, tm, th, final_norm):
    b, t, d = h.shape
    hid = w1.shape[1]
    vec = pl.BlockSpec((1, 1, d), lambda bi, i, j: (bi, 0, 0))
    par = pl.BlockSpec((1, d), lambda bi, i, j: (0, 0))
    return pl.pallas_call(
        functools.partial(_mlp_kernel, final_norm=final_norm),
        grid=(b, t // tm, hid // th),
        in_specs=[pl.BlockSpec((1, tm, d), lambda bi, i, j: (bi, i, 0)),
                  par, vec, vec, vec,
                  pl.BlockSpec((d, th), lambda bi, i, j: (0, j)),
                  pl.BlockSpec((th, d), lambda bi, i, j: (j, 0)),
                  par],
        out_specs=pl.BlockSpec((1, tm, d), lambda bi, i, j: (bi, i, 0)),
        out_shape=jax.ShapeDtypeStruct((b, t, d), F32),
        scratch_shapes=[pltpu.VMEM((tm, d), BF16), pltpu.VMEM((tm, d), F32)],
        compiler_params=_params("parallel", "parallel", "arbitrary"),
        name="mlp",
    )(h, g.reshape(1, d), shift, scale, gate, w1, w2, final_g.reshape(1, d))


def _rwkv_prep_kernel(p_ref, pp_ref, pn_ref, mup_ref, mun_ref, w0_ref, wup_ref, a0_ref, aup_ref,
                      gup_ref, kk_ref, ka_ref, rk_ref, bd_ref,
                      r_o, v_o, kkn_o, lw0_o, lw1_o, kd0_o, kd1_o, b0_o, b1_o, g_o, bonus_o,
                      *, rw_dim, w_lora, a_lora):
    i = pl.program_id(1)
    n = pl.num_programs(1)
    p = p_ref[0].astype(F32)
    tm = p.shape[0]
    row = lax.broadcasted_iota(jnp.int32, (tm, 1), 0)
    first = jnp.where(i > 0, pp_ref[0].astype(F32)[HALO_ROWS - 1:HALO_ROWS, :], 0.0)
    last = jnp.where(i < n - 1, pn_ref[0].astype(F32)[0:1, :], 0.0)
    prev = jnp.where(row == 0, first, pltpu.roll(p, 1, axis=0))
    nxt = jnp.where(row == tm - 1, last, pltpu.roll(p, tm - 1, axis=0))
    ps = p + (prev - p) * mup_ref[...] + (nxt - p) * mun_ref[...]

    r = ps[:, :rw_dim]
    k = ps[:, rw_dim:2 * rw_dim]
    v = ps[:, 2 * rw_dim:3 * rw_dim]
    off = 3 * rw_dim
    w_lo = ps[:, off:off + w_lora]
    a_lo = ps[:, off + w_lora:off + w_lora + a_lora]
    g_lo = ps[:, off + w_lora + a_lora:]

    w_pre = w0_ref[...] + _mm(jnp.tanh(w_lo), wup_ref[...])
    w_log = -_softplus(-w_pre) - 0.5
    log_decay = -jnp.exp(w_log)
    a = _sigmoid(a0_ref[...] + _mm(a_lo, aup_ref[...]))
    g = _mm(_sigmoid(g_lo), gup_ref[...])

    bd = bd_ref[...]
    kk = k * kk_ref[...]
    kk = kk / jnp.maximum(jnp.sqrt(_dot_sel_r(kk * kk, bd)), 1e-12)
    a_f = a[:, :rw_dim]
    a_b = a[:, rw_dim:]
    ka = ka_ref[...]
    kd_f = k * (1.0 + (a_f - 1.0) * ka)
    kd_b = k * (1.0 + (a_b - 1.0) * ka)
    bonus = _dot_sel_r(r * (kd_f + kd_b) * rk_ref[...], bd) * v

    r_o[0] = r.astype(BF16)
    v_o[0] = v.astype(BF16)
    kkn_o[0] = kk.astype(BF16)
    lw0_o[0] = log_decay[:, :rw_dim]
    lw1_o[0] = log_decay[:, rw_dim:]
    kd0_o[0] = kd_f.astype(BF16)
    kd1_o[0] = kd_b.astype(BF16)
    b0_o[0] = (kk * a_f).astype(BF16)
    b1_o[0] = (kk * a_b).astype(BF16)
    g_o[0] = g.astype(BF16)
    bonus_o[0] = bonus.astype(BF16)


def _rwkv_prep(p, prm, tm):
    b, t, _ = p.shape
    rw_dim = prm["k_k"].shape[1]
    w_lora = prm["w_up"].shape[0]
    a_lora = prm["a_up"].shape[0]
    g_lora = prm["g_up"].shape[0]
    cols = 3 * rw_dim + w_lora + a_lora + g_lora
    n_halo = t // HALO_ROWS
    tm_halo = tm // HALO_ROWS

    def full(a):
        return pl.BlockSpec(a.shape, lambda bi, i: (0,) * a.ndim)

    consts = [prm["mu_prev"], prm["mu_next"], prm["w0"], prm["w_up"], prm["a0"], prm["a_up"],
              prm["g_up"], prm["k_k"], prm["k_a"], prm["r_k"], prm["bd"]]
    out_spec = pl.BlockSpec((1, tm, rw_dim), lambda bi, i: (bi, i, 0))
    out_dtypes = [BF16, BF16, BF16, F32, F32, BF16, BF16, BF16, BF16, BF16, BF16]
    return pl.pallas_call(
        functools.partial(_rwkv_prep_kernel, rw_dim=rw_dim, w_lora=w_lora, a_lora=a_lora),
        grid=(b, t // tm),
        in_specs=[pl.BlockSpec((1, tm, cols), lambda bi, i: (bi, i, 0)),
                  pl.BlockSpec((1, HALO_ROWS, cols),
                               lambda bi, i: (bi, jnp.maximum(i * tm_halo - 1, 0), 0)),
                  pl.BlockSpec((1, HALO_ROWS, cols),
                               lambda bi, i: (bi, jnp.minimum((i + 1) * tm_halo, n_halo - 1), 0))]
                 + [full(a) for a in consts],
        out_specs=[out_spec] * 11,
        out_shape=[jax.ShapeDtypeStruct((b, t, rw_dim), dt) for dt in out_dtypes],
        compiler_params=_params("parallel", "parallel"),
        name="rwkv_prep",
    )(p, p, p, *consts)


def _rwkv_chunk(rd, kkd, binv, kinv, bfin, kfin, v, s, wtot, incl, strict):
    c = rd.shape[1]
    x = jnp.concatenate([kkd, rd], axis=1)
    y = jnp.concatenate([binv, kinv], axis=1)
    gm = _bmm_nt(x, y)
    h0 = _bmm_nt(x, s)
    a_bk = jnp.where(strict, gm[:, :c, :c], 0.0)
    a_kk = jnp.where(strict, gm[:, :c, c:], 0.0)
    bmat = jnp.concatenate([jnp.where(incl, gm[:, c:, :c], 0.0),
                            jnp.where(incl, gm[:, c:, c:], 0.0)], axis=2)
    u = -(h0[:, :c] + _bmm(a_kk, v))
    nm = -a_bk
    u = u + _bmm(nm, u)
    for _ in range(c.bit_length() - 2):
        nm = _bmm(nm, nm)
        u = u + _bmm(nm, u)
    uv = jnp.concatenate([u, v], axis=1)
    o = h0[:, c:] + _bmm(bmat, uv)
    s_new = s * wtot + _bmm_tn(uv, jnp.concatenate([bfin, kfin], axis=1))
    return o, s_new


def _rwkv_decay_terms(r, v, kk, lw, kd, bm, reverse):
    c = r.shape[0]
    r, v, kk, kd, bm = (z.astype(F32) for z in (r, v, kk, kd, bm))
    incl, _ = _order_masks(c, reverse)
    cum = _dot_sel_l(incl.astype(BF16), lw)
    last = 0 if reverse else c - 1
    tot = cum[last:last + 1, :]
    e_neg = jnp.exp(-cum)
    e_fin = jnp.exp(tot - cum)
    return (r * jnp.exp(cum), kk * jnp.exp(cum - lw), bm * e_neg, kd * e_neg, bm * e_fin,
            kd * e_fin, v, jnp.exp(tot))


def _rwkv_scan_kernel(rf, vf, kf, lwf, kdf, bf, rb, vb, kb, lwb, kdb, bb, s0_ref,
                      of_ref, ob_ref, sfin_ref, s_scr, *, heads):
    i = pl.program_id(1)

    @pl.when(i == 0)
    def _():
        s_scr[...] = s0_ref[0]

    c = rf.shape[1]
    fw = _rwkv_decay_terms(rf[0], vf[0], kf[0], lwf[0], kdf[0], bf[0], False)
    bw = _rwkv_decay_terms(rb[0], vb[0], kb[0], lwb[0], kdb[0], bb[0], True)
    ops = [jnp.stack([z[:, h * HEAD:(h + 1) * HEAD] for z in (zf, zb) for h in range(heads)])
           for zf, zb in zip(fw, bw)]
    t = lax.broadcasted_iota(jnp.int32, (2 * heads, c, c), 1)
    j = lax.broadcasted_iota(jnp.int32, (2 * heads, c, c), 2)
    rev = lax.broadcasted_iota(jnp.int32, (2 * heads, c, c), 0) >= heads
    ahead = jnp.where(rev, t - j, j - t)
    incl = ahead <= 0
    strict = ahead < 0
    o, s_new = _rwkv_chunk(*ops[:7], s_scr[...], ops[7], incl, strict)
    s_scr[...] = s_new
    of_ref[0] = jnp.concatenate([o[h] for h in range(heads)], axis=1).astype(of_ref.dtype)
    ob_ref[0] = jnp.concatenate([o[heads + h] for h in range(heads)], axis=1).astype(ob_ref.dtype)

    @pl.when(i == pl.num_programs(1) - 1)
    def _():
        sfin_ref[0] = s_scr[...]


def _rwkv_scan(prep, s0):
    r, v, kk, lw0, lw1, kd0, kd1, b0, b1 = prep
    b, t, rw_dim = r.shape
    heads = rw_dim // HEAD
    n = t // CHUNK
    fwd = pl.BlockSpec((1, CHUNK, rw_dim), lambda bi, i: (bi, i, 0))
    bwd = pl.BlockSpec((1, CHUNK, rw_dim), lambda bi, i: (bi, n - 1 - i, 0))
    st = pl.BlockSpec((1, 2 * heads, HEAD, HEAD), lambda bi, i: (bi, 0, 0, 0))
    o_sds = jax.ShapeDtypeStruct((b, t, rw_dim), BF16)
    return pl.pallas_call(
        functools.partial(_rwkv_scan_kernel, heads=heads),
        grid=(b, n),
        in_specs=[fwd] * 6 + [bwd] * 6 + [st],
        out_specs=[fwd, bwd, st],
        out_shape=[o_sds, o_sds, jax.ShapeDtypeStruct((b, 2 * heads, HEAD, HEAD), F32)],
        scratch_shapes=[pltpu.VMEM((2 * heads, HEAD, HEAD), F32)],
        compiler_params=_params("parallel", "arbitrary"),
        name="rwkv_scan",
    )(r, v, kk, lw0, kd0, b0, r, v, kk, lw1, kd1, b1, s0)


def _softmax_pv(s_list, v_list):
    m = s_list[0].max(axis=-1, keepdims=True)
    for s in s_list[1:]:
        m = jnp.maximum(m, s.max(axis=-1, keepdims=True))
    num = None
    den = None
    for s, v in zip(s_list, v_list):
        p = jnp.exp(s - m)
        d = p.sum(axis=-1, keepdims=True)
        o = _dot(p.astype(BF16), v)
        num = o if num is None else num + o
        den = d if den is None else den + d
    return num / den


def _natten_kernel(q_ref, k_ref, v_ref, kc_ref, vc_ref, bias_ref, o_ref, *, rows, scale, rb):
    kc = kc_ref[0]
    vc = vc_ref[0]
    l = kc.shape[0]
    nwin = NA_ROWS * GRID_W
    head0 = lax.broadcasted_iota(jnp.int32, (1, 2 * HEAD), 1) < HEAD

    def body(it, carry):
        base = it * rb
        start = pl.multiple_of(base * GRID_W, rb * GRID_W)
        q = q_ref[0, pl.ds(start, rb * GRID_W), :] * scale
        zero = jnp.zeros_like(q)
        q0 = jnp.where(head0, q, zero)
        q1 = jnp.where(head0, zero, q)
        qs, kws, vws, bs = [], [], [], []
        for j in range(rb):
            r = base + j
            r0 = jnp.clip(r - NA_ROWS // 2, 0, rows - NA_ROWS)
            rows_j = slice(j * GRID_W, (j + 1) * GRID_W)
            qs.append(jnp.concatenate([q0[rows_j], q1[rows_j]], axis=0))
            win = pl.ds(pl.multiple_of(r0 * GRID_W, GRID_W), nwin)
            kws.append(k_ref[0, win, :])
            vws.append(v_ref[0, win, :])
            bs.append(bias_ref[0, r - r0])
        qs = jnp.stack(qs)
        s_win = _bmm_nt(qs, jnp.stack(kws)) + jnp.stack(bs)
        s_ctx = _dot_nt(qs.reshape(rb * 2 * GRID_W, 2 * HEAD), kc).reshape(rb, 2 * GRID_W, l)
        m = jnp.maximum(s_win.max(axis=-1, keepdims=True), s_ctx.max(axis=-1, keepdims=True))
        p_win = jnp.exp(s_win - m)
        p_ctx = jnp.exp(s_ctx - m)
        den = p_win.sum(axis=-1, keepdims=True) + p_ctx.sum(axis=-1, keepdims=True)
        o = _bmm(p_win, jnp.stack(vws))
        o = o + _dot(p_ctx.reshape(rb * 2 * GRID_W, l).astype(BF16), vc).reshape(rb, 2 * GRID_W, 2 * HEAD)
        o = o / den
        out = jnp.concatenate([jnp.where(head0, o[j, :GRID_W], o[j, GRID_W:]) for j in range(rb)], axis=0)
        o_ref[0, pl.ds(start, rb * GRID_W), :] = out.astype(o_ref.dtype)
        return carry

    lax.fori_loop(0, rows // rb, body, 0)


def _natten(p_lat, p_ctx, bias, rw_cols, na_dim):
    b, t, _ = p_lat.shape
    l = p_ctx.shape[1]
    pairs = na_dim // (2 * HEAD)
    qb = rw_cols // (2 * HEAD)
    kb = qb + pairs
    vb = kb + pairs
    rows = t // GRID_W

    def lat(off):
        return pl.BlockSpec((1, t, 2 * HEAD), lambda bi, hp: (bi, 0, off + hp))

    def ctx(off):
        return pl.BlockSpec((1, l, 2 * HEAD), lambda bi, hp: (bi, 0, off + hp))

    rb = 4 if rows % 4 == 0 else 1
    return pl.pallas_call(
        functools.partial(_natten_kernel, rows=rows, scale=HEAD ** -0.5, rb=rb),
        grid=(b, pairs),
        in_specs=[lat(qb), lat(kb), lat(vb), ctx(kb), ctx(vb),
                  pl.BlockSpec((1, NA_ROWS, 2 * GRID_W, NA_ROWS * GRID_W), lambda bi, hp: (hp, 0, 0, 0))],
        out_specs=pl.BlockSpec((1, t, 2 * HEAD), lambda bi, hp: (bi, 0, hp)),
        out_shape=jax.ShapeDtypeStruct((b, t, na_dim), BF16),
        compiler_params=_params("parallel", "parallel"),
        name="natten",
    )(p_lat, p_lat, p_lat, p_ctx, p_ctx, bias)


def _ctx_attn_kernel(q_ref, k_ref, v_ref, o_ref, *, scale):
    q = q_ref[0] * scale
    k = k_ref[0]
    v = v_ref[0]
    outs = []
    for hh in range(2):
        sl = slice(hh * HEAD, (hh + 1) * HEAD)
        outs.append(_softmax_pv([_dot_nt(q[:, sl], k[:, sl])], [v[:, sl]]))
    o_ref[0] = jnp.concatenate(outs, axis=1).astype(o_ref.dtype)


def _ctx_attn(p_ctx, rw_cols, na_dim):
    b, l, _ = p_ctx.shape
    pairs = na_dim // (2 * HEAD)
    qb = rw_cols // (2 * HEAD)

    def blk(off):
        return pl.BlockSpec((1, l, 2 * HEAD), lambda bi, hp: (bi, 0, off + hp))

    return pl.pallas_call(
        functools.partial(_ctx_attn_kernel, scale=HEAD ** -0.5),
        grid=(b, pairs),
        in_specs=[blk(qb), blk(qb + pairs), blk(qb + 2 * pairs)],
        out_specs=pl.BlockSpec((1, l, 2 * HEAD), lambda bi, hp: (bi, 0, hp)),
        out_shape=jax.ShapeDtypeStruct((b, l, na_dim), BF16),
        compiler_params=_params("parallel", "parallel"),
        name="ctx_attn",
    )(p_ctx, p_ctx, p_ctx)


def _natten_bias(rpb):
    cols = np.arange(GRID_W)
    c0 = np.clip(cols - NA_COLS // 2, 0, GRID_W - NA_COLS)
    valid = (cols[None, :] >= c0[:, None]) & (cols[None, :] < c0[:, None] + NA_COLS)
    col_rel = cols[None, :] - cols[:, None] + NA_COLS - 1
    onehot = (col_rel[:, :, None] == np.arange(2 * NA_COLS - 1)).astype(np.float32)
    toe = jnp.einsum("hrj,qkj->hrqk", rpb.astype(F32), onehot, precision=HIGHEST)
    toe = jnp.where(valid[None, None], toe, MASK_BIAS)
    tab = jnp.stack([toe[:, NA_ROWS - 1 - d:2 * NA_ROWS - 1 - d] for d in range(NA_ROWS)], axis=1)
    tab = tab.transpose(0, 1, 3, 2, 4)
    heads = rpb.shape[0]
    tab = tab.reshape(heads // 2, 2, NA_ROWS, GRID_W, NA_ROWS * GRID_W)
    return tab.transpose(0, 2, 1, 3, 4).reshape(heads // 2, NA_ROWS, 2 * GRID_W, NA_ROWS * GRID_W)


def _even_out_kernel(h_ref, of_ref, ob_ref, bonus_ref, g_ref, na_ref, w_ref, gate_ref,
                     lng_ref, lnb_ref, bd_ref, o_ref):
    bd = bd_ref[...]
    o = of_ref[0].astype(F32) + ob_ref[0].astype(F32)
    inv = 1.0 / HEAD
    mu = _dot_sel_r(o, bd) * inv
    xc = o - mu
    var = _dot_sel_r(xc * xc, bd) * inv
    y = xc * lax.rsqrt(var + RW_GN_EPS) * lng_ref[...] + lnb_ref[...]
    rw = (y + bonus_ref[0].astype(F32)) * g_ref[0].astype(F32)
    cat = jnp.concatenate([rw.astype(BF16), na_ref[0]], axis=1)
    o_ref[0] = h_ref[0] + gate_ref[0] * _dot(cat, w_ref[...])


def _even_out(h, o_f, o_b, bonus, g, na, w_out, gate, ln_g, ln_b, bd, tm):
    b, t, d = h.shape
    rw_dim = o_f.shape[2]
    na_dim = na.shape[2]
    big = pl.BlockSpec((1, tm, d), lambda bi, i: (bi, i, 0))
    half = pl.BlockSpec((1, tm, rw_dim), lambda bi, i: (bi, i, 0))
    par = pl.BlockSpec((1, rw_dim), lambda bi, i: (0, 0))
    return pl.pallas_call(
        _even_out_kernel,
        grid=(b, t // tm),
        in_specs=[big, half, half, half, half,
                  pl.BlockSpec((1, tm, na_dim), lambda bi, i: (bi, i, 0)),
                  pl.BlockSpec(w_out.shape, lambda bi, i: (0, 0)),
                  pl.BlockSpec((1, 1, d), lambda bi, i: (bi, 0, 0)),
                  par, par,
                  pl.BlockSpec(bd.shape, lambda bi, i: (0, 0))],
        out_specs=big,
        out_shape=jax.ShapeDtypeStruct((b, t, d), F32),
        compiler_params=_params("parallel", "parallel"),
        name="even_out",
    )(h, o_f, o_b, bonus, g, na, w_out, gate, ln_g.reshape(1, rw_dim), ln_b.reshape(1, rw_dim), bd)


def _odd_out_kernel(h_ref, of_ref, ob_ref, gate_in_ref, w_ref, gate_ref, ng_ref, o_ref):
    o = of_ref[0].astype(F32) + ob_ref[0].astype(F32)
    y = _rms(o, ng_ref[...]) * _silu(gate_in_ref[0].astype(F32))
    o_ref[0] = h_ref[0] + gate_ref[0] * _dot(y.astype(BF16), w_ref[...])


def _odd_out(h, o_f, o_b, p, gate_block, w_out, gate, norm_g, tm):
    b, t, d = h.shape
    vd = o_f.shape[2]
    big = pl.BlockSpec((1, tm, d), lambda bi, i: (bi, i, 0))
    val = pl.BlockSpec((1, tm, vd), lambda bi, i: (bi, i, 0))
    return pl.pallas_call(
        _odd_out_kernel,
        grid=(b, t // tm),
        in_specs=[big, val, val,
                  pl.BlockSpec((1, tm, vd), lambda bi, i: (bi, i, gate_block)),
                  pl.BlockSpec(w_out.shape, lambda bi, i: (0, 0)),
                  pl.BlockSpec((1, 1, d), lambda bi, i: (bi, 0, 0)),
                  pl.BlockSpec((1, vd), lambda bi, i: (0, 0))],
        out_specs=big,
        out_shape=jax.ShapeDtypeStruct((b, t, d), F32),
        compiler_params=_params("parallel", "parallel"),
        name="odd_out",
    )(h, o_f, o_b, p, w_out, gate, norm_g.reshape(1, vd))


def _hgrn_terms(qp, fp, ip, lb, reverse):
    c = qp.shape[0]
    incl, _ = _order_masks(c, reverse)
    q = _silu(qp.astype(F32))
    forget = lb + (1.0 - lb) * _sigmoid(fp.astype(F32))
    k = 1.0 - forget
    cum = _dot_sel_l(incl.astype(BF16), jnp.log(forget))
    last = 0 if reverse else c - 1
    tot = cum[last:last + 1, :]
    mid = cum[c // 2:c // 2 + 1, :]
    return (q * jnp.exp(cum - mid), k * jnp.exp(mid - cum), q * jnp.exp(cum), k * jnp.exp(tot - cum),
            ip, jnp.exp(tot))


def _hgrn_scan_kernel(qf, ff, vf, qb, fb, vb, hl_ref, s0_ref, of_ref, ob_ref, sfin_ref, s_scr,
                      *, layer, heads):
    i = pl.program_id(1)

    @pl.when(i == 0)
    def _():
        s_scr[...] = s0_ref[0]

    hl = hl_ref[...]
    e = jnp.exp(hl - hl.max(axis=0, keepdims=True))
    sm = e / e.sum(axis=0, keepdims=True)
    lb = jnp.zeros_like(sm[0:1])
    for j in range(1, layer + 1):
        lb = lb + sm[j:j + 1]

    c = qf.shape[1]
    fw = _hgrn_terms(qf[0], ff[0], vf[0], lb, False)
    bw = _hgrn_terms(qb[0], fb[0], vb[0], lb, True)
    q_mid, k_mid, q_cum, k_fin, v, wtot = [
        jnp.stack([z[:, h * HG_KEY:(h + 1) * HG_KEY] for z in (zf, zb) for h in range(heads)])
        for zf, zb in zip(fw, bw)]
    t = lax.broadcasted_iota(jnp.int32, (2 * heads, c, c), 1)
    j = lax.broadcasted_iota(jnp.int32, (2 * heads, c, c), 2)
    rev = lax.broadcasted_iota(jnp.int32, (2 * heads, c, c), 0) >= heads
    incl = jnp.where(rev, t - j, j - t) <= 0
    s = s_scr[...]
    att = jnp.where(incl, _bmm_nt(q_mid, k_mid), 0.0)
    o = _bmm(att, v) + _bmm_nt(q_cum, s)
    s_scr[...] = s * wtot + _bmm_tn(v, k_fin)
    of_ref[0] = jnp.concatenate([o[h] for h in range(heads)], axis=1).astype(of_ref.dtype)
    ob_ref[0] = jnp.concatenate([o[heads + h] for h in range(heads)], axis=1).astype(ob_ref.dtype)

    @pl.when(i == pl.num_programs(1) - 1)
    def _():
        sfin_ref[0] = s_scr[...]


def _hgrn_scan(p, hg_lower, s0, layer, heads):
    b, t, _ = p.shape
    n = t // CHUNK
    width = heads * HG_KEY
    depth = hg_lower.shape[0]

    def fwd(blk):
        return pl.BlockSpec((1, CHUNK, width), lambda bi, i: (bi, i, blk))

    def bwd(blk):
        return pl.BlockSpec((1, CHUNK, width), lambda bi, i: (bi, n - 1 - i, blk))

    st = pl.BlockSpec((1, 2 * heads, HG_KEY, HG_KEY), lambda bi, i: (bi, 0, 0, 0))
    o_sds = jax.ShapeDtypeStruct((b, t, width), BF16)
    return pl.pallas_call(
        functools.partial(_hgrn_scan_kernel, layer=layer, heads=heads),
        grid=(b, n),
        in_specs=[fwd(0), fwd(1), fwd(3), bwd(0), bwd(2), bwd(3),
                  pl.BlockSpec((depth, width), lambda bi, i: (0, 0)), st],
        out_specs=[fwd(0), bwd(0), st],
        out_shape=[o_sds, o_sds, jax.ShapeDtypeStruct((b, 2 * heads, HG_KEY, HG_KEY), F32)],
        scratch_shapes=[pltpu.VMEM((2 * heads, HG_KEY, HG_KEY), F32)],
        compiler_params=_params("parallel", "arbitrary"),
        name="hgrn_scan",
    )(p, p, p, p, p, p, hg_lower, s0)


def _row_tile(t, target):
    return target if t % target == 0 else t


def kernel(x, c, ctx, c_ctx, norm_mix_g, norm_mlp_g, ada_w, ada_b, mlp_w1, mlp_w2, ev_w_in,
           ev_w_out, rw_mu_prev, rw_mu_next, rw_w0, rw_w_up, rw_a0, rw_a_up, rw_g_up, rw_k_k,
           rw_k_a, rw_r_k, rw_ln_g, rw_ln_b, na_rpb, od_w_in, od_w_out, hg_lower, hg_norm_g,
           final_norm_g):
    b, t, d = x.shape
    l = ctx.shape[1]
    depth = ada_w.shape[0]
    rw_dim = rw_k_k.shape[1]
    rw_heads = rw_dim // HEAD
    rw_cols = rw_mu_prev.shape[1]
    na_dim = d - rw_dim
    hg_heads = d // HG_KEY

    pad_rows = -(b + 1) % SUBLANES
    c_rows = jnp.concatenate([c, c_ctx[None], jnp.zeros((pad_rows, d), F32)], axis=0)
    mods = _ada(c_rows, ada_w, ada_b)

    def mod_vecs(layer):
        m = mods[layer]
        lat = [m[:b, j * d:(j + 1) * d].reshape(b, 1, d) for j in range(6)]
        cx = [jnp.broadcast_to(m[b, j * d:(j + 1) * d].reshape(1, 1, d), (b, 1, d)) for j in range(6)]
        return lat, cx

    eye = jnp.arange(rw_dim) // HEAD
    bd = (eye[:, None] == eye[None, :]).astype(BF16)

    tm_lat = _row_tile(t, 512)
    tm_ctx = _row_tile(l, 256)
    h_lat, h_ctx = x, ctx
    for layer in range(depth):
        last = layer == depth - 1
        (sh1, s1, g1, sh2, s2, g2), (csh1, cs1, cg1, csh2, cs2, cg2) = mod_vecs(layer)
        if layer % 2 == 0:
            e = layer // 2
            w_in = ev_w_in[e].astype(BF16)
            n_in = w_in.shape[1]
            p_lat = _norm_mod_matmul(h_lat, norm_mix_g[layer], sh1, s1, w_in, tm_lat, n_in)
            p_ctx = _norm_mod_matmul(h_ctx, norm_mix_g[layer], csh1, cs1, w_in, tm_ctx, n_in)
            prm = dict(
                mu_prev=rw_mu_prev[e][None], mu_next=rw_mu_next[e][None],
                w0=rw_w0[e].reshape(1, 2 * rw_dim),
                w_up=jnp.concatenate([rw_w_up[e, 0], rw_w_up[e, 1]], axis=1),
                a0=rw_a0[e].reshape(1, 2 * rw_dim),
                a_up=jnp.concatenate([rw_a_up[e, 0], rw_a_up[e, 1]], axis=1),
                g_up=rw_g_up[e], k_k=rw_k_k[e][None], k_a=rw_k_a[e][None],
                r_k=rw_r_k[e].reshape(1, rw_dim), bd=bd)
            s_zero = jnp.zeros((b, 2 * rw_heads, HEAD, HEAD), F32)
            prep_c = _rwkv_prep(p_ctx, prm, tm_ctx)
            oc_f, oc_b, s_ctx = _rwkv_scan(prep_c[:9], s_zero)
            prep_l = _rwkv_prep(p_lat, prm, tm_lat)
            ol_f, ol_b, _ = _rwkv_scan(prep_l[:9], s_ctx)
            na_lat = _natten(p_lat, p_ctx, _natten_bias(na_rpb[e]), rw_cols, na_dim)
            w_out = ev_w_out[e].astype(BF16)
            h_lat = _even_out(h_lat, ol_f, ol_b, prep_l[10], prep_l[9], na_lat, w_out, g1,
                              rw_ln_g[e], rw_ln_b[e], bd, tm_lat)
            if not last:
                na_ctx = _ctx_attn(p_ctx, rw_cols, na_dim)
                h_ctx = _even_out(h_ctx, oc_f, oc_b, prep_c[10], prep_c[9], na_ctx, w_out, cg1,
                                  rw_ln_g[e], rw_ln_b[e], bd, tm_ctx)
        else:
            o = layer // 2
            w_in = od_w_in[o].astype(BF16)
            n_in = w_in.shape[1]
            p_lat = _norm_mod_matmul(h_lat, norm_mix_g[layer], sh1, s1, w_in, tm_lat, n_in)
            p_ctx = _norm_mod_matmul(h_ctx, norm_mix_g[layer], csh1, cs1, w_in, tm_ctx, n_in)
            s_zero = jnp.zeros((b, 2 * hg_heads, HG_KEY, HG_KEY), F32)
            oc_f, oc_b, s_ctx = _hgrn_scan(p_ctx, hg_lower, s_zero, layer, hg_heads)
            ol_f, ol_b, _ = _hgrn_scan(p_lat, hg_lower, s_ctx, layer, hg_heads)
            w_out = od_w_out[o].astype(BF16)
            gate_block = (n_in - d) // d
            h_lat = _odd_out(h_lat, ol_f, ol_b, p_lat, gate_block, w_out, g1, hg_norm_g[o], tm_lat)
            if not last:
                h_ctx = _odd_out(h_ctx, oc_f, oc_b, p_ctx, gate_block, w_out, cg1, hg_norm_g[o], tm_ctx)
        w1 = mlp_w1[layer].astype(BF16)
        w2 = mlp_w2[layer].astype(BF16)
        h_lat = _mlp(h_lat, norm_mlp_g[layer], sh2, s2, g2, w1, w2, final_norm_g,
                     _row_tile(t, 1024), 512, last)
        if not last:
            h_ctx = _mlp(h_ctx, norm_mlp_g[layer], csh2, cs2, cg2, w1, w2, final_norm_g, tm_ctx, 512, False)
    return h_lat
```

```python
import functools
import math

import jax
import jax.numpy as jnp
import numpy as np
from jax import lax
from jax.experimental import pallas as pl
from jax.experimental.pallas import tpu as pltpu

F32 = jnp.float32
BF16 = jnp.bfloat16
HIGHEST = lax.Precision.HIGHEST

NORM_EPS = 1e-6
RW_GN_EPS = 64e-5
HEAD = 64
GRID_W = 64
NA_ROWS = 8
NA_COLS = 16
HG_KEY = 128
CHUNK = 64
SCAN_BATCH = 2
MASK_BIAS = -1e30
DECAY_SCALE = math.exp(-0.5)
SUBLANES = 8
HALO_ROWS = 16
VMEM_LIMIT = 52 * 1024 * 1024


def _params(*sem):
    return pltpu.CompilerParams(dimension_semantics=sem, vmem_limit_bytes=VMEM_LIMIT)


def _dot(a, b):
    return jnp.dot(a, b, preferred_element_type=F32)


def _dot_nt(a, b):
    return lax.dot_general(a, b, (((1,), (1,)), ((), ())), preferred_element_type=F32)


def _dot_tn(a, b):
    return lax.dot_general(a, b, (((0,), (0,)), ((), ())), preferred_element_type=F32)


def _mm(a, b):
    return _dot(a.astype(BF16), b.astype(BF16))


def _mm_nt(a, b):
    return _dot_nt(a.astype(BF16), b.astype(BF16))


def _mm_tn(a, b):
    return _dot_tn(a.astype(BF16), b.astype(BF16))


def _bmm(a, b):
    return lax.dot_general(a.astype(BF16), b.astype(BF16), (((2,), (1,)), ((0,), (0,))),
                           preferred_element_type=F32)


def _bmm_nt(a, b):
    return lax.dot_general(a.astype(BF16), b.astype(BF16), (((2,), (2,)), ((0,), (0,))),
                           preferred_element_type=F32)


def _bmm_tn(a, b):
    return lax.dot_general(a.astype(BF16), b.astype(BF16), (((1,), (1,)), ((0,), (0,))),
                           preferred_element_type=F32)


def _split3(x):
    x0 = x.astype(BF16)
    r = x - x0.astype(F32)
    x1 = r.astype(BF16)
    x2 = (r - x1.astype(F32)).astype(BF16)
    return x0, x1, x2


def _dot_sel_r(x, sel):
    hi = x.astype(BF16)
    lo = (x - hi.astype(F32)).astype(BF16)
    return _dot(hi, sel) + _dot(lo, sel)


def _dot_sel_l(sel, x):
    x0, x1, x2 = _split3(x)
    return _dot(sel, x0) + (_dot(sel, x1) + _dot(sel, x2))


def _sigmoid(x):
    return 1.0 / (1.0 + jnp.exp(-x))


def _silu(x):
    return x * _sigmoid(x)


def _rms(x, g):
    return x * lax.rsqrt(jnp.mean(x * x, axis=-1, keepdims=True) + NORM_EPS) * g


def _order_masks(n, reverse):
    t = lax.broadcasted_iota(jnp.int32, (n, n), 0)
    i = lax.broadcasted_iota(jnp.int32, (n, n), 1)
    if reverse:
        return i >= t, i > t
    return i <= t, i < t


def _ada_kernel(c_ref, w_ref, b_ref, o_ref):
    s = _silu(c_ref[...])
    o_ref[0] = jnp.dot(s, w_ref[0], precision=HIGHEST, preferred_element_type=F32) + b_ref[0]


def _ada(c_rows, ada_w, ada_b):
    depth, d, n = ada_w.shape
    rows = c_rows.shape[0]
    tn = 512
    return pl.pallas_call(
        _ada_kernel,
        grid=(depth, n // tn),
        in_specs=[pl.BlockSpec((rows, d), lambda l, j: (0, 0)),
                  pl.BlockSpec((1, d, tn), lambda l, j: (l, 0, j)),
                  pl.BlockSpec((1, 1, tn), lambda l, j: (l, 0, j))],
        out_specs=pl.BlockSpec((1, rows, tn), lambda l, j: (l, 0, j)),
        out_shape=jax.ShapeDtypeStruct((depth, rows, n), F32),
        compiler_params=_params("parallel", "parallel"),
        name="ada",
    )(c_rows, ada_w, ada_b.reshape(depth, 1, n))


def _nmm_kernel(x_ref, g_ref, sh_ref, sc_ref, w_ref, o_ref, a_scr, *, silu_cols):
    @pl.when(pl.program_id(2) == 0)
    def _():
        a = _rms(x_ref[0], g_ref[...]) * (1.0 + sc_ref[0]) + sh_ref[0]
        a_scr[...] = a.astype(BF16)

    if silu_cols:
        o_ref[0, :, :silu_cols] = _silu(_dot(a_scr[...], w_ref[:, :silu_cols])).astype(o_ref.dtype)
        o_ref[0, :, silu_cols:] = _dot(a_scr[...], w_ref[:, silu_cols:]).astype(o_ref.dtype)
    else:
        o_ref[0] = _dot(a_scr[...], w_ref[...]).astype(o_ref.dtype)


def _norm_mod_matmul(x, g, shift, scale, w, tm, tn, silu_cols=0):
    b, t, d = x.shape
    n = w.shape[1]
    assert silu_cols == 0 or tn == n
    return pl.pallas_call(
        functools.partial(_nmm_kernel, silu_cols=silu_cols),
        grid=(b, t // tm, n // tn),
        in_specs=[pl.BlockSpec((1, tm, d), lambda bi, i, j: (bi, i, 0)),
                  pl.BlockSpec((1, d), lambda bi, i, j: (0, 0)),
                  pl.BlockSpec((1, 1, d), lambda bi, i, j: (bi, 0, 0)),
                  pl.BlockSpec((1, 1, d), lambda bi, i, j: (bi, 0, 0)),
                  pl.BlockSpec((d, tn), lambda bi, i, j: (0, j))],
        out_specs=pl.BlockSpec((1, tm, tn), lambda bi, i, j: (bi, i, j)),
        out_shape=jax.ShapeDtypeStruct((b, t, n), BF16),
        scratch_shapes=[pltpu.VMEM((tm, d), BF16)],
        compiler_params=_params("parallel", "parallel", "arbitrary"),
        name="norm_mod_matmul",
    )(x, g.reshape(1, d), shift, scale, w)


def _mlp_kernel(h_ref, g_ref, sh_ref, sc_ref, gate_ref, w1_ref, w2_ref, fg_ref, o_ref,
                a_scr, acc_scr, *, final_norm):
    j = pl.program_id(2)

    @pl.when(j == 0)
    def _():
        a = _rms(h_ref[0], g_ref[...]) * (1.0 + sc_ref[0]) + sh_ref[0]
        a_scr[...] = a.astype(BF16)
        acc_scr[...] = jnp.zeros_like(acc_scr)

    hid = jnp.maximum(_dot(a_scr[...], w1_ref[...]), 0.0)
    acc_scr[...] += _dot((hid * hid).astype(BF16), w2_ref[...])

    @pl.when(j == pl.num_programs(2) - 1)
    def _():
        out = h_ref[0] + gate_ref[0] * acc_scr[...]
        if final_norm:
            out = _rms(out, fg_ref[...])
        o_ref[0] = out


def _mlp(h, g, shift, scale, gate, w1, w2, final_g, tm, th, final_norm):
    b, t, d = h.shape
    hid = w1.shape[1]
    vec = pl.BlockSpec((1, 1, d), lambda bi, i, j: (bi, 0, 0))
    par = pl.BlockSpec((1, d), lambda bi, i, j: (0, 0))
    return pl.pallas_call(
        functools.partial(_mlp_kernel, final_norm=final_norm),
        grid=(b, t // tm, hid // th),
        in_specs=[pl.BlockSpec((1, tm, d), lambda bi, i, j: (bi, i, 0)),
                  par, vec, vec, vec,
                  pl.BlockSpec((d, th), lambda bi, i, j: (0, j)),
                  pl.BlockSpec((th, d), lambda bi, i, j: (j, 0)),
                  par],
        out_specs=pl.BlockSpec((1, tm, d), lambda bi, i, j: (bi, i, 0)),
        out_shape=jax.ShapeDtypeStruct((b, t, d), F32),
        scratch_shapes=[pltpu.VMEM((tm, d), BF16), pltpu.VMEM((tm, d), F32)],
        compiler_params=_params("parallel", "parallel", "arbitrary"),
        name="mlp",
    )(h, g.reshape(1, d), shift, scale, gate, w1, w2, final_g.reshape(1, d))


def _rwkv_prep_kernel(p_ref, pp_ref, pn_ref, mup_ref, mun_ref, w0_ref, wup_ref, a0_ref, aup_ref,
                      gup_ref, kk_ref, ka_ref, rk_ref, bd_ref,
                      r_o, v_o, kkn_o, lw0_o, lw1_o, kd0_o, kd1_o, b0_o, b1_o, g_o, bonus_o,
                      *, rw_dim, w_lora, a_lora):
    i = pl.program_id(1)
    n = pl.num_programs(1)
    p = p_ref[0].astype(F32)
    tm = p.shape[0]
    row = lax.broadcasted_iota(jnp.int32, (SUBLANES, 1), 0)
    first = jnp.where(i > 0, pp_ref[0].astype(F32)[HALO_ROWS - 1:HALO_ROWS, :], 0.0)
    last = jnp.where(i < n - 1, pn_ref[0].astype(F32)[0:1, :], 0.0)
    prev = pltpu.roll(p, 1, axis=0)
    prev = jnp.concatenate([jnp.where(row == 0, first, prev[:SUBLANES]), prev[SUBLANES:]], axis=0)
    nxt = pltpu.roll(p, tm - 1, axis=0)
    nxt = jnp.concatenate([nxt[:tm - SUBLANES],
                           jnp.where(row == SUBLANES - 1, last, nxt[tm - SUBLANES:])], axis=0)
    ps = p + (prev - p) * mup_ref[...] + (nxt - p) * mun_ref[...]

    r = ps[:, :rw_dim]
    k = ps[:, rw_dim:2 * rw_dim]
    v = ps[:, 2 * rw_dim:3 * rw_dim]
    off = 3 * rw_dim
    w_lo = ps[:, off:off + w_lora]
    a_lo = ps[:, off + w_lora:off + w_lora + a_lora]
    g_lo = ps[:, off + w_lora + a_lora:]

    w_pre = w0_ref[...] + _mm(jnp.tanh(w_lo), wup_ref[...])
    log_decay = -DECAY_SCALE * _sigmoid(w_pre)
    a = _sigmoid(a0_ref[...] + _mm(a_lo, aup_ref[...]))
    g = _mm(_sigmoid(g_lo), gup_ref[...])

    bd = bd_ref[...]
    kk = k * kk_ref[...]
    kk = kk / jnp.maximum(jnp.sqrt(_dot_sel_r(kk * kk, bd)), 1e-12)
    a_f = a[:, :rw_dim]
    a_b = a[:, rw_dim:]
    ka = ka_ref[...]
    kd_f = k * (1.0 + (a_f - 1.0) * ka)
    kd_b = k * (1.0 + (a_b - 1.0) * ka)
    bonus = _dot_sel_r(r * (kd_f + kd_b) * rk_ref[...], bd) * v

    r_o[0] = r.astype(BF16)
    v_o[0] = v.astype(BF16)
    kkn_o[0] = kk.astype(BF16)
    lw0_o[0] = log_decay[:, :rw_dim]
    lw1_o[0] = log_decay[:, rw_dim:]
    kd0_o[0] = kd_f.astype(BF16)
    kd1_o[0] = kd_b.astype(BF16)
    b0_o[0] = (kk * a_f).astype(BF16)
    b1_o[0] = (kk * a_b).astype(BF16)
    g_o[0] = g.astype(BF16)
    bonus_o[0] = bonus.astype(BF16)


def _rwkv_prep(p, prm, tm):
    b, t, _ = p.shape
    rw_dim = prm["k_k"].shape[1]
    w_lora = prm["w_up"].shape[0]
    a_lora = prm["a_up"].shape[0]
    g_lora = prm["g_up"].shape[0]
    cols = 3 * rw_dim + w_lora + a_lora + g_lora
    n_halo = t // HALO_ROWS
    tm_halo = tm // HALO_ROWS

    def full(a):
        return pl.BlockSpec(a.shape, lambda bi, i: (0,) * a.ndim)

    consts = [prm["mu_prev"], prm["mu_next"], prm["w0"], prm["w_up"], prm["a0"], prm["a_up"],
              prm["g_up"], prm["k_k"], prm["k_a"], prm["r_k"], prm["bd"]]
    out_spec = pl.BlockSpec((1, tm, rw_dim), lambda bi, i: (bi, i, 0))
    out_dtypes = [BF16, BF16, BF16, F32, F32, BF16, BF16, BF16, BF16, BF16, BF16]
    return pl.pallas_call(
        functools.partial(_rwkv_prep_kernel, rw_dim=rw_dim, w_lora=w_lora, a_lora=a_lora),
        grid=(b, t // tm),
        in_specs=[pl.BlockSpec((1, tm, cols), lambda bi, i: (bi, i, 0)),
                  pl.BlockSpec((1, HALO_ROWS, cols),
                               lambda bi, i: (bi, jnp.maximum(i * tm_halo - 1, 0), 0)),
                  pl.BlockSpec((1, HALO_ROWS, cols),
                               lambda bi, i: (bi, jnp.minimum((i + 1) * tm_halo, n_halo - 1), 0))]
                 + [full(a) for a in consts],
        out_specs=[out_spec] * 11,
        out_shape=[jax.ShapeDtypeStruct((b, t, rw_dim), dt) for dt in out_dtypes],
        compiler_params=_params("parallel", "parallel"),
        name="rwkv_prep",
    )(p, p, p, *consts)


def _rwkv_chunk(x, v, y, bk, s, wtot, incl2, strict2):
    n = x.shape[0]
    c = v.shape[1]
    lanes = v.shape[2]
    y2 = jnp.concatenate([y, s], axis=1)
    gh = _bmm_nt(x, jnp.stack([y2[p // 2] for p in range(n)]))
    a2 = jnp.where(strict2, gh[:, :c, :2 * c], 0.0)
    bmat = jnp.where(incl2, gh[:, c:, :2 * c], 0.0)
    u = -(gh[:, :c, 2 * c:] + _bmm(a2, jnp.concatenate([jnp.zeros_like(v), v], axis=1)))
    nm = -a2[:, :, :c]
    for _ in range(c.bit_length() - 2):
        w = _bmm(nm, jnp.concatenate([u, nm], axis=2))
        u = u + w[:, :, :lanes]
        nm = w[:, :, lanes:]
    u = u + _bmm(nm, u)
    uv = jnp.concatenate([u, v], axis=1)
    o = gh[:, c:, 2 * c:] + _bmm(bmat, uv)
    uv_pair = jnp.stack([uv[p] + uv[p + 1] for p in range(0, n, 2)])
    row = lax.broadcasted_iota(jnp.int32, s.shape, 1) // HEAD
    col = lax.broadcasted_iota(jnp.int32, s.shape, 2) // HEAD
    s_new = s * wtot + jnp.where(row == col, _bmm_tn(uv_pair, bk), 0.0)
    return o, s_new


def _rwkv_decay_terms(r, v, kk, lw, kd, bm, reverse):
    c = r.shape[0]
    r, v, kk, kd, bm = (z.astype(F32) for z in (r, v, kk, kd, bm))
    incl, _ = _order_masks(c, reverse)
    cum = _dot_sel_l(incl.astype(BF16), lw)
    last = 0 if reverse else c - 1
    tot = cum[last:last + 1, :]
    e_neg = jnp.exp(-cum)
    e_fin = jnp.exp(tot - cum)
    return (r * jnp.exp(cum), kk * jnp.exp(cum - lw), bm * e_neg, kd * e_neg, bm * e_fin,
            kd * e_fin, v, jnp.exp(tot))


def _rwkv_scan_kernel(rf, vf, kf, lwf, kdf, bf, rb, vb, kb, lwb, kdb, bb, s0_ref,
                      of_ref, ob_ref, sfin_ref, s_scr, *, heads):
    i = pl.program_id(1)

    nb = rf.shape[0]
    c = rf.shape[1]
    pair = 2 * HEAD

    @pl.when(i == 0)
    def _():
        s_scr[...] = s0_ref[...].reshape(s_scr.shape)

    own = [lax.broadcasted_iota(jnp.int32, (1, pair), 1) // HEAD == half for half in range(2)]
    xs, vs, ys, bks, ws = [], [], [], [], []
    for bi in range(nb):
        for terms in (_rwkv_decay_terms(rf[bi], vf[bi], kf[bi], lwf[bi], kdf[bi], bf[bi], False),
                      _rwkv_decay_terms(rb[bi], vb[bi], kb[bi], lwb[bi], kdb[bi], bb[bi], True)):
            rd, kkd, binv, kinv, bfin, kfin, v, wtot = terms
            for p in range(heads // 2):
                sl = slice(p * pair, (p + 1) * pair)
                ys.append(jnp.concatenate([binv[:, sl], kinv[:, sl]], axis=0))
                bks.append(jnp.concatenate([bfin[:, sl], kfin[:, sl]], axis=0))
                ws.append(wtot[:, sl])
                for m in own:
                    xs.append(jnp.where(m, jnp.concatenate([kkd[:, sl], rd[:, sl]], axis=0), 0.0))
                    vs.append(jnp.where(m, v[:, sl], 0.0))
    n = nb * 2 * heads
    t = lax.broadcasted_iota(jnp.int32, (n, c, 2 * c), 1)
    j = lax.broadcasted_iota(jnp.int32, (n, c, 2 * c), 2) % c
    rev = lax.broadcasted_iota(jnp.int32, (n, c, 2 * c), 0) // heads % 2 == 1
    ahead = jnp.where(rev, t - j, j - t)
    o, s_new = _rwkv_chunk(jnp.stack(xs), jnp.stack(vs), jnp.stack(ys), jnp.stack(bks), s_scr[...],
                           jnp.stack(ws), ahead <= 0, ahead < 0)
    s_scr[...] = s_new
    for bi in range(nb):
        base = bi * 2 * heads
        of_ref[bi] = jnp.concatenate([o[base + h] + o[base + h + 1] for h in range(0, heads, 2)],
                                     axis=1).astype(of_ref.dtype)
        ob_ref[bi] = jnp.concatenate([o[base + heads + h] + o[base + heads + h + 1]
                                      for h in range(0, heads, 2)], axis=1).astype(ob_ref.dtype)

    @pl.when(i == pl.num_programs(1) - 1)
    def _():
        sfin_ref[...] = s_scr[...].reshape(sfin_ref.shape)


def _rwkv_scan(prep, s0):
    r, v, kk, lw0, lw1, kd0, kd1, b0, b1 = prep
    b, t, rw_dim = r.shape
    heads = rw_dim // HEAD
    n = t // CHUNK
    nb = SCAN_BATCH if b % SCAN_BATCH == 0 else 1
    fwd = pl.BlockSpec((nb, CHUNK, rw_dim), lambda bi, i: (bi, i, 0))
    bwd = pl.BlockSpec((nb, CHUNK, rw_dim), lambda bi, i: (bi, n - 1 - i, 0))
    st = pl.BlockSpec((nb,) + s0.shape[1:], lambda bi, i: (bi, 0, 0, 0))
    o_sds = jax.ShapeDtypeStruct((b, t, rw_dim), BF16)
    return pl.pallas_call(
        functools.partial(_rwkv_scan_kernel, heads=heads),
        grid=(b // nb, n),
        in_specs=[fwd] * 6 + [bwd] * 6 + [st],
        out_specs=[fwd, bwd, st],
        out_shape=[o_sds, o_sds, jax.ShapeDtypeStruct(s0.shape, F32)],
        scratch_shapes=[pltpu.VMEM((nb * s0.shape[1],) + s0.shape[2:], F32)],
        compiler_params=_params("parallel", "arbitrary"),
        name="rwkv_scan",
    )(r, v, kk, lw0, kd0, b0, r, v, kk, lw1, kd1, b1, s0)


def _softmax_pv(s_list, v_list):
    m = s_list[0].max(axis=-1, keepdims=True)
    for s in s_list[1:]:
        m = jnp.maximum(m, s.max(axis=-1, keepdims=True))
    num = None
    den = None
    for s, v in zip(s_list, v_list):
        p = jnp.exp(s - m)
        d = p.sum(axis=-1, keepdims=True)
        o = _dot(p.astype(BF16), v)
        num = o if num is None else num + o
        den = d if den is None else den + d
    return num / den


def _natten_kernel(q_ref, k_ref, v_ref, kc_ref, vc_ref, bias_ref, o_ref, *, rows, scale, rb):
    kc = kc_ref[0]
    vc = vc_ref[0]
    l = kc.shape[0]
    nwin = NA_ROWS * GRID_W
    head0 = lax.broadcasted_iota(jnp.int32, (1, 2 * HEAD), 1) < HEAD

    def body(it, carry):
        base = it * rb
        start = pl.multiple_of(base * GRID_W, rb * GRID_W)
        q = q_ref[0, pl.ds(start, rb * GRID_W), :] * scale
        zero = jnp.zeros_like(q)
        q0 = jnp.where(head0, q, zero)
        q1 = jnp.where(head0, zero, q)
        qs, kws, vws, bs = [], [], [], []
        for j in range(rb):
            r = base + j
            r0 = jnp.clip(r - NA_ROWS // 2, 0, rows - NA_ROWS)
            rows_j = slice(j * GRID_W, (j + 1) * GRID_W)
            qs.append(jnp.concatenate([q0[rows_j], q1[rows_j]], axis=0))
            win = pl.ds(pl.multiple_of(r0 * GRID_W, GRID_W), nwin)
            kws.append(k_ref[0, win, :])
            vws.append(v_ref[0, win, :])
            bs.append(bias_ref[0, r - r0])
        qs = jnp.stack(qs)
        s_win = _bmm_nt(qs, jnp.stack(kws)) + jnp.stack(bs)
        s_ctx = _dot_nt(qs.reshape(rb * 2 * GRID_W, 2 * HEAD), kc).reshape(rb, 2 * GRID_W, l)
        m = jnp.maximum(s_win.max(axis=-1, keepdims=True), s_ctx.max(axis=-1, keepdims=True))
        p_win = jnp.exp(s_win - m)
        p_ctx = jnp.exp(s_ctx - m)
        den = p_win.sum(axis=-1, keepdims=True) + p_ctx.sum(axis=-1, keepdims=True)
        o = _bmm(p_win, jnp.stack(vws))
        o = o + _dot(p_ctx.reshape(rb * 2 * GRID_W, l).astype(BF16), vc).reshape(rb, 2 * GRID_W, 2 * HEAD)
        o = o / den
        out = jnp.concatenate([jnp.where(head0, o[j, :GRID_W], o[j, GRID_W:]) for j in range(rb)], axis=0)
        o_ref[0, pl.ds(start, rb * GRID_W), :] = out.astype(o_ref.dtype)
        return carry

    lax.fori_loop(0, rows // rb, body, 0)


def _natten(p_lat, p_ctx, bias, rw_cols, na_dim):
    b, t, _ = p_lat.shape
    l = p_ctx.shape[1]
    pairs = na_dim // (2 * HEAD)
    qb = rw_cols // (2 * HEAD)
    kb = qb + pairs
    vb = kb + pairs
    rows = t // GRID_W

    def lat(off):
        return pl.BlockSpec((1, t, 2 * HEAD), lambda bi, hp: (bi, 0, off + hp))

    def ctx(off):
        return pl.BlockSpec((1, l, 2 * HEAD), lambda bi, hp: (bi, 0, off + hp))

    rb = 4 if rows % 4 == 0 else 1
    return pl.pallas_call(
        functools.partial(_natten_kernel, rows=rows, scale=HEAD ** -0.5, rb=rb),
        grid=(b, pairs),
        in_specs=[lat(qb), lat(kb), lat(vb), ctx(kb), ctx(vb),
                  pl.BlockSpec((1, NA_ROWS, 2 * GRID_W, NA_ROWS * GRID_W), lambda bi, hp: (hp, 0, 0, 0))],
        out_specs=pl.BlockSpec((1, t, 2 * HEAD), lambda bi, hp: (bi, 0, hp)),
        out_shape=jax.ShapeDtypeStruct((b, t, na_dim), BF16),
        compiler_params=_params("parallel", "parallel"),
        name="natten",
    )(p_lat, p_lat, p_lat, p_ctx, p_ctx, bias)


def _ctx_attn_kernel(q_ref, k_ref, v_ref, o_ref, *, scale):
    q = q_ref[0] * scale
    k = k_ref[0]
    v = v_ref[0]
    outs = []
    for hh in range(2):
        sl = slice(hh * HEAD, (hh + 1) * HEAD)
        outs.append(_softmax_pv([_dot_nt(q[:, sl], k[:, sl])], [v[:, sl]]))
    o_ref[0] = jnp.concatenate(outs, axis=1).astype(o_ref.dtype)


def _ctx_attn(p_ctx, rw_cols, na_dim):
    b, l, _ = p_ctx.shape
    pairs = na_dim // (2 * HEAD)
    qb = rw_cols // (2 * HEAD)

    def blk(off):
        return pl.BlockSpec((1, l, 2 * HEAD), lambda bi, hp: (bi, 0, off + hp))

    return pl.pallas_call(
        functools.partial(_ctx_attn_kernel, scale=HEAD ** -0.5),
        grid=(b, pairs),
        in_specs=[blk(qb), blk(qb + pairs), blk(qb + 2 * pairs)],
        out_specs=pl.BlockSpec((1, l, 2 * HEAD), lambda bi, hp: (bi, 0, hp)),
        out_shape=jax.ShapeDtypeStruct((b, l, na_dim), BF16),
        compiler_params=_params("parallel", "parallel"),
        name="ctx_attn",
    )(p_ctx, p_ctx, p_ctx)


def _natten_bias(rpb):
    cols = np.arange(GRID_W)
    c0 = np.clip(cols - NA_COLS // 2, 0, GRID_W - NA_COLS)
    valid = (cols[None, :] >= c0[:, None]) & (cols[None, :] < c0[:, None] + NA_COLS)
    col_rel = cols[None, :] - cols[:, None] + NA_COLS - 1
    onehot = (col_rel[:, :, None] == np.arange(2 * NA_COLS - 1)).astype(np.float32)
    toe = jnp.einsum("hrj,qkj->hrqk", rpb.astype(F32), onehot, precision=HIGHEST)
    toe = jnp.where(valid[None, None], toe, MASK_BIAS)
    tab = jnp.stack([toe[:, NA_ROWS - 1 - d:2 * NA_ROWS - 1 - d] for d in range(NA_ROWS)], axis=1)
    tab = tab.transpose(0, 1, 3, 2, 4)
    heads = rpb.shape[0]
    tab = tab.reshape(heads // 2, 2, NA_ROWS, GRID_W, NA_ROWS * GRID_W)
    return tab.transpose(0, 2, 1, 3, 4).reshape(heads // 2, NA_ROWS, 2 * GRID_W, NA_ROWS * GRID_W)


def _even_out_kernel(h_ref, of_ref, ob_ref, bonus_ref, g_ref, na_ref, w_ref, gate_ref,
                     lng_ref, lnb_ref, bd_ref, o_ref):
    bd = bd_ref[...]
    o = of_ref[0].astype(F32) + ob_ref[0].astype(F32)
    inv = 1.0 / HEAD
    mu = _dot_sel_r(o, bd) * inv
    xc = o - mu
    var = _dot_sel_r(xc * xc, bd) * inv
    y = xc * lax.rsqrt(var + RW_GN_EPS) * lng_ref[...] + lnb_ref[...]
    rw = (y + bonus_ref[0].astype(F32)) * g_ref[0].astype(F32)
    cat = jnp.concatenate([rw.astype(BF16), na_ref[0]], axis=1)
    o_ref[0] = h_ref[0] + gate_ref[0] * _dot(cat, w_ref[...])


def _even_out(h, o_f, o_b, bonus, g, na, w_out, gate, ln_g, ln_b, bd, tm):
    b, t, d = h.shape
    rw_dim = o_f.shape[2]
    na_dim = na.shape[2]
    big = pl.BlockSpec((1, tm, d), lambda bi, i: (bi, i, 0))
    half = pl.BlockSpec((1, tm, rw_dim), lambda bi, i: (bi, i, 0))
    par = pl.BlockSpec((1, rw_dim), lambda bi, i: (0, 0))
    return pl.pallas_call(
        _even_out_kernel,
        grid=(b, t // tm),
        in_specs=[big, half, half, half, half,
                  pl.BlockSpec((1, tm, na_dim), lambda bi, i: (bi, i, 0)),
                  pl.BlockSpec(w_out.shape, lambda bi, i: (0, 0)),
                  pl.BlockSpec((1, 1, d), lambda bi, i: (bi, 0, 0)),
                  par, par,
                  pl.BlockSpec(bd.shape, lambda bi, i: (0, 0))],
        out_specs=big,
        out_shape=jax.ShapeDtypeStruct((b, t, d), F32),
        compiler_params=_params("parallel", "parallel"),
        name="even_out",
    )(h, o_f, o_b, bonus, g, na, w_out, gate, ln_g.reshape(1, rw_dim), ln_b.reshape(1, rw_dim), bd)


def _odd_out_kernel(h_ref, of_ref, ob_ref, gate_in_ref, w_ref, gate_ref, ng_ref, o_ref):
    o = of_ref[0].astype(F32) + ob_ref[0].astype(F32)
    y = _rms(o, ng_ref[...]) * _silu(gate_in_ref[0].astype(F32))
    o_ref[0] = h_ref[0] + gate_ref[0] * _dot(y.astype(BF16), w_ref[...])


def _odd_out(h, o_f, o_b, p, gate_block, w_out, gate, norm_g, tm):
    b, t, d = h.shape
    vd = o_f.shape[2]
    big = pl.BlockSpec((1, tm, d), lambda bi, i: (bi, i, 0))
    val = pl.BlockSpec((1, tm, vd), lambda bi, i: (bi, i, 0))
    return pl.pallas_call(
        _odd_out_kernel,
        grid=(b, t // tm),
        in_specs=[big, val, val,
                  pl.BlockSpec((1, tm, vd), lambda bi, i: (bi, i, gate_block)),
                  pl.BlockSpec(w_out.shape, lambda bi, i: (0, 0)),
                  pl.BlockSpec((1, 1, d), lambda bi, i: (bi, 0, 0)),
                  pl.BlockSpec((1, vd), lambda bi, i: (0, 0))],
        out_specs=big,
        out_shape=jax.ShapeDtypeStruct((b, t, d), F32),
        compiler_params=_params("parallel", "parallel"),
        name="odd_out",
    )(h, o_f, o_b, p, w_out, gate, norm_g.reshape(1, vd))


def _hgrn_terms(qp, fp, ip, lb, reverse):
    c = qp.shape[0]
    incl, _ = _order_masks(c, reverse)
    q = qp.astype(F32)
    forget = lb + (1.0 - lb) * _sigmoid(fp.astype(F32))
    k = 1.0 - forget
    cum = _dot_sel_l(incl.astype(BF16), jnp.log(forget))
    last = 0 if reverse else c - 1
    tot = cum[last:last + 1, :]
    mid = cum[c // 2:c // 2 + 1, :]
    q_mid = q * jnp.exp(cum - mid)
    k_mid = k * jnp.exp(mid - cum)
    return q_mid, k_mid, q_mid * jnp.exp(mid), k_mid * jnp.exp(tot - mid), ip, jnp.exp(tot)


def _hgrn_scan_kernel(qf, ff, vf, qb, fb, vb, hl_ref, s0_ref, of_ref, ob_ref, sfin_ref, s_scr,
                      *, layer, heads):
    i = pl.program_id(1)

    @pl.when(i == 0)
    def _():
        s_scr[...] = s0_ref[...].reshape(s_scr.shape)

    hl = hl_ref[...]
    e = jnp.exp(hl - hl.max(axis=0, keepdims=True))
    sm = e / e.sum(axis=0, keepdims=True)
    lb = jnp.zeros_like(sm[0:1])
    for j in range(1, layer + 1):
        lb = lb + sm[j:j + 1]

    nb = qf.shape[0]
    c = qf.shape[1]
    n = nb * 2 * heads
    terms = [tm for bi in range(nb) for tm in (_hgrn_terms(qf[bi], ff[bi], vf[bi], lb, False),
                                               _hgrn_terms(qb[bi], fb[bi], vb[bi], lb, True))]
    q_mid, k_mid, q_cum, k_fin, v, wtot = [
        jnp.stack([z[:, h * HG_KEY:(h + 1) * HG_KEY] for z in zs for h in range(heads)])
        for zs in zip(*terms)]
    t = lax.broadcasted_iota(jnp.int32, (n, c, c), 1)
    j = lax.broadcasted_iota(jnp.int32, (n, c, c), 2)
    rev = lax.broadcasted_iota(jnp.int32, (n, c, c), 0) // heads % 2 == 1
    incl = jnp.where(rev, t - j, j - t) <= 0
    s = s_scr[...]
    att = jnp.where(incl, _bmm_nt(q_mid, k_mid), 0.0)
    o = _bmm(att, v) + _bmm_nt(q_cum, s)
    s_scr[...] = s * wtot + _bmm_tn(v, k_fin)
    for bi in range(nb):
        base = bi * 2 * heads
        of_ref[bi] = jnp.concatenate([o[base + h] for h in range(heads)], axis=1).astype(of_ref.dtype)
        ob_ref[bi] = jnp.concatenate([o[base + heads + h] for h in range(heads)],
                                     axis=1).astype(ob_ref.dtype)

    @pl.when(i == pl.num_programs(1) - 1)
    def _():
        sfin_ref[...] = s_scr[...].reshape(sfin_ref.shape)


def _hgrn_scan(p, hg_lower, s0, layer, heads):
    b, t, _ = p.shape
    n = t // CHUNK
    width = heads * HG_KEY
    depth = hg_lower.shape[0]

    nb = SCAN_BATCH if b % SCAN_BATCH == 0 else 1

    def fwd(blk):
        return pl.BlockSpec((nb, CHUNK, width), lambda bi, i: (bi, i, blk))

    def bwd(blk):
        return pl.BlockSpec((nb, CHUNK, width), lambda bi, i: (bi, n - 1 - i, blk))

    st = pl.BlockSpec((nb, 2 * heads, HG_KEY, HG_KEY), lambda bi, i: (bi, 0, 0, 0))
    o_sds = jax.ShapeDtypeStruct((b, t, width), BF16)
    return pl.pallas_call(
        functools.partial(_hgrn_scan_kernel, layer=layer, heads=heads),
        grid=(b // nb, n),
        in_specs=[fwd(0), fwd(1), fwd(3), bwd(0), bwd(2), bwd(3),
                  pl.BlockSpec((depth, width), lambda bi, i: (0, 0)), st],
        out_specs=[fwd(0), bwd(0), st],
        out_shape=[o_sds, o_sds, jax.ShapeDtypeStruct((b, 2 * heads, HG_KEY, HG_KEY), F32)],
        scratch_shapes=[pltpu.VMEM((nb * 2 * heads, HG_KEY, HG_KEY), F32)],
        compiler_params=_params("parallel", "arbitrary"),
        name="hgrn_scan",
    )(p, p, p, p, p, p, hg_lower, s0)


def _row_tile(t, target):
    return target if t % target == 0 else t


def kernel(x, c, ctx, c_ctx, norm_mix_g, norm_mlp_g, ada_w, ada_b, mlp_w1, mlp_w2, ev_w_in,
           ev_w_out, rw_mu_prev, rw_mu_next, rw_w0, rw_w_up, rw_a0, rw_a_up, rw_g_up, rw_k_k,
           rw_k_a, rw_r_k, rw_ln_g, rw_ln_b, na_rpb, od_w_in, od_w_out, hg_lower, hg_norm_g,
           final_norm_g):
    b, t, d = x.shape
    l = ctx.shape[1]
    depth = ada_w.shape[0]
    rw_dim = rw_k_k.shape[1]
    rw_heads = rw_dim // HEAD
    rw_cols = rw_mu_prev.shape[1]
    na_dim = d - rw_dim
    hg_heads = d // HG_KEY

    pad_rows = -(b + 1) % SUBLANES
    c_rows = jnp.concatenate([c, c_ctx[None], jnp.zeros((pad_rows, d), F32)], axis=0)
    mods = _ada(c_rows, ada_w, ada_b)

    def mod_vecs(layer):
        m = mods[layer]
        lat = [m[:b, j * d:(j + 1) * d].reshape(b, 1, d) for j in range(6)]
        cx = [jnp.broadcast_to(m[b, j * d:(j + 1) * d].reshape(1, 1, d), (b, 1, d)) for j in range(6)]
        return lat, cx

    eye = jnp.arange(rw_dim) // HEAD
    bd = (eye[:, None] == eye[None, :]).astype(BF16)

    tm_lat = _row_tile(t, 512)
    tm_ctx = _row_tile(l, 256)
    h_lat, h_ctx = x, ctx
    for layer in range(depth):
        last = layer == depth - 1
        (sh1, s1, g1, sh2, s2, g2), (csh1, cs1, cg1, csh2, cs2, cg2) = mod_vecs(layer)
        if layer % 2 == 0:
            e = layer // 2
            w_in = ev_w_in[e].astype(BF16)
            n_in = w_in.shape[1]
            p_lat = _norm_mod_matmul(h_lat, norm_mix_g[layer], sh1, s1, w_in, tm_lat, n_in)
            p_ctx = _norm_mod_matmul(h_ctx, norm_mix_g[layer], csh1, cs1, w_in, tm_ctx, n_in)
            prm = dict(
                mu_prev=rw_mu_prev[e][None], mu_next=rw_mu_next[e][None],
                w0=rw_w0[e].reshape(1, 2 * rw_dim),
                w_up=jnp.concatenate([rw_w_up[e, 0], rw_w_up[e, 1]], axis=1),
                a0=rw_a0[e].reshape(1, 2 * rw_dim),
                a_up=jnp.concatenate([rw_a_up[e, 0], rw_a_up[e, 1]], axis=1),
                g_up=rw_g_up[e], k_k=rw_k_k[e][None], k_a=rw_k_a[e][None],
                r_k=rw_r_k[e].reshape(1, rw_dim), bd=bd)
            s_zero = jnp.zeros((b, rw_heads, 2 * HEAD, 2 * HEAD), F32)
            prep_c = _rwkv_prep(p_ctx, prm, tm_ctx)
            oc_f, oc_b, s_ctx = _rwkv_scan(prep_c[:9], s_zero)
            prep_l = _rwkv_prep(p_lat, prm, tm_lat)
            ol_f, ol_b, _ = _rwkv_scan(prep_l[:9], s_ctx)
            na_lat = _natten(p_lat, p_ctx, _natten_bias(na_rpb[e]), rw_cols, na_dim)
            w_out = ev_w_out[e].astype(BF16)
            h_lat = _even_out(h_lat, ol_f, ol_b, prep_l[10], prep_l[9], na_lat, w_out, g1,
                              rw_ln_g[e], rw_ln_b[e], bd, tm_lat)
            if not last:
                na_ctx = _ctx_attn(p_ctx, rw_cols, na_dim)
                h_ctx = _even_out(h_ctx, oc_f, oc_b, prep_c[10], prep_c[9], na_ctx, w_out, cg1,
                                  rw_ln_g[e], rw_ln_b[e], bd, tm_ctx)
        else:
            o = layer // 2
            w_in = od_w_in[o].astype(BF16)
            n_in = w_in.shape[1]
            hg_kdim = hg_heads * HG_KEY
            p_lat = _norm_mod_matmul(h_lat, norm_mix_g[layer], sh1, s1, w_in, tm_lat, n_in, hg_kdim)
            p_ctx = _norm_mod_matmul(h_ctx, norm_mix_g[layer], csh1, cs1, w_in, tm_ctx, n_in, hg_kdim)
            s_zero = jnp.zeros((b, 2 * hg_heads, HG_KEY, HG_KEY), F32)
            oc_f, oc_b, s_ctx = _hgrn_scan(p_ctx, hg_lower, s_zero, layer, hg_heads)
            ol_f, ol_b, _ = _hgrn_scan(p_lat, hg_lower, s_ctx, layer, hg_heads)
            w_out = od_w_out[o].astype(BF16)
            gate_block = (n_in - d) // d
            h_lat = _odd_out(h_lat, ol_f, ol_b, p_lat, gate_block, w_out, g1, hg_norm_g[o], tm_lat)
            if not last:
                h_ctx = _odd_out(h_ctx, oc_f, oc_b, p_ctx, gate_block, w_out, cg1, hg_norm_g[o], tm_ctx)
        w1 = mlp_w1[layer].astype(BF16)
        w2 = mlp_w2[layer].astype(BF16)
        h_lat = _mlp(h_lat, norm_mlp_g[layer], sh2, s2, g2, w1, w2, final_norm_g,
                     _row_tile(t, 1024), 512, last)
        if not last:
            h_ctx = _mlp(h_ctx, norm_mlp_g[layer], csh2, cs2, cg2, w1, w2, final_norm_g, tm_ctx, 512, False)
    return h_lat
```

```python
import functools
import math

import jax
import jax.numpy as jnp
import numpy as np
from jax import lax
from jax.experimental import pallas as pl
from jax.experimental.pallas import tpu as pltpu

F32 = jnp.float32
BF16 = jnp.bfloat16
HIGHEST = lax.Precision.HIGHEST

NORM_EPS = 1e-6
RW_GN_EPS = 64e-5
HEAD = 64
GRID_W = 64
NA_ROWS = 8
NA_COLS = 16
HG_KEY = 128
CHUNK = 64
SCAN_BATCH = 2
MASK_BIAS = -1e30
DECAY_SCALE = math.exp(-0.5)
SUBLANES = 8
HALO_ROWS = 16
VMEM_LIMIT = 52 * 1024 * 1024


def _params(*sem):
    return pltpu.CompilerParams(dimension_semantics=sem, vmem_limit_bytes=VMEM_LIMIT)


def _dot(a, b):
    return jnp.dot(a, b, preferred_element_type=F32)


def _dot_nt(a, b):
    return lax.dot_general(a, b, (((1,), (1,)), ((), ())), preferred_element_type=F32)


def _dot_tn(a, b):
    return lax.dot_general(a, b, (((0,), (0,)), ((), ())), preferred_element_type=F32)


def _mm(a, b):
    return _dot(a.astype(BF16), b.astype(BF16))


def _mm_nt(a, b):
    return _dot_nt(a.astype(BF16), b.astype(BF16))


def _mm_tn(a, b):
    return _dot_tn(a.astype(BF16), b.astype(BF16))


def _bmm(a, b):
    return lax.dot_general(a.astype(BF16), b.astype(BF16), (((2,), (1,)), ((0,), (0,))),
                           preferred_element_type=F32)


def _bmm_nt(a, b):
    return lax.dot_general(a.astype(BF16), b.astype(BF16), (((2,), (2,)), ((0,), (0,))),
                           preferred_element_type=F32)


def _bmm_tn(a, b):
    return lax.dot_general(a.astype(BF16), b.astype(BF16), (((1,), (1,)), ((0,), (0,))),
                           preferred_element_type=F32)


def _split3(x):
    x0 = x.astype(BF16)
    r = x - x0.astype(F32)
    x1 = r.astype(BF16)
    x2 = (r - x1.astype(F32)).astype(BF16)
    return x0, x1, x2


def _dot_sel_r(x, sel):
    hi = x.astype(BF16)
    lo = (x - hi.astype(F32)).astype(BF16)
    return _dot(hi, sel) + _dot(lo, sel)


def _dot_sel_l(sel, x):
    x0, x1, x2 = _split3(x)
    return _dot(sel, x0) + (_dot(sel, x1) + _dot(sel, x2))


def _sigmoid(x):
    return 1.0 / (1.0 + jnp.exp(-x))


def _silu(x):
    return x * _sigmoid(x)


def _rms(x, g):
    return x * lax.rsqrt(jnp.mean(x * x, axis=-1, keepdims=True) + NORM_EPS) * g


def _order_masks(n, reverse):
    t = lax.broadcasted_iota(jnp.int32, (n, n), 0)
    i = lax.broadcasted_iota(jnp.int32, (n, n), 1)
    if reverse:
        return i >= t, i > t
    return i <= t, i < t


def _ada_kernel(c_ref, w_ref, b_ref, o_ref):
    s = _silu(c_ref[...])
    o_ref[0] = jnp.dot(s, w_ref[0], precision=HIGHEST, preferred_element_type=F32) + b_ref[0]


def _ada(c_rows, ada_w, ada_b):
    depth, d, n = ada_w.shape
    rows = c_rows.shape[0]
    tn = 512
    return pl.pallas_call(
        _ada_kernel,
        grid=(depth, n // tn),
        in_specs=[pl.BlockSpec((rows, d), lambda l, j: (0, 0)),
                  pl.BlockSpec((1, d, tn), lambda l, j: (l, 0, j)),
                  pl.BlockSpec((1, 1, tn), lambda l, j: (l, 0, j))],
        out_specs=pl.BlockSpec((1, rows, tn), lambda l, j: (l, 0, j)),
        out_shape=jax.ShapeDtypeStruct((depth, rows, n), F32),
        compiler_params=_params("parallel", "parallel"),
        name="ada",
    )(c_rows, ada_w, ada_b.reshape(depth, 1, n))


def _nmm_kernel(x_ref, g_ref, sh_ref, sc_ref, w_ref, o_ref, a_scr, *, silu_cols):
    @pl.when(pl.program_id(2) == 0)
    def _():
        a = _rms(x_ref[0], g_ref[...]) * (1.0 + sc_ref[0]) + sh_ref[0]
        a_scr[...] = a.astype(BF16)

    if silu_cols:
        o_ref[0, :, :silu_cols] = _silu(_dot(a_scr[...], w_ref[:, :silu_cols])).astype(o_ref.dtype)
        o_ref[0, :, silu_cols:] = _dot(a_scr[...], w_ref[:, silu_cols:]).astype(o_ref.dtype)
    else:
        o_ref[0] = _dot(a_scr[...], w_ref[...]).astype(o_ref.dtype)


def _norm_mod_matmul(x, g, shift, scale, w, tm, tn, silu_cols=0):
    b, t, d = x.shape
    n = w.shape[1]
    assert silu_cols == 0 or tn == n
    return pl.pallas_call(
        functools.partial(_nmm_kernel, silu_cols=silu_cols),
        grid=(b, t // tm, n // tn),
        in_specs=[pl.BlockSpec((1, tm, d), lambda bi, i, j: (bi, i, 0)),
                  pl.BlockSpec((1, d), lambda bi, i, j: (0, 0)),
                  pl.BlockSpec((1, 1, d), lambda bi, i, j: (bi, 0, 0)),
                  pl.BlockSpec((1, 1, d), lambda bi, i, j: (bi, 0, 0)),
                  pl.BlockSpec((d, tn), lambda bi, i, j: (0, j))],
        out_specs=pl.BlockSpec((1, tm, tn), lambda bi, i, j: (bi, i, j)),
        out_shape=jax.ShapeDtypeStruct((b, t, n), BF16),
        scratch_shapes=[pltpu.VMEM((tm, d), BF16)],
        compiler_params=_params("parallel", "parallel", "arbitrary"),
        name="norm_mod_matmul",
    )(x, g.reshape(1, d), shift, scale, w)


def _mlp_kernel(h_ref, g_ref, sh_ref, sc_ref, gate_ref, w1_ref, w2_ref, fg_ref, o_ref,
                a_scr, acc_scr, *, final_norm):
    j = pl.program_id(2)

    @pl.when(j == 0)
    def _():
        a = _rms(h_ref[0], g_ref[...]) * (1.0 + sc_ref[0]) + sh_ref[0]
        a_scr[...] = a.astype(BF16)
        acc_scr[...] = jnp.zeros_like(acc_scr)

    hid = jnp.maximum(_dot(a_scr[...], w1_ref[...]), 0.0)
    acc_scr[...] += _dot((hid * hid).astype(BF16), w2_ref[...])

    @pl.when(j == pl.num_programs(2) - 1)
    def _():
        out = h_ref[0] + gate_ref[0] * acc_scr[...]
        if final_norm:
            out = _rms(out, fg_ref[...])
        o_ref[0] = out


def _mlp(h, g, shift, scale, gate, w1, w2, final_g, tm, th, final_norm):
    b, t, d = h.shape
    hid = w1.shape[1]
    vec = pl.BlockSpec((1, 1, d), lambda bi, i, j: (bi, 0, 0))
    par = pl.BlockSpec((1, d), lambda bi, i, j: (0, 0))
    return pl.pallas_call(
        functools.partial(_mlp_kernel, final_norm=final_norm),
        grid=(b, t // tm, hid // th),
        in_specs=[pl.BlockSpec((1, tm, d), lambda bi, i, j: (bi, i, 0)),
                  par, vec, vec, vec,
                  pl.BlockSpec((d, th), lambda bi, i, j: (0, j)),
                  pl.BlockSpec((th, d), lambda bi, i, j: (j, 0)),
                  par],
        out_specs=pl.BlockSpec((1, tm, d), lambda bi, i, j: (bi, i, 0)),
        out_shape=jax.ShapeDtypeStruct((b, t, d), F32),
        scratch_shapes=[pltpu.VMEM((tm, d), BF16), pltpu.VMEM((tm, d), F32)],
        compiler_params=_params("parallel", "parallel", "arbitrary"),
        name="mlp",
    )(h, g.reshape(1, d), shift, scale, gate, w1, w2, final_g.reshape(1, d))


def _rwkv_prep_kernel(p_ref, pp_ref, pn_ref, mup_ref, mun_ref, w0_ref, wup_ref, a0_ref, aup_ref,
                      gup_ref, kk_ref, ka_ref, rk_ref, bd_ref,
                      r_o, v_o, kkn_o, lw0_o, lw1_o, kd0_o, kd1_o, b0_o, b1_o, g_o, bonus_o,
                      *, rw_dim, w_lora, a_lora):
    i = pl.program_id(1)
    n = pl.num_programs(1)
    p = p_ref[0].astype(F32)
    tm = p.shape[0]
    row = lax.broadcasted_iota(jnp.int32, (SUBLANES, 1), 0)
    first = jnp.where(i > 0, pp_ref[0].astype(F32)[HALO_ROWS - 1:HALO_ROWS, :], 0.0)
    last = jnp.where(i < n - 1, pn_ref[0].astype(F32)[0:1, :], 0.0)
    prev = pltpu.roll(p, 1, axis=0)
    prev = jnp.concatenate([jnp.where(row == 0, first, prev[:SUBLANES]), prev[SUBLANES:]], axis=0)
    nxt = pltpu.roll(p, tm - 1, axis=0)
    nxt = jnp.concatenate([nxt[:tm - SUBLANES],
                           jnp.where(row == SUBLANES - 1, last, nxt[tm - SUBLANES:])], axis=0)
    ps = p + (prev - p) * mup_ref[...] + (nxt - p) * mun_ref[...]

    r = ps[:, :rw_dim]
    k = ps[:, rw_dim:2 * rw_dim]
    v = ps[:, 2 * rw_dim:3 * rw_dim]
    off = 3 * rw_dim
    w_lo = ps[:, off:off + w_lora]
    a_lo = ps[:, off + w_lora:off + w_lora + a_lora]
    g_lo = ps[:, off + w_lora + a_lora:]

    w_pre = w0_ref[...] + _mm(jnp.tanh(w_lo), wup_ref[...])
    log_decay = -DECAY_SCALE * _sigmoid(w_pre)
    a = _sigmoid(a0_ref[...] + _mm(a_lo, aup_ref[...]))
    g = _mm(_sigmoid(g_lo), gup_ref[...])

    bd = bd_ref[...]
    kk = k * kk_ref[...]
    kk = kk / jnp.maximum(jnp.sqrt(_dot_sel_r(kk * kk, bd)), 1e-12)
    a_f = a[:, :rw_dim]
    a_b = a[:, rw_dim:]
    ka = ka_ref[...]
    kd_f = k * (1.0 + (a_f - 1.0) * ka)
    kd_b = k * (1.0 + (a_b - 1.0) * ka)
    bonus = _dot_sel_r(r * (kd_f + kd_b) * rk_ref[...], bd) * v

    r_o[0] = r.astype(BF16)
    v_o[0] = v.astype(BF16)
    kkn_o[0] = kk.astype(BF16)
    lw0_o[0] = log_decay[:, :rw_dim]
    lw1_o[0] = log_decay[:, rw_dim:]
    kd0_o[0] = kd_f.astype(BF16)
    kd1_o[0] = kd_b.astype(BF16)
    b0_o[0] = (kk * a_f).astype(BF16)
    b1_o[0] = (kk * a_b).astype(BF16)
    g_o[0] = g.astype(BF16)
    bonus_o[0] = bonus.astype(BF16)


def _rwkv_prep(p, prm, tm):
    b, t, _ = p.shape
    rw_dim = prm["k_k"].shape[1]
    w_lora = prm["w_up"].shape[0]
    a_lora = prm["a_up"].shape[0]
    g_lora = prm["g_up"].shape[0]
    cols = 3 * rw_dim + w_lora + a_lora + g_lora
    n_halo = t // HALO_ROWS
    tm_halo = tm // HALO_ROWS

    def full(a):
        return pl.BlockSpec(a.shape, lambda bi, i: (0,) * a.ndim)

    consts = [prm["mu_prev"], prm["mu_next"], prm["w0"], prm["w_up"], prm["a0"], prm["a_up"],
              prm["g_up"], prm["k_k"], prm["k_a"], prm["r_k"], prm["bd"]]
    out_spec = pl.BlockSpec((1, tm, rw_dim), lambda bi, i: (bi, i, 0))
    out_dtypes = [BF16, BF16, BF16, F32, F32, BF16, BF16, BF16, BF16, BF16, BF16]
    return pl.pallas_call(
        functools.partial(_rwkv_prep_kernel, rw_dim=rw_dim, w_lora=w_lora, a_lora=a_lora),
        grid=(b, t // tm),
        in_specs=[pl.BlockSpec((1, tm, cols), lambda bi, i: (bi, i, 0)),
                  pl.BlockSpec((1, HALO_ROWS, cols),
                               lambda bi, i: (bi, jnp.maximum(i * tm_halo - 1, 0), 0)),
                  pl.BlockSpec((1, HALO_ROWS, cols),
                               lambda bi, i: (bi, jnp.minimum((i + 1) * tm_halo, n_halo - 1), 0))]
                 + [full(a) for a in consts],
        out_specs=[out_spec] * 11,
        out_shape=[jax.ShapeDtypeStruct((b, t, rw_dim), dt) for dt in out_dtypes],
        compiler_params=_params("parallel", "parallel"),
        name="rwkv_prep",
    )(p, p, p, *consts)


def _rwkv_chunk(kkd, rd, binv, kinv, bfin, kfin, v, s, wtot, incl2, strict2):
    c = v.shape[1]
    lanes = v.shape[2]
    la = lax.broadcasted_iota(jnp.int32, (1, 1, lanes), 2) < HEAD

    def only_a(z):
        return jnp.where(la, z, 0.0)

    def only_b(z):
        return jnp.where(la, 0.0, z)

    x = jnp.concatenate([kkd, rd], axis=1)
    gh_a = _bmm_nt(only_a(x), jnp.concatenate([binv, kinv, s], axis=1))
    gh_b = _bmm_nt(only_b(x), jnp.concatenate([kinv, binv, s], axis=1))
    a2_a = jnp.where(strict2, gh_a[:, :c, :2 * c], 0.0)
    a2_b = jnp.where(strict2, gh_b[:, :c, :2 * c], 0.0)
    v_a = only_a(v)
    v_b = only_b(v)
    akk = jnp.where(la, a2_b, a2_a)
    u = -(gh_a[:, :c, 2 * c:] + gh_b[:, :c, 2 * c:] + _bmm(akk, jnp.concatenate([v_b, v_a], axis=1)))
    nm = -jnp.where(la, a2_a, a2_b)
    for _ in range(c.bit_length() - 2):
        top = jnp.concatenate([only_a(u), only_a(nm)], axis=2)
        bot = jnp.concatenate([only_b(u), only_b(nm)], axis=2)
        w = _bmm(nm, jnp.concatenate([top, bot], axis=1))
        u = u + w[:, :, :lanes]
        nm = w[:, :, lanes:]
    u = u + _bmm(nm, jnp.concatenate([only_a(u), only_b(u)], axis=1))
    bmat = jnp.concatenate([jnp.where(incl2, gh_a[:, c:, :2 * c], 0.0),
                            jnp.where(incl2, gh_b[:, c:, :2 * c], 0.0)], axis=2)
    o = (gh_a[:, c:, 2 * c:] + gh_b[:, c:, 2 * c:]
         + _bmm(bmat, jnp.concatenate([only_a(u), v_a, v_b, only_b(u)], axis=1)))
    row = lax.broadcasted_iota(jnp.int32, s.shape, 1) // HEAD
    col = lax.broadcasted_iota(jnp.int32, s.shape, 2) // HEAD
    upd = _bmm_tn(jnp.concatenate([u, v], axis=1), jnp.concatenate([bfin, kfin], axis=1))
    s_new = s * wtot + jnp.where(row == col, upd, 0.0)
    return o, s_new


def _rwkv_decay_terms(r, v, kk, lw, kd, bm, reverse):
    c = r.shape[0]
    r, v, kk, kd, bm = (z.astype(F32) for z in (r, v, kk, kd, bm))
    incl, _ = _order_masks(c, reverse)
    cum = _dot_sel_l(incl.astype(BF16), lw)
    last = 0 if reverse else c - 1
    tot = cum[last:last + 1, :]
    e_neg = jnp.exp(-cum)
    e_fin = jnp.exp(tot - cum)
    return (kk * jnp.exp(cum - lw), r * jnp.exp(cum), bm * e_neg, kd * e_neg, bm * e_fin,
            kd * e_fin, v, jnp.exp(tot))


def _rwkv_scan_kernel(rf, vf, kf, lwf, kdf, bf, rb, vb, kb, lwb, kdb, bb, s0_ref,
                      of_ref, ob_ref, sfin_ref, s_scr, *, heads):
    i = pl.program_id(1)

    nb = rf.shape[0]
    c = rf.shape[1]
    pair = 2 * HEAD

    @pl.when(i == 0)
    def _():
        s_scr[...] = s0_ref[...].reshape(s_scr.shape)

    pairs = heads // 2
    terms = [tm for bi in range(nb)
             for tm in (_rwkv_decay_terms(rf[bi], vf[bi], kf[bi], lwf[bi], kdf[bi], bf[bi], False),
                        _rwkv_decay_terms(rb[bi], vb[bi], kb[bi], lwb[bi], kdb[bi], bb[bi], True))]
    ops = [jnp.stack([z[:, p * pair:(p + 1) * pair] for z in zs for p in range(pairs)])
           for zs in zip(*terms)]
    n = nb * 2 * pairs
    t = lax.broadcasted_iota(jnp.int32, (n, c, 2 * c), 1)
    j = lax.broadcasted_iota(jnp.int32, (n, c, 2 * c), 2) % c
    rev = lax.broadcasted_iota(jnp.int32, (n, c, 2 * c), 0) // pairs % 2 == 1
    ahead = jnp.where(rev, t - j, j - t)
    o, s_new = _rwkv_chunk(*ops[:7], s_scr[...], ops[7], ahead <= 0, ahead < 0)
    s_scr[...] = s_new
    for bi in range(nb):
        base = bi * 2 * pairs
        of_ref[bi] = jnp.concatenate([o[base + p] for p in range(pairs)], axis=1).astype(of_ref.dtype)
        ob_ref[bi] = jnp.concatenate([o[base + pairs + p] for p in range(pairs)],
                                     axis=1).astype(ob_ref.dtype)

    @pl.when(i == pl.num_programs(1) - 1)
    def _():
        sfin_ref[...] = s_scr[...].reshape(sfin_ref.shape)


def _rwkv_scan(prep, s0):
    r, v, kk, lw0, lw1, kd0, kd1, b0, b1 = prep
    b, t, rw_dim = r.shape
    heads = rw_dim // HEAD
    n = t // CHUNK
    nb = SCAN_BATCH if b % SCAN_BATCH == 0 else 1
    fwd = pl.BlockSpec((nb, CHUNK, rw_dim), lambda bi, i: (bi, i, 0))
    bwd = pl.BlockSpec((nb, CHUNK, rw_dim), lambda bi, i: (bi, n - 1 - i, 0))
    st = pl.BlockSpec((nb,) + s0.shape[1:], lambda bi, i: (bi, 0, 0, 0))
    o_sds = jax.ShapeDtypeStruct((b, t, rw_dim), BF16)
    return pl.pallas_call(
        functools.partial(_rwkv_scan_kernel, heads=heads),
        grid=(b // nb, n),
        in_specs=[fwd] * 6 + [bwd] * 6 + [st],
        out_specs=[fwd, bwd, st],
        out_shape=[o_sds, o_sds, jax.ShapeDtypeStruct(s0.shape, F32)],
        scratch_shapes=[pltpu.VMEM((nb * s0.shape[1],) + s0.shape[2:], F32)],
        compiler_params=_params("parallel", "arbitrary"),
        name="rwkv_scan",
    )(r, v, kk, lw0, kd0, b0, r, v, kk, lw1, kd1, b1, s0)


def _softmax_pv(s_list, v_list):
    m = s_list[0].max(axis=-1, keepdims=True)
    for s in s_list[1:]:
        m = jnp.maximum(m, s.max(axis=-1, keepdims=True))
    num = None
    den = None
    for s, v in zip(s_list, v_list):
        p = jnp.exp(s - m)
        d = p.sum(axis=-1, keepdims=True)
        o = _dot(p.astype(BF16), v)
        num = o if num is None else num + o
        den = d if den is None else den + d
    return num / den


def _natten_kernel(q_ref, k_ref, v_ref, kc_ref, vc_ref, bias_ref, o_ref, *, rows, scale, rb):
    kc = kc_ref[0]
    vc = vc_ref[0]
    l = kc.shape[0]
    nwin = NA_ROWS * GRID_W
    head0 = lax.broadcasted_iota(jnp.int32, (1, 2 * HEAD), 1) < HEAD

    def body(it, carry):
        base = it * rb
        start = pl.multiple_of(base * GRID_W, rb * GRID_W)
        q = q_ref[0, pl.ds(start, rb * GRID_W), :] * scale
        zero = jnp.zeros_like(q)
        q0 = jnp.where(head0, q, zero)
        q1 = jnp.where(head0, zero, q)
        qs, kws, vws, bs = [], [], [], []
        for j in range(rb):
            r = base + j
            r0 = jnp.clip(r - NA_ROWS // 2, 0, rows - NA_ROWS)
            rows_j = slice(j * GRID_W, (j + 1) * GRID_W)
            qs.append(jnp.concatenate([q0[rows_j], q1[rows_j]], axis=0))
            win = pl.ds(pl.multiple_of(r0 * GRID_W, GRID_W), nwin)
            kws.append(k_ref[0, win, :])
            vws.append(v_ref[0, win, :])
            bs.append(bias_ref[0, r - r0])
        qs = jnp.stack(qs)
        s_win = _bmm_nt(qs, jnp.stack(kws)) + jnp.stack(bs)
        s_ctx = _dot_nt(qs.reshape(rb * 2 * GRID_W, 2 * HEAD), kc).reshape(rb, 2 * GRID_W, l)
        m = jnp.maximum(s_win.max(axis=-1, keepdims=True), s_ctx.max(axis=-1, keepdims=True))
        p_win = jnp.exp(s_win - m)
        p_ctx = jnp.exp(s_ctx - m)
        den = p_win.sum(axis=-1, keepdims=True) + p_ctx.sum(axis=-1, keepdims=True)
        o = _bmm(p_win, jnp.stack(vws))
        o = o + _dot(p_ctx.reshape(rb * 2 * GRID_W, l).astype(BF16), vc).reshape(rb, 2 * GRID_W, 2 * HEAD)
        o = o / den
        out = jnp.concatenate([jnp.where(head0, o[j, :GRID_W], o[j, GRID_W:]) for j in range(rb)], axis=0)
        o_ref[0, pl.ds(start, rb * GRID_W), :] = out.astype(o_ref.dtype)
        return carry

    lax.fori_loop(0, rows // rb, body, 0)


def _natten(p_lat, p_ctx, bias, rw_cols, na_dim):
    b, t, _ = p_lat.shape
    l = p_ctx.shape[1]
    pairs = na_dim // (2 * HEAD)
    qb = rw_cols // (2 * HEAD)
    kb = qb + pairs
    vb = kb + pairs
    rows = t // GRID_W

    def lat(off):
        return pl.BlockSpec((1, t, 2 * HEAD), lambda bi, hp: (bi, 0, off + hp))

    def ctx(off):
        return pl.BlockSpec((1, l, 2 * HEAD), lambda bi, hp: (bi, 0, off + hp))

    rb = NA_ROWS if rows % NA_ROWS == 0 else 1
    return pl.pallas_call(
        functools.partial(_natten_kernel, rows=rows, scale=HEAD ** -0.5, rb=rb),
        grid=(b, pairs),
        in_specs=[lat(qb), lat(kb), lat(vb), ctx(kb), ctx(vb),
                  pl.BlockSpec((1, NA_ROWS, 2 * GRID_W, NA_ROWS * GRID_W), lambda bi, hp: (hp, 0, 0, 0))],
        out_specs=pl.BlockSpec((1, t, 2 * HEAD), lambda bi, hp: (bi, 0, hp)),
        out_shape=jax.ShapeDtypeStruct((b, t, na_dim), BF16),
        compiler_params=_params("parallel", "parallel"),
        name="natten",
    )(p_lat, p_lat, p_lat, p_ctx, p_ctx, bias)


def _ctx_attn_kernel(q_ref, k_ref, v_ref, o_ref, *, scale):
    q = q_ref[0] * scale
    k = k_ref[0]
    v = v_ref[0]
    outs = []
    for hh in range(2):
        sl = slice(hh * HEAD, (hh + 1) * HEAD)
        outs.append(_softmax_pv([_dot_nt(q[:, sl], k[:, sl])], [v[:, sl]]))
    o_ref[0] = jnp.concatenate(outs, axis=1).astype(o_ref.dtype)


def _ctx_attn(p_ctx, rw_cols, na_dim):
    b, l, _ = p_ctx.shape
    pairs = na_dim // (2 * HEAD)
    qb = rw_cols // (2 * HEAD)

    def blk(off):
        return pl.BlockSpec((1, l, 2 * HEAD), lambda bi, hp: (bi, 0, off + hp))

    return pl.pallas_call(
        functools.partial(_ctx_attn_kernel, scale=HEAD ** -0.5),
        grid=(b, pairs),
        in_specs=[blk(qb), blk(qb + pairs), blk(qb + 2 * pairs)],
        out_specs=pl.BlockSpec((1, l, 2 * HEAD), lambda bi, hp: (bi, 0, hp)),
        out_shape=jax.ShapeDtypeStruct((b, l, na_dim), BF16),
        compiler_params=_params("parallel", "parallel"),
        name="ctx_attn",
    )(p_ctx, p_ctx, p_ctx)


def _natten_bias(rpb):
    cols = np.arange(GRID_W)
    c0 = np.clip(cols - NA_COLS // 2, 0, GRID_W - NA_COLS)
    valid = (cols[None, :] >= c0[:, None]) & (cols[None, :] < c0[:, None] + NA_COLS)
    col_rel = cols[None, :] - cols[:, None] + NA_COLS - 1
    onehot = (col_rel[:, :, None] == np.arange(2 * NA_COLS - 1)).astype(np.float32)
    toe = jnp.einsum("hrj,qkj->hrqk", rpb.astype(F32), onehot, precision=HIGHEST)
    toe = jnp.where(valid[None, None], toe, MASK_BIAS)
    tab = jnp.stack([toe[:, NA_ROWS - 1 - d:2 * NA_ROWS - 1 - d] for d in range(NA_ROWS)], axis=1)
    tab = tab.transpose(0, 1, 3, 2, 4)
    heads = rpb.shape[0]
    tab = tab.reshape(heads // 2, 2, NA_ROWS, GRID_W, NA_ROWS * GRID_W)
    return tab.transpose(0, 2, 1, 3, 4).reshape(heads // 2, NA_ROWS, 2 * GRID_W, NA_ROWS * GRID_W)


def _even_out_kernel(h_ref, of_ref, ob_ref, bonus_ref, g_ref, na_ref, w_ref, gate_ref,
                     lng_ref, lnb_ref, bd_ref, o_ref):
    bd = bd_ref[...]
    o = of_ref[0].astype(F32) + ob_ref[0].astype(F32)
    inv = 1.0 / HEAD
    mu = _dot_sel_r(o, bd) * inv
    xc = o - mu
    var = _dot_sel_r(xc * xc, bd) * inv
    y = xc * lax.rsqrt(var + RW_GN_EPS) * lng_ref[...] + lnb_ref[...]
    rw = (y + bonus_ref[0].astype(F32)) * g_ref[0].astype(F32)
    cat = jnp.concatenate([rw.astype(BF16), na_ref[0]], axis=1)
    o_ref[0] = h_ref[0] + gate_ref[0] * _dot(cat, w_ref[...])


def _even_out(h, o_f, o_b, bonus, g, na, w_out, gate, ln_g, ln_b, bd, tm):
    b, t, d = h.shape
    rw_dim = o_f.shape[2]
    na_dim = na.shape[2]
    big = pl.BlockSpec((1, tm, d), lambda bi, i: (bi, i, 0))
    half = pl.BlockSpec((1, tm, rw_dim), lambda bi, i: (bi, i, 0))
    par = pl.BlockSpec((1, rw_dim), lambda bi, i: (0, 0))
    return pl.pallas_call(
        _even_out_kernel,
        grid=(b, t // tm),
        in_specs=[big, half, half, half, half,
                  pl.BlockSpec((1, tm, na_dim), lambda bi, i: (bi, i, 0)),
                  pl.BlockSpec(w_out.shape, lambda bi, i: (0, 0)),
                  pl.BlockSpec((1, 1, d), lambda bi, i: (bi, 0, 0)),
                  par, par,
                  pl.BlockSpec(bd.shape, lambda bi, i: (0, 0))],
        out_specs=big,
        out_shape=jax.ShapeDtypeStruct((b, t, d), F32),
        compiler_params=_params("parallel", "parallel"),
        name="even_out",
    )(h, o_f, o_b, bonus, g, na, w_out, gate, ln_g.reshape(1, rw_dim), ln_b.reshape(1, rw_dim), bd)


def _odd_out_kernel(h_ref, of_ref, ob_ref, gate_in_ref, w_ref, gate_ref, ng_ref, o_ref):
    o = of_ref[0].astype(F32) + ob_ref[0].astype(F32)
    y = _rms(o, ng_ref[...]) * _silu(gate_in_ref[0].astype(F32))
    o_ref[0] = h_ref[0] + gate_ref[0] * _dot(y.astype(BF16), w_ref[...])


def _odd_out(h, o_f, o_b, p, gate_block, w_out, gate, norm_g, tm):
    b, t, d = h.shape
    vd = o_f.shape[2]
    big = pl.BlockSpec((1, tm, d), lambda bi, i: (bi, i, 0))
    val = pl.BlockSpec((1, tm, vd), lambda bi, i: (bi, i, 0))
    return pl.pallas_call(
        _odd_out_kernel,
        grid=(b, t // tm),
        in_specs=[big, val, val,
                  pl.BlockSpec((1, tm, vd), lambda bi, i: (bi, i, gate_block)),
                  pl.BlockSpec(w_out.shape, lambda bi, i: (0, 0)),
                  pl.BlockSpec((1, 1, d), lambda bi, i: (bi, 0, 0)),
                  pl.BlockSpec((1, vd), lambda bi, i: (0, 0))],
        out_specs=big,
        out_shape=jax.ShapeDtypeStruct((b, t, d), F32),
        compiler_params=_params("parallel", "parallel"),
        name="odd_out",
    )(h, o_f, o_b, p, w_out, gate, norm_g.reshape(1, vd))


def _hgrn_terms(qp, fp, ip, lb, reverse):
    c = qp.shape[0]
    incl, _ = _order_masks(c, reverse)
    q = qp.astype(F32)
    forget = lb + (1.0 - lb) * _sigmoid(fp.astype(F32))
    k = 1.0 - forget
    cum = _dot_sel_l(incl.astype(BF16), jnp.log(forget))
    last = 0 if reverse else c - 1
    tot = cum[last:last + 1, :]
    mid = cum[c // 2:c // 2 + 1, :]
    q_mid = q * jnp.exp(cum - mid)
    k_mid = k * jnp.exp(mid - cum)
    return q_mid, k_mid, q_mid * jnp.exp(mid), k_mid * jnp.exp(tot - mid), ip, jnp.exp(tot)


def _hgrn_scan_kernel(qf, ff, vf, qb, fb, vb, hl_ref, s0_ref, of_ref, ob_ref, sfin_ref, s_scr,
                      *, layer, heads):
    i = pl.program_id(1)

    @pl.when(i == 0)
    def _():
        s_scr[...] = s0_ref[...].reshape(s_scr.shape)

    hl = hl_ref[...]
    e = jnp.exp(hl - hl.max(axis=0, keepdims=True))
    sm = e / e.sum(axis=0, keepdims=True)
    lb = jnp.zeros_like(sm[0:1])
    for j in range(1, layer + 1):
        lb = lb + sm[j:j + 1]

    nb = qf.shape[0]
    c = qf.shape[1]
    n = nb * 2 * heads
    terms = [tm for bi in range(nb) for tm in (_hgrn_terms(qf[bi], ff[bi], vf[bi], lb, False),
                                               _hgrn_terms(qb[bi], fb[bi], vb[bi], lb, True))]
    q_mid, k_mid, q_cum, k_fin, v, wtot = [
        jnp.stack([z[:, h * HG_KEY:(h + 1) * HG_KEY] for z in zs for h in range(heads)])
        for zs in zip(*terms)]
    t = lax.broadcasted_iota(jnp.int32, (n, c, c), 1)
    j = lax.broadcasted_iota(jnp.int32, (n, c, c), 2)
    rev = lax.broadcasted_iota(jnp.int32, (n, c, c), 0) // heads % 2 == 1
    incl = jnp.where(rev, t - j, j - t) <= 0
    s = s_scr[...]
    att = jnp.where(incl, _bmm_nt(q_mid, k_mid), 0.0)
    o = _bmm(att, v) + _bmm_nt(q_cum, s)
    s_scr[...] = s * wtot + _bmm_tn(v, k_fin)
    for bi in range(nb):
        base = bi * 2 * heads
        of_ref[bi] = jnp.concatenate([o[base + h] for h in range(heads)], axis=1).astype(of_ref.dtype)
        ob_ref[bi] = jnp.concatenate([o[base + heads + h] for h in range(heads)],
                                     axis=1).astype(ob_ref.dtype)

    @pl.when(i == pl.num_programs(1) - 1)
    def _():
        sfin_ref[...] = s_scr[...].reshape(sfin_ref.shape)


def _hgrn_scan(p, hg_lower, s0, layer, heads):
    b, t, _ = p.shape
    n = t // CHUNK
    width = heads * HG_KEY
    depth = hg_lower.shape[0]

    nb = SCAN_BATCH if b % SCAN_BATCH == 0 else 1

    def fwd(blk):
        return pl.BlockSpec((nb, CHUNK, width), lambda bi, i: (bi, i, blk))

    def bwd(blk):
        return pl.BlockSpec((nb, CHUNK, width), lambda bi, i: (bi, n - 1 - i, blk))

    st = pl.BlockSpec((nb, 2 * heads, HG_KEY, HG_KEY), lambda bi, i: (bi, 0, 0, 0))
    o_sds = jax.ShapeDtypeStruct((b, t, width), BF16)
    return pl.pallas_call(
        functools.partial(_hgrn_scan_kernel, layer=layer, heads=heads),
        grid=(b // nb, n),
        in_specs=[fwd(0), fwd(1), fwd(3), bwd(0), bwd(2), bwd(3),
                  pl.BlockSpec((depth, width), lambda bi, i: (0, 0)), st],
        out_specs=[fwd(0), bwd(0), st],
        out_shape=[o_sds, o_sds, jax.ShapeDtypeStruct((b, 2 * heads, HG_KEY, HG_KEY), F32)],
        scratch_shapes=[pltpu.VMEM((nb * 2 * heads, HG_KEY, HG_KEY), F32)],
        compiler_params=_params("parallel", "arbitrary"),
        name="hgrn_scan",
    )(p, p, p, p, p, p, hg_lower, s0)


def _row_tile(t, target):
    return target if t % target == 0 else t


def kernel(x, c, ctx, c_ctx, norm_mix_g, norm_mlp_g, ada_w, ada_b, mlp_w1, mlp_w2, ev_w_in,
           ev_w_out, rw_mu_prev, rw_mu_next, rw_w0, rw_w_up, rw_a0, rw_a_up, rw_g_up, rw_k_k,
           rw_k_a, rw_r_k, rw_ln_g, rw_ln_b, na_rpb, od_w_in, od_w_out, hg_lower, hg_norm_g,
           final_norm_g):
    b, t, d = x.shape
    l = ctx.shape[1]
    depth = ada_w.shape[0]
    rw_dim = rw_k_k.shape[1]
    rw_heads = rw_dim // HEAD
    rw_cols = rw_mu_prev.shape[1]
    na_dim = d - rw_dim
    hg_heads = d // HG_KEY

    pad_rows = -(b + 1) % SUBLANES
    c_rows = jnp.concatenate([c, c_ctx[None], jnp.zeros((pad_rows, d), F32)], axis=0)
    mods = _ada(c_rows, ada_w, ada_b)

    def mod_vecs(layer):
        m = mods[layer]
        lat = [m[:b, j * d:(j + 1) * d].reshape(b, 1, d) for j in range(6)]
        cx = [m[b, j * d:(j + 1) * d].reshape(1, 1, d) for j in range(6)]
        return lat, cx

    eye = jnp.arange(rw_dim) // HEAD
    bd = (eye[:, None] == eye[None, :]).astype(BF16)

    def flat(z):
        return z.reshape(1, b * l, z.shape[-1])

    def per_sample(z):
        return z.reshape(b, l, z.shape[-1])

    tm_lat = _row_tile(t, 512)
    tm_ctx = _row_tile(b * l, 512)
    tm_mlp_ctx = _row_tile(b * l, 1024)
    h_lat, h_ctx = x, flat(ctx)
    for layer in range(depth):
        last = layer == depth - 1
        (sh1, s1, g1, sh2, s2, g2), (csh1, cs1, cg1, csh2, cs2, cg2) = mod_vecs(layer)
        if layer % 2 == 0:
            e = layer // 2
            w_in = ev_w_in[e].astype(BF16)
            n_in = w_in.shape[1]
            p_lat = _norm_mod_matmul(h_lat, norm_mix_g[layer], sh1, s1, w_in, tm_lat, n_in)
            p_ctx = per_sample(_norm_mod_matmul(h_ctx, norm_mix_g[layer], csh1, cs1, w_in, tm_ctx, n_in))
            prm = dict(
                mu_prev=rw_mu_prev[e][None], mu_next=rw_mu_next[e][None],
                w0=rw_w0[e].reshape(1, 2 * rw_dim),
                w_up=jnp.concatenate([rw_w_up[e, 0], rw_w_up[e, 1]], axis=1),
                a0=rw_a0[e].reshape(1, 2 * rw_dim),
                a_up=jnp.concatenate([rw_a_up[e, 0], rw_a_up[e, 1]], axis=1),
                g_up=rw_g_up[e], k_k=rw_k_k[e][None], k_a=rw_k_a[e][None],
                r_k=rw_r_k[e].reshape(1, rw_dim), bd=bd)
            s_zero = jnp.zeros((b, rw_heads, 2 * HEAD, 2 * HEAD), F32)
            prep_c = _rwkv_prep(p_ctx, prm, _row_tile(l, 256))
            oc_f, oc_b, s_ctx = _rwkv_scan(prep_c[:9], s_zero)
            prep_l = _rwkv_prep(p_lat, prm, tm_lat)
            ol_f, ol_b, _ = _rwkv_scan(prep_l[:9], s_ctx)
            na_lat = _natten(p_lat, p_ctx, _natten_bias(na_rpb[e]), rw_cols, na_dim)
            w_out = ev_w_out[e].astype(BF16)
            h_lat = _even_out(h_lat, ol_f, ol_b, prep_l[10], prep_l[9], na_lat, w_out, g1,
                              rw_ln_g[e], rw_ln_b[e], bd, tm_lat)
            if not last:
                na_ctx = _ctx_attn(p_ctx, rw_cols, na_dim)
                h_ctx = _even_out(h_ctx, flat(oc_f), flat(oc_b), flat(prep_c[10]), flat(prep_c[9]),
                                  flat(na_ctx), w_out, cg1, rw_ln_g[e], rw_ln_b[e], bd, tm_ctx)
        else:
            o = layer // 2
            w_in = od_w_in[o].astype(BF16)
            n_in = w_in.shape[1]
            hg_kdim = hg_heads * HG_KEY
            p_lat = _norm_mod_matmul(h_lat, norm_mix_g[layer], sh1, s1, w_in, tm_lat, n_in, hg_kdim)
            p_ctx = _norm_mod_matmul(h_ctx, norm_mix_g[layer], csh1, cs1, w_in, tm_ctx, n_in, hg_kdim)
            s_zero = jnp.zeros((b, 2 * hg_heads, HG_KEY, HG_KEY), F32)
            oc_f, oc_b, s_ctx = _hgrn_scan(per_sample(p_ctx), hg_lower, s_zero, layer, hg_heads)
            ol_f, ol_b, _ = _hgrn_scan(p_lat, hg_lower, s_ctx, layer, hg_heads)
            w_out = od_w_out[o].astype(BF16)
            gate_block = (n_in - d) // d
            h_lat = _odd_out(h_lat, ol_f, ol_b, p_lat, gate_block, w_out, g1, hg_norm_g[o], tm_lat)
            if not last:
                h_ctx = _odd_out(h_ctx, flat(oc_f), flat(oc_b), p_ctx, gate_block, w_out, cg1,
                                 hg_norm_g[o], tm_ctx)
        w1 = mlp_w1[layer].astype(BF16)
        w2 = mlp_w2[layer].astype(BF16)
        h_lat = _mlp(h_lat, norm_mlp_g[layer], sh2, s2, g2, w1, w2, final_norm_g,
                     _row_tile(t, 1024), 1024, last)
        if not last:
            h_ctx = _mlp(h_ctx, norm_mlp_g[layer], csh2, cs2, cg2, w1, w2, final_norm_g,
                         tm_mlp_ctx, 1024, False)
    return h_lat
```

```python
import functools
import math

import jax
import jax.numpy as jnp
import numpy as np
from jax import lax
from jax.experimental import pallas as pl
from jax.experimental.pallas import tpu as pltpu

F32 = jnp.float32
BF16 = jnp.bfloat16
HIGHEST = lax.Precision.HIGHEST

NORM_EPS = 1e-6
RW_GN_EPS = 64e-5
HEAD = 64
GRID_W = 64
NA_ROWS = 8
NA_COLS = 16
HG_KEY = 128
CHUNK = 64
MLP_ROW_CHUNK = 256
SCAN_BATCH = 2
MASK_BIAS = -1e30
DECAY_SCALE_LOG2 = math.exp(-0.5) / math.log(2.0)
SUBLANES = 8
HALO_ROWS = 16
VMEM_LIMIT = 52 * 1024 * 1024


def _params(*sem):
    return pltpu.CompilerParams(dimension_semantics=sem, vmem_limit_bytes=VMEM_LIMIT)


def _dot(a, b):
    return jnp.dot(a, b, preferred_element_type=F32)


def _dot_nt(a, b):
    return lax.dot_general(a, b, (((1,), (1,)), ((), ())), preferred_element_type=F32)


def _dot_tn(a, b):
    return lax.dot_general(a, b, (((0,), (0,)), ((), ())), preferred_element_type=F32)


def _mm(a, b):
    return _dot(a.astype(BF16), b.astype(BF16))


def _mm_nt(a, b):
    return _dot_nt(a.astype(BF16), b.astype(BF16))


def _mm_tn(a, b):
    return _dot_tn(a.astype(BF16), b.astype(BF16))


def _bmm(a, b):
    return lax.dot_general(a.astype(BF16), b.astype(BF16), (((2,), (1,)), ((0,), (0,))),
                           preferred_element_type=F32)


def _bmm_nt(a, b):
    return lax.dot_general(a.astype(BF16), b.astype(BF16), (((2,), (2,)), ((0,), (0,))),
                           preferred_element_type=F32)


def _bmm_tn(a, b):
    return lax.dot_general(a.astype(BF16), b.astype(BF16), (((1,), (1,)), ((0,), (0,))),
                           preferred_element_type=F32)


def _hi_lo(x):
    hi = x.astype(BF16)
    return hi, (x - hi.astype(F32)).astype(BF16)


def _dot_sel_r(x, sel):
    return _dot(x.astype(BF16), sel)


def _dot_sel_l(sel, x):
    hi, lo = _hi_lo(x)
    return _dot(sel, hi) + _dot(sel, lo)


def _sigmoid(x):
    return 1.0 / (1.0 + jnp.exp(-x))


def _silu(x):
    return x * _sigmoid(x)


def _rms(x, g):
    return x * lax.rsqrt(jnp.mean(x * x, axis=-1, keepdims=True) + NORM_EPS) * g


def _order_masks(n, reverse):
    t = lax.broadcasted_iota(jnp.int32, (n, n), 0)
    i = lax.broadcasted_iota(jnp.int32, (n, n), 1)
    if reverse:
        return i >= t, i > t
    return i <= t, i < t


def _ada_kernel(c_ref, w_ref, b_ref, o_ref):
    s = _silu(c_ref[...])
    o_ref[0] = jnp.dot(s, w_ref[0], precision=HIGHEST, preferred_element_type=F32) + b_ref[0]


def _ada(c_rows, ada_w, ada_b):
    depth, d, n = ada_w.shape
    rows = c_rows.shape[0]
    tn = 512
    return pl.pallas_call(
        _ada_kernel,
        grid=(depth, n // tn),
        in_specs=[pl.BlockSpec((rows, d), lambda l, j: (0, 0)),
                  pl.BlockSpec((1, d, tn), lambda l, j: (l, 0, j)),
                  pl.BlockSpec((1, 1, tn), lambda l, j: (l, 0, j))],
        out_specs=pl.BlockSpec((1, rows, tn), lambda l, j: (l, 0, j)),
        out_shape=jax.ShapeDtypeStruct((depth, rows, n), F32),
        compiler_params=_params("parallel", "parallel"),
        name="ada",
    )(c_rows, ada_w, ada_b.reshape(depth, 1, n))


def _nmm_kernel(x_ref, g_ref, sh_ref, sc_ref, w_ref, o_ref, a_scr, *, silu_cols):
    @pl.when(pl.program_id(2) == 0)
    def _():
        a = _rms(x_ref[0], g_ref[...]) * (1.0 + sc_ref[0]) + sh_ref[0]
        a_scr[...] = a.astype(BF16)

    if silu_cols:
        o_ref[0, :, :silu_cols] = _silu(_dot(a_scr[...], w_ref[:, :silu_cols])).astype(o_ref.dtype)
        o_ref[0, :, silu_cols:] = _dot(a_scr[...], w_ref[:, silu_cols:]).astype(o_ref.dtype)
    else:
        o_ref[0] = _dot(a_scr[...], w_ref[...]).astype(o_ref.dtype)


def _norm_mod_matmul(x, g, shift, scale, w, tm, tn, silu_cols=0):
    b, t, d = x.shape
    n = w.shape[1]
    assert silu_cols == 0 or tn == n
    return pl.pallas_call(
        functools.partial(_nmm_kernel, silu_cols=silu_cols),
        grid=(b, t // tm, n // tn),
        in_specs=[pl.BlockSpec((1, tm, d), lambda bi, i, j: (bi, i, 0)),
                  pl.BlockSpec((1, d), lambda bi, i, j: (0, 0)),
                  pl.BlockSpec((1, 1, d), lambda bi, i, j: (bi, 0, 0)),
                  pl.BlockSpec((1, 1, d), lambda bi, i, j: (bi, 0, 0)),
                  pl.BlockSpec((d, tn), lambda bi, i, j: (0, j))],
        out_specs=pl.BlockSpec((1, tm, tn), lambda bi, i, j: (bi, i, j)),
        out_shape=jax.ShapeDtypeStruct((b, t, n), BF16),
        scratch_shapes=[pltpu.VMEM((tm, d), BF16)],
        compiler_params=_params("parallel", "parallel", "arbitrary"),
        name="norm_mod_matmul",
    )(x, g.reshape(1, d), shift, scale, w)


def _mlp_kernel(h_ref, g_ref, sh_ref, sc_ref, gate_ref, w1_ref, w2_ref, fg_ref, o_ref,
                a_scr, acc_scr, *, final_norm):
    j = pl.program_id(2)
    last = pl.num_programs(2) - 1
    tm = a_scr.shape[0]
    rc = MLP_ROW_CHUNK if tm % MLP_ROW_CHUNK == 0 else tm

    def hidden(a):
        hid = jnp.maximum(_dot(a, w1_ref[...]), 0.0)
        return _dot((hid * hid).astype(BF16), w2_ref[...])

    @pl.when(j == 0)
    def _():
        for c in range(tm // rc):
            rows = pl.ds(c * rc, rc)
            a = _rms(h_ref[0, rows, :], g_ref[...]) * (1.0 + sc_ref[0]) + sh_ref[0]
            a = a.astype(BF16)
            a_scr[rows, :] = a
            acc_scr[rows, :] = hidden(a)

    @pl.when(jnp.logical_and(j > 0, j < last))
    def _():
        acc_scr[...] += hidden(a_scr[...])

    @pl.when(j == last)
    def _():
        for c in range(tm // rc):
            rows = pl.ds(c * rc, rc)
            out = h_ref[0, rows, :] + gate_ref[0] * (acc_scr[rows, :] + hidden(a_scr[rows, :]))
            if final_norm:
                out = _rms(out, fg_ref[...])
            o_ref[0, rows, :] = out


def _mlp(h, g, shift, scale, gate, w1, w2, final_g, tm, th, final_norm):
    b, t, d = h.shape
    hid = w1.shape[1]
    assert hid // th >= 2
    vec = pl.BlockSpec((1, 1, d), lambda bi, i, j: (bi, 0, 0))
    par = pl.BlockSpec((1, d), lambda bi, i, j: (0, 0))
    return pl.pallas_call(
        functools.partial(_mlp_kernel, final_norm=final_norm),
        grid=(b, t // tm, hid // th),
        in_specs=[pl.BlockSpec((1, tm, d), lambda bi, i, j: (bi, i, 0)),
                  par, vec, vec, vec,
                  pl.BlockSpec((d, th), lambda bi, i, j: (0, j)),
                  pl.BlockSpec((th, d), lambda bi, i, j: (j, 0)),
                  par],
        out_specs=pl.BlockSpec((1, tm, d), lambda bi, i, j: (bi, i, 0)),
        out_shape=jax.ShapeDtypeStruct((b, t, d), F32),
        scratch_shapes=[pltpu.VMEM((tm, d), BF16), pltpu.VMEM((tm, d), F32)],
        compiler_params=_params("parallel", "parallel", "arbitrary"),
        name="mlp",
    )(h, g.reshape(1, d), shift, scale, gate, w1, w2, final_g.reshape(1, d))


def _rwkv_prep_kernel(p_ref, pp_ref, pn_ref, mup_ref, mun_ref, w0_ref, wup_ref, a0_ref, aup_ref,
                      gup_ref, kk_ref, ka_ref, rk_ref, bd_ref,
                      r_o, v_o, kkn_o, lw0_o, lw1_o, kd0_o, kd1_o, b0_o, b1_o, g_o, bonus_o,
                      *, rw_dim, w_lora, a_lora):
    i = pl.program_id(1)
    n = pl.num_programs(1)
    p = p_ref[0].astype(F32)
    tm = p.shape[0]
    row = lax.broadcasted_iota(jnp.int32, (SUBLANES, 1), 0)
    first = jnp.where(i > 0, pp_ref[0].astype(F32)[HALO_ROWS - 1:HALO_ROWS, :], 0.0)
    last = jnp.where(i < n - 1, pn_ref[0].astype(F32)[0:1, :], 0.0)
    prev = pltpu.roll(p, 1, axis=0)
    prev = jnp.concatenate([jnp.where(row == 0, first, prev[:SUBLANES]), prev[SUBLANES:]], axis=0)
    nxt = pltpu.roll(p, tm - 1, axis=0)
    nxt = jnp.concatenate([nxt[:tm - SUBLANES],
                           jnp.where(row == SUBLANES - 1, last, nxt[tm - SUBLANES:])], axis=0)
    ps = p + (prev - p) * mup_ref[...] + (nxt - p) * mun_ref[...]

    r = ps[:, :rw_dim]
    k = ps[:, rw_dim:2 * rw_dim]
    v = ps[:, 2 * rw_dim:3 * rw_dim]
    off = 3 * rw_dim
    w_lo = ps[:, off:off + w_lora]
    a_lo = ps[:, off + w_lora:off + w_lora + a_lora]
    g_lo = ps[:, off + w_lora + a_lora:]

    w_pre = w0_ref[...] + _mm(jnp.tanh(w_lo), wup_ref[...])
    log_decay = -DECAY_SCALE_LOG2 * _sigmoid(w_pre)
    a = _sigmoid(a0_ref[...] + _mm(a_lo, aup_ref[...]))
    g = _mm(_sigmoid(g_lo), gup_ref[...])

    bd = bd_ref[...]
    kk = k * kk_ref[...]
    kk = kk / jnp.maximum(jnp.sqrt(_dot_sel_r(kk * kk, bd)), 1e-12)
    a_f = a[:, :rw_dim]
    a_b = a[:, rw_dim:]
    ka = ka_ref[...]
    kd_f = k * (1.0 + (a_f - 1.0) * ka)
    kd_b = k * (1.0 + (a_b - 1.0) * ka)
    bonus = _dot_sel_r(r * (kd_f + kd_b) * rk_ref[...], bd) * v

    r_o[0] = r.astype(BF16)
    v_o[0] = v.astype(BF16)
    kkn_o[0] = kk.astype(BF16)
    lw0_o[0] = log_decay[:, :rw_dim]
    lw1_o[0] = log_decay[:, rw_dim:]
    kd0_o[0] = kd_f.astype(BF16)
    kd1_o[0] = kd_b.astype(BF16)
    b0_o[0] = (kk * a_f).astype(BF16)
    b1_o[0] = (kk * a_b).astype(BF16)
    g_o[0] = g.astype(BF16)
    bonus_o[0] = bonus.astype(BF16)


def _rwkv_prep(p, prm, tm):
    b, t, _ = p.shape
    rw_dim = prm["k_k"].shape[1]
    w_lora = prm["w_up"].shape[0]
    a_lora = prm["a_up"].shape[0]
    g_lora = prm["g_up"].shape[0]
    cols = 3 * rw_dim + w_lora + a_lora + g_lora
    n_halo = t // HALO_ROWS
    tm_halo = tm // HALO_ROWS

    def full(a):
        return pl.BlockSpec(a.shape, lambda bi, i: (0,) * a.ndim)

    consts = [prm["mu_prev"], prm["mu_next"], prm["w0"], prm["w_up"], prm["a0"], prm["a_up"],
              prm["g_up"], prm["k_k"], prm["k_a"], prm["r_k"], prm["bd"]]
    out_spec = pl.BlockSpec((1, tm, rw_dim), lambda bi, i: (bi, i, 0))
    out_dtypes = [BF16, BF16, BF16, F32, F32, BF16, BF16, BF16, BF16, BF16, BF16]
    return pl.pallas_call(
        functools.partial(_rwkv_prep_kernel, rw_dim=rw_dim, w_lora=w_lora, a_lora=a_lora),
        grid=(b, t // tm),
        in_specs=[pl.BlockSpec((1, tm, cols), lambda bi, i: (bi, i, 0)),
                  pl.BlockSpec((1, HALO_ROWS, cols),
                               lambda bi, i: (bi, jnp.maximum(i * tm_halo - 1, 0), 0)),
                  pl.BlockSpec((1, HALO_ROWS, cols),
                               lambda bi, i: (bi, jnp.minimum((i + 1) * tm_halo, n_halo - 1), 0))]
                 + [full(a) for a in consts],
        out_specs=[out_spec] * 11,
        out_shape=[jax.ShapeDtypeStruct((b, t, rw_dim), dt) for dt in out_dtypes],
        compiler_params=_params("parallel", "parallel"),
        name="rwkv_prep",
    )(p, p, p, *consts)


def _rwkv_chunk(kkd, rd, binv, kinv, bfin, kfin, v, s, wtot, incl2, strict2):
    c = v.shape[1]
    lanes = v.shape[2]
    la = lax.broadcasted_iota(jnp.int32, (1, 1, lanes), 2) < HEAD

    def only_a(z):
        return jnp.where(la, z, 0.0)

    def only_b(z):
        return jnp.where(la, 0.0, z)

    x = jnp.concatenate([kkd, rd], axis=1)
    gh_a = _bmm_nt(only_a(x), jnp.concatenate([binv, kinv, s], axis=1))
    gh_b = _bmm_nt(only_b(x), jnp.concatenate([kinv, binv, s], axis=1))
    a2_a = jnp.where(strict2, gh_a[:, :c, :2 * c], 0.0)
    a2_b = jnp.where(strict2, gh_b[:, :c, :2 * c], 0.0)
    v_a = only_a(v)
    v_b = only_b(v)
    akk = jnp.where(la, a2_b, a2_a)
    u = -(gh_a[:, :c, 2 * c:] + gh_b[:, :c, 2 * c:] + _bmm(akk, jnp.concatenate([v_b, v_a], axis=1)))
    nm = -jnp.where(la, a2_a, a2_b)
    for _ in range(c.bit_length() - 2):
        top = jnp.concatenate([only_a(u), only_a(nm)], axis=2)
        bot = jnp.concatenate([only_b(u), only_b(nm)], axis=2)
        w = _bmm(nm, jnp.concatenate([top, bot], axis=1))
        u = u + w[:, :, :lanes]
        nm = w[:, :, lanes:]
    u = u + _bmm(nm, jnp.concatenate([only_a(u), only_b(u)], axis=1))
    bmat = jnp.concatenate([jnp.where(incl2, gh_a[:, c:, :2 * c], 0.0),
                            jnp.where(incl2, gh_b[:, c:, :2 * c], 0.0)], axis=2)
    o = (gh_a[:, c:, 2 * c:] + gh_b[:, c:, 2 * c:]
         + _bmm(bmat, jnp.concatenate([only_a(u), v_a, v_b, only_b(u)], axis=1)))
    row = lax.broadcasted_iota(jnp.int32, s.shape, 1) // HEAD
    col = lax.broadcasted_iota(jnp.int32, s.shape, 2) // HEAD
    upd = _bmm_tn(jnp.concatenate([u, v], axis=1), jnp.concatenate([bfin, kfin], axis=1))
    s_new = s * wtot + jnp.where(row == col, upd, 0.0)
    return o, s_new


def _rwkv_decay_terms(r, v, kk, lw, kd, bm, reverse):
    c = r.shape[0]
    r, v, kk, kd, bm = (z.astype(F32) for z in (r, v, kk, kd, bm))
    incl, _ = _order_masks(c, reverse)
    cum = _dot_sel_l(incl.astype(BF16), lw)
    last = 0 if reverse else c - 1
    wtot = jnp.exp2(cum[last:last + 1, :])
    e_pos = jnp.exp2(cum)
    e_neg = 1.0 / e_pos
    e_fin = e_neg * wtot
    return (kk * jnp.exp2(cum - lw), r * e_pos, bm * e_neg, kd * e_neg, bm * e_fin,
            kd * e_fin, v, wtot)


def _rwkv_scan_kernel(rf, vf, kf, lwf, kdf, bf, rb, vb, kb, lwb, kdb, bb, s0_ref,
                      of_ref, ob_ref, sfin_ref, s_scr, *, heads):
    i = pl.program_id(1)

    nb = rf.shape[0]
    c = rf.shape[1]
    pair = 2 * HEAD

    @pl.when(i == 0)
    def _():
        s_scr[...] = s0_ref[...].reshape(s_scr.shape)

    pairs = heads // 2
    terms = [tm for bi in range(nb)
             for tm in (_rwkv_decay_terms(rf[bi], vf[bi], kf[bi], lwf[bi], kdf[bi], bf[bi], False),
                        _rwkv_decay_terms(rb[bi], vb[bi], kb[bi], lwb[bi], kdb[bi], bb[bi], True))]
    ops = [jnp.stack([z[:, p * pair:(p + 1) * pair] for z in zs for p in range(pairs)])
           for zs in zip(*terms)]
    n = nb * 2 * pairs
    t = lax.broadcasted_iota(jnp.int32, (n, c, 2 * c), 1)
    j = lax.broadcasted_iota(jnp.int32, (n, c, 2 * c), 2) % c
    rev = lax.broadcasted_iota(jnp.int32, (n, c, 2 * c), 0) // pairs % 2 == 1
    ahead = jnp.where(rev, t - j, j - t)
    o, s_new = _rwkv_chunk(*ops[:7], s_scr[...], ops[7], ahead <= 0, ahead < 0)
    s_scr[...] = s_new
    for bi in range(nb):
        base = bi * 2 * pairs
        of_ref[bi] = jnp.concatenate([o[base + p] for p in range(pairs)], axis=1).astype(of_ref.dtype)
        ob_ref[bi] = jnp.concatenate([o[base + pairs + p] for p in range(pairs)],
                                     axis=1).astype(ob_ref.dtype)

    @pl.when(i == pl.num_programs(1) - 1)
    def _():
        sfin_ref[...] = s_scr[...].reshape(sfin_ref.shape)


def _rwkv_scan(prep, s0):
    r, v, kk, lw0, lw1, kd0, kd1, b0, b1 = prep
    b, t, rw_dim = r.shape
    heads = rw_dim // HEAD
    n = t // CHUNK
    nb = SCAN_BATCH if b % SCAN_BATCH == 0 else 1
    fwd = pl.BlockSpec((nb, CHUNK, rw_dim), lambda bi, i: (bi, i, 0))
    bwd = pl.BlockSpec((nb, CHUNK, rw_dim), lambda bi, i: (bi, n - 1 - i, 0))
    st = pl.BlockSpec((nb,) + s0.shape[1:], lambda bi, i: (bi, 0, 0, 0))
    o_sds = jax.ShapeDtypeStruct((b, t, rw_dim), BF16)
    return pl.pallas_call(
        functools.partial(_rwkv_scan_kernel, heads=heads),
        grid=(b // nb, n),
        in_specs=[fwd] * 6 + [bwd] * 6 + [st],
        out_specs=[fwd, bwd, st],
        out_shape=[o_sds, o_sds, jax.ShapeDtypeStruct(s0.shape, F32)],
        scratch_shapes=[pltpu.VMEM((nb * s0.shape[1],) + s0.shape[2:], F32)],
        compiler_params=_params("parallel", "arbitrary"),
        name="rwkv_scan",
    )(r, v, kk, lw0, kd0, b0, r, v, kk, lw1, kd1, b1, s0)


def _softmax_pv(s_list, v_list):
    m = s_list[0].max(axis=-1, keepdims=True)
    for s in s_list[1:]:
        m = jnp.maximum(m, s.max(axis=-1, keepdims=True))
    num = None
    den = None
    for s, v in zip(s_list, v_list):
        p = jnp.exp(s - m)
        d = p.sum(axis=-1, keepdims=True)
        o = _dot(p.astype(BF16), v)
        num = o if num is None else num + o
        den = d if den is None else den + d
    return num / den


def _natten_kernel(q_ref, k_ref, v_ref, kc_ref, vc_ref, bias_ref, o_ref, *, rows, scale, rb):
    kc = kc_ref[0]
    vc = vc_ref[0]
    l = kc.shape[0]
    nwin = NA_ROWS * GRID_W
    head0 = lax.broadcasted_iota(jnp.int32, (1, 2 * HEAD), 1) < HEAD

    def body(it, carry):
        base = it * rb
        start = pl.multiple_of(base * GRID_W, rb * GRID_W)
        q = q_ref[0, pl.ds(start, rb * GRID_W), :] * scale
        zero = jnp.zeros_like(q)
        q0 = jnp.where(head0, q, zero)
        q1 = jnp.where(head0, zero, q)
        qs, kws, vws, bs = [], [], [], []
        for j in range(rb):
            r = base + j
            r0 = jnp.clip(r - NA_ROWS // 2, 0, rows - NA_ROWS)
            rows_j = slice(j * GRID_W, (j + 1) * GRID_W)
            qs.append(jnp.concatenate([q0[rows_j], q1[rows_j]], axis=0))
            win = pl.ds(pl.multiple_of(r0 * GRID_W, GRID_W), nwin)
            kws.append(k_ref[0, win, :])
            vws.append(v_ref[0, win, :])
            bs.append(bias_ref[0, r - r0])
        qs = jnp.stack(qs)
        s_win = _bmm_nt(qs, jnp.stack(kws)) + jnp.stack(bs)
        s_ctx = _dot_nt(qs.reshape(rb * 2 * GRID_W, 2 * HEAD), kc).reshape(rb, 2 * GRID_W, l)
        m = jnp.maximum(s_win.max(axis=-1, keepdims=True), s_ctx.max(axis=-1, keepdims=True))
        p_win = jnp.exp(s_win - m)
        p_ctx = jnp.exp(s_ctx - m)
        den = p_win.sum(axis=-1, keepdims=True) + p_ctx.sum(axis=-1, keepdims=True)
        o = _bmm(p_win, jnp.stack(vws))
        o = o + _dot(p_ctx.reshape(rb * 2 * GRID_W, l).astype(BF16), vc).reshape(rb, 2 * GRID_W, 2 * HEAD)
        o = o / den
        out = jnp.concatenate([jnp.where(head0, o[j, :GRID_W], o[j, GRID_W:]) for j in range(rb)], axis=0)
        o_ref[0, pl.ds(start, rb * GRID_W), :] = out.astype(o_ref.dtype)
        return carry

    lax.fori_loop(0, rows // rb, body, 0)


def _natten(p_lat, p_ctx, bias, rw_cols, na_dim):
    b, t, _ = p_lat.shape
    l = p_ctx.shape[1]
    pairs = na_dim // (2 * HEAD)
    qb = rw_cols // (2 * HEAD)
    kb = qb + pairs
    vb = kb + pairs
    rows = t // GRID_W

    def lat(off):
        return pl.BlockSpec((1, t, 2 * HEAD), lambda bi, hp: (bi, 0, off + hp))

    def ctx(off):
        return pl.BlockSpec((1, l, 2 * HEAD), lambda bi, hp: (bi, 0, off + hp))

    rb = 2 * NA_ROWS if rows % (2 * NA_ROWS) == 0 else 1
    return pl.pallas_call(
        functools.partial(_natten_kernel, rows=rows, scale=HEAD ** -0.5, rb=rb),
        grid=(b, pairs),
        in_specs=[lat(qb), lat(kb), lat(vb), ctx(kb), ctx(vb),
                  pl.BlockSpec((1, NA_ROWS, 2 * GRID_W, NA_ROWS * GRID_W), lambda bi, hp: (hp, 0, 0, 0))],
        out_specs=pl.BlockSpec((1, t, 2 * HEAD), lambda bi, hp: (bi, 0, hp)),
        out_shape=jax.ShapeDtypeStruct((b, t, na_dim), BF16),
        compiler_params=_params("parallel", "parallel"),
        name="natten",
    )(p_lat, p_lat, p_lat, p_ctx, p_ctx, bias)


def _ctx_attn_kernel(q_ref, k_ref, v_ref, o_ref, *, scale):
    q = q_ref[0] * scale
    k = k_ref[0]
    v = v_ref[0]
    outs = []
    for hh in range(2):
        sl = slice(hh * HEAD, (hh + 1) * HEAD)
        outs.append(_softmax_pv([_dot_nt(q[:, sl], k[:, sl])], [v[:, sl]]))
    o_ref[0] = jnp.concatenate(outs, axis=1).astype(o_ref.dtype)


def _ctx_attn(p_ctx, rw_cols, na_dim):
    b, l, _ = p_ctx.shape
    pairs = na_dim // (2 * HEAD)
    qb = rw_cols // (2 * HEAD)

    def blk(off):
        return pl.BlockSpec((1, l, 2 * HEAD), lambda bi, hp: (bi, 0, off + hp))

    return pl.pallas_call(
        functools.partial(_ctx_attn_kernel, scale=HEAD ** -0.5),
        grid=(b, pairs),
        in_specs=[blk(qb), blk(qb + pairs), blk(qb + 2 * pairs)],
        out_specs=pl.BlockSpec((1, l, 2 * HEAD), lambda bi, hp: (bi, 0, hp)),
        out_shape=jax.ShapeDtypeStruct((b, l, na_dim), BF16),
        compiler_params=_params("parallel", "parallel"),
        name="ctx_attn",
    )(p_ctx, p_ctx, p_ctx)


def _natten_bias(rpb):
    cols = np.arange(GRID_W)
    c0 = np.clip(cols - NA_COLS // 2, 0, GRID_W - NA_COLS)
    valid = (cols[None, :] >= c0[:, None]) & (cols[None, :] < c0[:, None] + NA_COLS)
    col_rel = cols[None, :] - cols[:, None] + NA_COLS - 1
    onehot = (col_rel[:, :, None] == np.arange(2 * NA_COLS - 1)).astype(np.float32)
    toe = jnp.einsum("hrj,qkj->hrqk", rpb.astype(F32), onehot, precision=HIGHEST)
    toe = jnp.where(valid[None, None], toe, MASK_BIAS)
    tab = jnp.stack([toe[:, NA_ROWS - 1 - d:2 * NA_ROWS - 1 - d] for d in range(NA_ROWS)], axis=1)
    tab = tab.transpose(0, 1, 3, 2, 4)
    heads = rpb.shape[0]
    tab = tab.reshape(heads // 2, 2, NA_ROWS, GRID_W, NA_ROWS * GRID_W)
    return tab.transpose(0, 2, 1, 3, 4).reshape(heads // 2, NA_ROWS, 2 * GRID_W, NA_ROWS * GRID_W)


def _even_out_kernel(h_ref, of_ref, ob_ref, bonus_ref, g_ref, na_ref, w_ref, gate_ref,
                     lng_ref, lnb_ref, bd_ref, o_ref):
    bd = bd_ref[...]
    o = of_ref[0].astype(F32) + ob_ref[0].astype(F32)
    inv = 1.0 / HEAD
    mu = _dot_sel_r(o, bd) * inv
    xc = o - mu
    var = _dot_sel_r(xc * xc, bd) * inv
    y = xc * lax.rsqrt(var + RW_GN_EPS) * lng_ref[...] + lnb_ref[...]
    rw = (y + bonus_ref[0].astype(F32)) * g_ref[0].astype(F32)
    cat = jnp.concatenate([rw.astype(BF16), na_ref[0]], axis=1)
    o_ref[0] = h_ref[0] + gate_ref[0] * _dot(cat, w_ref[...])


def _even_out(h, o_f, o_b, bonus, g, na, w_out, gate, ln_g, ln_b, bd, tm):
    b, t, d = h.shape
    rw_dim = o_f.shape[2]
    na_dim = na.shape[2]
    big = pl.BlockSpec((1, tm, d), lambda bi, i: (bi, i, 0))
    half = pl.BlockSpec((1, tm, rw_dim), lambda bi, i: (bi, i, 0))
    par = pl.BlockSpec((1, rw_dim), lambda bi, i: (0, 0))
    return pl.pallas_call(
        _even_out_kernel,
        grid=(b, t // tm),
        in_specs=[big, half, half, half, half,
                  pl.BlockSpec((1, tm, na_dim), lambda bi, i: (bi, i, 0)),
                  pl.BlockSpec(w_out.shape, lambda bi, i: (0, 0)),
                  pl.BlockSpec((1, 1, d), lambda bi, i: (bi, 0, 0)),
                  par, par,
                  pl.BlockSpec(bd.shape, lambda bi, i: (0, 0))],
        out_specs=big,
        out_shape=jax.ShapeDtypeStruct((b, t, d), F32),
        compiler_params=_params("parallel", "parallel"),
        name="even_out",
    )(h, o_f, o_b, bonus, g, na, w_out, gate, ln_g.reshape(1, rw_dim), ln_b.reshape(1, rw_dim), bd)


def _odd_out_kernel(h_ref, of_ref, ob_ref, gate_in_ref, w_ref, gate_ref, ng_ref, o_ref):
    o = of_ref[0].astype(F32) + ob_ref[0].astype(F32)
    y = _rms(o, ng_ref[...]) * _silu(gate_in_ref[0].astype(F32))
    o_ref[0] = h_ref[0] + gate_ref[0] * _dot(y.astype(BF16), w_ref[...])


def _odd_out(h, o_f, o_b, p, gate_block, w_out, gate, norm_g, tm):
    b, t, d = h.shape
    vd = o_f.shape[2]
    big = pl.BlockSpec((1, tm, d), lambda bi, i: (bi, i, 0))
    val = pl.BlockSpec((1, tm, vd), lambda bi, i: (bi, i, 0))
    return pl.pallas_call(
        _odd_out_kernel,
        grid=(b, t // tm),
        in_specs=[big, val, val,
                  pl.BlockSpec((1, tm, vd), lambda bi, i: (bi, i, gate_block)),
                  pl.BlockSpec(w_out.shape, lambda bi, i: (0, 0)),
                  pl.BlockSpec((1, 1, d), lambda bi, i: (bi, 0, 0)),
                  pl.BlockSpec((1, vd), lambda bi, i: (0, 0))],
        out_specs=big,
        out_shape=jax.ShapeDtypeStruct((b, t, d), F32),
        compiler_params=_params("parallel", "parallel"),
        name="odd_out",
    )(h, o_f, o_b, p, w_out, gate, norm_g.reshape(1, vd))


def _hgrn_terms(qp, fp, ip, lb, reverse):
    c = qp.shape[0]
    incl, _ = _order_masks(c, reverse)
    q = qp.astype(F32)
    forget = lb + (1.0 - lb) * _sigmoid(fp.astype(F32))
    k = 1.0 - forget
    cum = _dot_sel_l(incl.astype(BF16), jnp.log2(forget))
    last = 0 if reverse else c - 1
    tot = cum[last:last + 1, :]
    mid = cum[c // 2:c // 2 + 1, :]
    e_mid = jnp.exp2(cum - mid)
    q_mid = q * e_mid
    k_mid = k / e_mid
    return q_mid, k_mid, q_mid * jnp.exp2(mid), k_mid * jnp.exp2(tot - mid), ip, jnp.exp2(tot)


def _hgrn_scan_kernel(qf, ff, vf, qb, fb, vb, hl_ref, s0_ref, of_ref, ob_ref, sfin_ref, s_scr,
                      *, layer, heads):
    i = pl.program_id(1)

    @pl.when(i == 0)
    def _():
        s_scr[...] = s0_ref[...].reshape(s_scr.shape)

    hl = hl_ref[...]
    e = jnp.exp(hl - hl.max(axis=0, keepdims=True))
    sm = e / e.sum(axis=0, keepdims=True)
    lb = jnp.zeros_like(sm[0:1])
    for j in range(1, layer + 1):
        lb = lb + sm[j:j + 1]

    nb = qf.shape[0]
    c = qf.shape[1]
    n = nb * 2 * heads
    terms = [tm for bi in range(nb) for tm in (_hgrn_terms(qf[bi], ff[bi], vf[bi], lb, False),
                                               _hgrn_terms(qb[bi], fb[bi], vb[bi], lb, True))]
    q_mid, k_mid, q_cum, k_fin, v, wtot = [
        jnp.stack([z[:, h * HG_KEY:(h + 1) * HG_KEY] for z in zs for h in range(heads)])
        for zs in zip(*terms)]
    t = lax.broadcasted_iota(jnp.int32, (n, c, c), 1)
    j = lax.broadcasted_iota(jnp.int32, (n, c, c), 2)
    rev = lax.broadcasted_iota(jnp.int32, (n, c, c), 0) // heads % 2 == 1
    incl = jnp.where(rev, t - j, j - t) <= 0
    s = s_scr[...]
    att = jnp.where(incl, _bmm_nt(q_mid, k_mid), 0.0)
    o = _bmm(att, v) + _bmm_nt(q_cum, s)
    s_scr[...] = s * wtot + _bmm_tn(v, k_fin)
    for bi in range(nb):
        base = bi * 2 * heads
        of_ref[bi] = jnp.concatenate([o[base + h] for h in range(heads)], axis=1).astype(of_ref.dtype)
        ob_ref[bi] = jnp.concatenate([o[base + heads + h] for h in range(heads)],
                                     axis=1).astype(ob_ref.dtype)

    @pl.when(i == pl.num_programs(1) - 1)
    def _():
        sfin_ref[...] = s_scr[...].reshape(sfin_ref.shape)


def _hgrn_scan(p, hg_lower, s0, layer, heads):
    b, t, _ = p.shape
    n = t // CHUNK
    width = heads * HG_KEY
    depth = hg_lower.shape[0]

    nb = SCAN_BATCH if b % SCAN_BATCH == 0 else 1

    def fwd(blk):
        return pl.BlockSpec((nb, CHUNK, width), lambda bi, i: (bi, i, blk))

    def bwd(blk):
        return pl.BlockSpec((nb, CHUNK, width), lambda bi, i: (bi, n - 1 - i, blk))

    st = pl.BlockSpec((nb, 2 * heads, HG_KEY, HG_KEY), lambda bi, i: (bi, 0, 0, 0))
    o_sds = jax.ShapeDtypeStruct((b, t, width), BF16)
    return pl.pallas_call(
        functools.partial(_hgrn_scan_kernel, layer=layer, heads=heads),
        grid=(b // nb, n),
        in_specs=[fwd(0), fwd(1), fwd(3), bwd(0), bwd(2), bwd(3),
                  pl.BlockSpec((depth, width), lambda bi, i: (0, 0)), st],
        out_specs=[fwd(0), bwd(0), st],
        out_shape=[o_sds, o_sds, jax.ShapeDtypeStruct((b, 2 * heads, HG_KEY, HG_KEY), F32)],
        scratch_shapes=[pltpu.VMEM((nb * 2 * heads, HG_KEY, HG_KEY), F32)],
        compiler_params=_params("parallel", "arbitrary"),
        name="hgrn_scan",
    )(p, p, p, p, p, p, hg_lower, s0)


def _row_tile(t, target):
    return target if t % target == 0 else t


def kernel(x, c, ctx, c_ctx, norm_mix_g, norm_mlp_g, ada_w, ada_b, mlp_w1, mlp_w2, ev_w_in,
           ev_w_out, rw_mu_prev, rw_mu_next, rw_w0, rw_w_up, rw_a0, rw_a_up, rw_g_up, rw_k_k,
           rw_k_a, rw_r_k, rw_ln_g, rw_ln_b, na_rpb, od_w_in, od_w_out, hg_lower, hg_norm_g,
           final_norm_g):
    b, t, d = x.shape
    l = ctx.shape[1]
    depth = ada_w.shape[0]
    rw_dim = rw_k_k.shape[1]
    rw_heads = rw_dim // HEAD
    rw_cols = rw_mu_prev.shape[1]
    na_dim = d - rw_dim
    hg_heads = d // HG_KEY

    pad_rows = -(b + 1) % SUBLANES
    c_rows = jnp.concatenate([c, c_ctx[None], jnp.zeros((pad_rows, d), F32)], axis=0)
    mods = _ada(c_rows, ada_w, ada_b)

    def mod_vecs(layer):
        m = mods[layer]
        lat = [m[:b, j * d:(j + 1) * d].reshape(b, 1, d) for j in range(6)]
        cx = [m[b, j * d:(j + 1) * d].reshape(1, 1, d) for j in range(6)]
        return lat, cx

    eye = jnp.arange(rw_dim) // HEAD
    bd = (eye[:, None] == eye[None, :]).astype(BF16)

    def flat(z):
        return z.reshape(1, b * l, z.shape[-1])

    def per_sample(z):
        return z.reshape(b, l, z.shape[-1])

    tm_lat = _row_tile(t, 512)
    tm_ctx = _row_tile(b * l, 512)
    tm_mlp_ctx = _row_tile(b * l, 1024)
    h_lat, h_ctx = x, flat(ctx)
    for layer in range(depth):
        last = layer == depth - 1
        (sh1, s1, g1, sh2, s2, g2), (csh1, cs1, cg1, csh2, cs2, cg2) = mod_vecs(layer)
        if layer % 2 == 0:
            e = layer // 2
            w_in = ev_w_in[e].astype(BF16)
            n_in = w_in.shape[1]
            p_lat = _norm_mod_matmul(h_lat, norm_mix_g[layer], sh1, s1, w_in, tm_lat, n_in)
            p_ctx = per_sample(_norm_mod_matmul(h_ctx, norm_mix_g[layer], csh1, cs1, w_in, tm_ctx, n_in))
            prm = dict(
                mu_prev=rw_mu_prev[e][None], mu_next=rw_mu_next[e][None],
                w0=rw_w0[e].reshape(1, 2 * rw_dim),
                w_up=jnp.concatenate([rw_w_up[e, 0], rw_w_up[e, 1]], axis=1),
                a0=rw_a0[e].reshape(1, 2 * rw_dim),
                a_up=jnp.concatenate([rw_a_up[e, 0], rw_a_up[e, 1]], axis=1),
                g_up=rw_g_up[e], k_k=rw_k_k[e][None], k_a=rw_k_a[e][None],
                r_k=rw_r_k[e].reshape(1, rw_dim), bd=bd)
            s_zero = jnp.zeros((b, rw_heads, 2 * HEAD, 2 * HEAD), F32)
            prep_c = _rwkv_prep(p_ctx, prm, _row_tile(l, 256))
            oc_f, oc_b, s_ctx = _rwkv_scan(prep_c[:9], s_zero)
            prep_l = _rwkv_prep(p_lat, prm, tm_lat)
            ol_f, ol_b, _ = _rwkv_scan(prep_l[:9], s_ctx)
            na_lat = _natten(p_lat, p_ctx, _natten_bias(na_rpb[e]), rw_cols, na_dim)
            w_out = ev_w_out[e].astype(BF16)
            h_lat = _even_out(h_lat, ol_f, ol_b, prep_l[10], prep_l[9], na_lat, w_out, g1,
                              rw_ln_g[e], rw_ln_b[e], bd, tm_lat)
            if not last:
                na_ctx = _ctx_attn(p_ctx, rw_cols, na_dim)
                h_ctx = _even_out(h_ctx, flat(oc_f), flat(oc_b), flat(prep_c[10]), flat(prep_c[9]),
                                  flat(na_ctx), w_out, cg1, rw_ln_g[e], rw_ln_b[e], bd, tm_ctx)
        else:
            o = layer // 2
            w_in = od_w_in[o].astype(BF16)
            n_in = w_in.shape[1]
            hg_kdim = hg_heads * HG_KEY
            p_lat = _norm_mod_matmul(h_lat, norm_mix_g[layer], sh1, s1, w_in, tm_lat, n_in, hg_kdim)
            p_ctx = _norm_mod_matmul(h_ctx, norm_mix_g[layer], csh1, cs1, w_in, tm_ctx, n_in, hg_kdim)
            s_zero = jnp.zeros((b, 2 * hg_heads, HG_KEY, HG_KEY), F32)
            oc_f, oc_b, s_ctx = _hgrn_scan(per_sample(p_ctx), hg_lower, s_zero, layer, hg_heads)
            ol_f, ol_b, _ = _hgrn_scan(p_lat, hg_lower, s_ctx, layer, hg_heads)
            w_out = od_w_out[o].astype(BF16)
            gate_block = (n_in - d) // d
            h_lat = _odd_out(h_lat, ol_f, ol_b, p_lat, gate_block, w_out, g1, hg_norm_g[o], tm_lat)
            if not last:
                h_ctx = _odd_out(h_ctx, flat(oc_f), flat(oc_b), p_ctx, gate_block, w_out, cg1,
                                 hg_norm_g[o], tm_ctx)
        w1 = mlp_w1[layer].astype(BF16)
        w2 = mlp_w2[layer].astype(BF16)
        h_lat = _mlp(h_lat, norm_mlp_g[layer], sh2, s2, g2, w1, w2, final_norm_g,
                     _row_tile(t, 1024), 1024, last)
        if not last:
            h_ctx = _mlp(h_ctx, norm_mlp_g[layer], csh2, cs2, cg2, w1, w2, final_norm_g,
                         tm_mlp_ctx, 1024, False)
    return h_lat
```

```python
import functools
import math

import jax
import jax.numpy as jnp
import numpy as np
from jax import lax
from jax.experimental import pallas as pl
from jax.experimental.pallas import tpu as pltpu

F32 = jnp.float32
BF16 = jnp.bfloat16
HIGHEST = lax.Precision.HIGHEST

NORM_EPS = 1e-6
RW_GN_EPS = 64e-5
HEAD = 64
GRID_W = 64
NA_ROWS = 8
NA_COLS = 16
HG_KEY = 128
CHUNK = 64
MLP_ROW_CHUNK = 256
SCAN_BATCH = 4
MASK_BIAS = -1e30
DECAY_SCALE_LOG2 = math.exp(-0.5) / math.log(2.0)
SUBLANES = 8
HALO_ROWS = 16
VMEM_LIMIT = 52 * 1024 * 1024


def _params(*sem):
    return pltpu.CompilerParams(dimension_semantics=sem, vmem_limit_bytes=VMEM_LIMIT)


def _dot(a, b):
    return jnp.dot(a, b, preferred_element_type=F32)


def _dot_nt(a, b):
    return lax.dot_general(a, b, (((1,), (1,)), ((), ())), preferred_element_type=F32)


def _dot_tn(a, b):
    return lax.dot_general(a, b, (((0,), (0,)), ((), ())), preferred_element_type=F32)


def _mm(a, b):
    return _dot(a.astype(BF16), b.astype(BF16))


def _mm_nt(a, b):
    return _dot_nt(a.astype(BF16), b.astype(BF16))


def _mm_tn(a, b):
    return _dot_tn(a.astype(BF16), b.astype(BF16))


def _bmm(a, b):
    return lax.dot_general(a.astype(BF16), b.astype(BF16), (((2,), (1,)), ((0,), (0,))),
                           preferred_element_type=F32)


def _bmm_nt(a, b):
    return lax.dot_general(a.astype(BF16), b.astype(BF16), (((2,), (2,)), ((0,), (0,))),
                           preferred_element_type=F32)


def _bmm_tn(a, b):
    return lax.dot_general(a.astype(BF16), b.astype(BF16), (((1,), (1,)), ((0,), (0,))),
                           preferred_element_type=F32)


def _hi_lo(x):
    hi = x.astype(BF16)
    return hi, (x - hi.astype(F32)).astype(BF16)


def _dot_sel_r(x, sel):
    return _dot(x.astype(BF16), sel)


def _dot_sel_l(sel, x):
    hi, lo = _hi_lo(x)
    return _dot(sel, hi) + _dot(sel, lo)


def _sigmoid(x):
    return 1.0 / (1.0 + jnp.exp(-x))


def _silu(x):
    return x * _sigmoid(x)


def _rms(x, g):
    return x * lax.rsqrt(jnp.mean(x * x, axis=-1, keepdims=True) + NORM_EPS) * g


def _order_masks(n, reverse):
    t = lax.broadcasted_iota(jnp.int32, (n, n), 0)
    i = lax.broadcasted_iota(jnp.int32, (n, n), 1)
    if reverse:
        return i >= t, i > t
    return i <= t, i < t


def _ada_kernel(c_ref, w_ref, b_ref, o_ref):
    s_hi, s_lo = _hi_lo(_silu(c_ref[...]))
    w_hi, w_lo = _hi_lo(w_ref[0])
    o_ref[0] = _dot(s_hi, w_hi) + (_dot(s_hi, w_lo) + _dot(s_lo, w_hi)) + b_ref[0]


def _ada(c_rows, ada_w, ada_b):
    depth, d, n = ada_w.shape
    rows = c_rows.shape[0]
    tn = 512
    return pl.pallas_call(
        _ada_kernel,
        grid=(depth, n // tn),
        in_specs=[pl.BlockSpec((rows, d), lambda l, j: (0, 0)),
                  pl.BlockSpec((1, d, tn), lambda l, j: (l, 0, j)),
                  pl.BlockSpec((1, 1, tn), lambda l, j: (l, 0, j))],
        out_specs=pl.BlockSpec((1, rows, tn), lambda l, j: (l, 0, j)),
        out_shape=jax.ShapeDtypeStruct((depth, rows, n), F32),
        compiler_params=_params("parallel", "parallel"),
        name="ada",
    )(c_rows, ada_w, ada_b.reshape(depth, 1, n))


def _nmm_kernel(x_ref, g_ref, sh_ref, sc_ref, w_ref, o_ref, *, silu_cols):
    tm = x_ref.shape[1]
    rc = MLP_ROW_CHUNK if tm % MLP_ROW_CHUNK == 0 else tm
    for c in range(tm // rc):
        rows = pl.ds(c * rc, rc)
        a = (_rms(x_ref[0, rows, :], g_ref[...]) * (1.0 + sc_ref[0]) + sh_ref[0]).astype(BF16)
        if silu_cols:
            o_ref[0, rows, :silu_cols] = _silu(_dot(a, w_ref[:, :silu_cols])).astype(o_ref.dtype)
            o_ref[0, rows, silu_cols:] = _dot(a, w_ref[:, silu_cols:]).astype(o_ref.dtype)
        else:
            o_ref[0, rows, :] = _dot(a, w_ref[...]).astype(o_ref.dtype)


def _norm_mod_matmul(x, g, shift, scale, w, tm, silu_cols=0):
    b, t, d = x.shape
    n = w.shape[1]
    return pl.pallas_call(
        functools.partial(_nmm_kernel, silu_cols=silu_cols),
        grid=(b, t // tm),
        in_specs=[pl.BlockSpec((1, tm, d), lambda bi, i: (bi, i, 0)),
                  pl.BlockSpec((1, d), lambda bi, i: (0, 0)),
                  pl.BlockSpec((1, 1, d), lambda bi, i: (bi, 0, 0)),
                  pl.BlockSpec((1, 1, d), lambda bi, i: (bi, 0, 0)),
                  pl.BlockSpec((d, n), lambda bi, i: (0, 0))],
        out_specs=pl.BlockSpec((1, tm, n), lambda bi, i: (bi, i, 0)),
        out_shape=jax.ShapeDtypeStruct((b, t, n), BF16),
        compiler_params=_params("parallel", "parallel"),
        name="norm_mod_matmul",
    )(x, g.reshape(1, d), shift, scale, w)


def _mlp_kernel(h_ref, g_ref, sh_ref, sc_ref, gate_ref, w1_ref, w2_ref, fg_ref, o_ref,
                a_scr, acc_scr, *, final_norm):
    j = pl.program_id(2)
    last = pl.num_programs(2) - 1
    tm = a_scr.shape[0]
    rc = MLP_ROW_CHUNK if tm % MLP_ROW_CHUNK == 0 else tm

    def hidden(a):
        hid = jnp.maximum(_dot(a, w1_ref[...]), 0.0)
        return _dot((hid * hid).astype(BF16), w2_ref[...])

    @pl.when(j == 0)
    def _():
        for c in range(tm // rc):
            rows = pl.ds(c * rc, rc)
            a = _rms(h_ref[0, rows, :], g_ref[...]) * (1.0 + sc_ref[0]) + sh_ref[0]
            a = a.astype(BF16)
            a_scr[rows, :] = a
            acc_scr[rows, :] = hidden(a)

    @pl.when(jnp.logical_and(j > 0, j < last))
    def _():
        acc_scr[...] += hidden(a_scr[...])

    @pl.when(j == last)
    def _():
        for c in range(tm // rc):
            rows = pl.ds(c * rc, rc)
            out = h_ref[0, rows, :] + gate_ref[0] * (acc_scr[rows, :] + hidden(a_scr[rows, :]))
            if final_norm:
                out = _rms(out, fg_ref[...])
            o_ref[0, rows, :] = out


def _mlp(h, g, shift, scale, gate, w1, w2, final_g, tm, th, final_norm):
    b, t, d = h.shape
    hid = w1.shape[1]
    assert hid // th >= 2
    vec = pl.BlockSpec((1, 1, d), lambda bi, i, j: (bi, 0, 0))
    par = pl.BlockSpec((1, d), lambda bi, i, j: (0, 0))
    return pl.pallas_call(
        functools.partial(_mlp_kernel, final_norm=final_norm),
        grid=(b, t // tm, hid // th),
        in_specs=[pl.BlockSpec((1, tm, d), lambda bi, i, j: (bi, i, 0)),
                  par, vec, vec, vec,
                  pl.BlockSpec((d, th), lambda bi, i, j: (0, j)),
                  pl.BlockSpec((th, d), lambda bi, i, j: (j, 0)),
                  par],
        out_specs=pl.BlockSpec((1, tm, d), lambda bi, i, j: (bi, i, 0)),
        out_shape=jax.ShapeDtypeStruct((b, t, d), F32),
        scratch_shapes=[pltpu.VMEM((tm, d), BF16), pltpu.VMEM((tm, d), F32)],
        compiler_params=_params("parallel", "parallel", "arbitrary"),
        name="mlp",
    )(h, g.reshape(1, d), shift, scale, gate, w1, w2, final_g.reshape(1, d))


def _rwkv_prep_kernel(p_ref, pp_ref, pn_ref, mup_ref, mun_ref, w0_ref, wup_ref, a0_ref, aup_ref,
                      gup_ref, kk_ref, ka_ref, rk_ref, bd_ref,
                      r_o, v_o, kkn_o, lw0_o, lw1_o, kd0_o, kd1_o, b0_o, b1_o, g_o, bonus_o,
                      *, rw_dim, w_lora, a_lora):
    i = pl.program_id(1)
    n = pl.num_programs(1)
    p = p_ref[0].astype(F32)
    tm = p.shape[0]
    row = lax.broadcasted_iota(jnp.int32, (SUBLANES, 1), 0)
    first = jnp.where(i > 0, pp_ref[0].astype(F32)[HALO_ROWS - 1:HALO_ROWS, :], 0.0)
    last = jnp.where(i < n - 1, pn_ref[0].astype(F32)[0:1, :], 0.0)
    prev = pltpu.roll(p, 1, axis=0)
    prev = jnp.concatenate([jnp.where(row == 0, first, prev[:SUBLANES]), prev[SUBLANES:]], axis=0)
    nxt = pltpu.roll(p, tm - 1, axis=0)
    nxt = jnp.concatenate([nxt[:tm - SUBLANES],
                           jnp.where(row == SUBLANES - 1, last, nxt[tm - SUBLANES:])], axis=0)
    ps = p + (prev - p) * mup_ref[...] + (nxt - p) * mun_ref[...]

    r = ps[:, :rw_dim]
    k = ps[:, rw_dim:2 * rw_dim]
    v = ps[:, 2 * rw_dim:3 * rw_dim]
    off = 3 * rw_dim
    w_lo = ps[:, off:off + w_lora]
    a_lo = ps[:, off + w_lora:off + w_lora + a_lora]
    g_lo = ps[:, off + w_lora + a_lora:]

    w_pre = w0_ref[...] + _mm(jnp.tanh(w_lo), wup_ref[...])
    log_decay = -DECAY_SCALE_LOG2 * _sigmoid(w_pre)
    a = _sigmoid(a0_ref[...] + _mm(a_lo, aup_ref[...]))
    g = _mm(_sigmoid(g_lo), gup_ref[...])

    bd = bd_ref[...]
    kk = k * kk_ref[...]
    kk = kk / jnp.maximum(jnp.sqrt(_dot_sel_r(kk * kk, bd)), 1e-12)
    a_f = a[:, :rw_dim]
    a_b = a[:, rw_dim:]
    ka = ka_ref[...]
    kd_f = k * (1.0 + (a_f - 1.0) * ka)
    kd_b = k * (1.0 + (a_b - 1.0) * ka)
    bonus = _dot_sel_r(r * (kd_f + kd_b) * rk_ref[...], bd) * v

    r_o[0] = r.astype(BF16)
    v_o[0] = v.astype(BF16)
    kkn_o[0] = kk.astype(BF16)
    lw0_o[0] = log_decay[:, :rw_dim]
    lw1_o[0] = log_decay[:, rw_dim:]
    kd0_o[0] = kd_f.astype(BF16)
    kd1_o[0] = kd_b.astype(BF16)
    b0_o[0] = (kk * a_f).astype(BF16)
    b1_o[0] = (kk * a_b).astype(BF16)
    g_o[0] = g.astype(BF16)
    bonus_o[0] = bonus.astype(BF16)


def _rwkv_prep(p, prm, tm):
    b, t, _ = p.shape
    rw_dim = prm["k_k"].shape[1]
    w_lora = prm["w_up"].shape[0]
    a_lora = prm["a_up"].shape[0]
    g_lora = prm["g_up"].shape[0]
    cols = 3 * rw_dim + w_lora + a_lora + g_lora
    n_halo = t // HALO_ROWS
    tm_halo = tm // HALO_ROWS

    def full(a):
        return pl.BlockSpec(a.shape, lambda bi, i: (0,) * a.ndim)

    consts = [prm["mu_prev"], prm["mu_next"], prm["w0"], prm["w_up"], prm["a0"], prm["a_up"],
              prm["g_up"], prm["k_k"], prm["k_a"], prm["r_k"], prm["bd"]]
    out_spec = pl.BlockSpec((1, tm, rw_dim), lambda bi, i: (bi, i, 0))
    out_dtypes = [BF16, BF16, BF16, F32, F32, BF16, BF16, BF16, BF16, BF16, BF16]
    return pl.pallas_call(
        functools.partial(_rwkv_prep_kernel, rw_dim=rw_dim, w_lora=w_lora, a_lora=a_lora),
        grid=(b, t // tm),
        in_specs=[pl.BlockSpec((1, tm, cols), lambda bi, i: (bi, i, 0)),
                  pl.BlockSpec((1, HALO_ROWS, cols),
                               lambda bi, i: (bi, jnp.maximum(i * tm_halo - 1, 0), 0)),
                  pl.BlockSpec((1, HALO_ROWS, cols),
                               lambda bi, i: (bi, jnp.minimum((i + 1) * tm_halo, n_halo - 1), 0))]
                 + [full(a) for a in consts],
        out_specs=[out_spec] * 11,
        out_shape=[jax.ShapeDtypeStruct((b, t, rw_dim), dt) for dt in out_dtypes],
        compiler_params=_params("parallel", "parallel"),
        name="rwkv_prep",
    )(p, p, p, *consts)


def _rwkv_chunk(kkd, rd, binv, kinv, bfin, kfin, v, s, wtot, incl2, strict2):
    c = v.shape[1]
    lanes = v.shape[2]
    la = lax.broadcasted_iota(jnp.int32, (1, 1, lanes), 2) < HEAD

    def only_a(z):
        return jnp.where(la, z, 0.0)

    def only_b(z):
        return jnp.where(la, 0.0, z)

    x = jnp.concatenate([kkd, rd], axis=1)
    gh_a = _bmm_nt(only_a(x), jnp.concatenate([binv, kinv, s], axis=1))
    gh_b = _bmm_nt(only_b(x), jnp.concatenate([kinv, binv, s], axis=1))
    a2_a = jnp.where(strict2, gh_a[:, :c, :2 * c], 0.0)
    a2_b = jnp.where(strict2, gh_b[:, :c, :2 * c], 0.0)
    v_a = only_a(v)
    v_b = only_b(v)
    akk = jnp.where(la, a2_b, a2_a)
    u = -(gh_a[:, :c, 2 * c:] + gh_b[:, :c, 2 * c:] + _bmm(akk, jnp.concatenate([v_b, v_a], axis=1)))
    nm = -jnp.where(la, a2_a, a2_b)
    for _ in range(c.bit_length() - 2):
        top = jnp.concatenate([only_a(u), only_a(nm)], axis=2)
        bot = jnp.concatenate([only_b(u), only_b(nm)], axis=2)
        w = _bmm(nm, jnp.concatenate([top, bot], axis=1))
        u = u + w[:, :, :lanes]
        nm = w[:, :, lanes:]
    u = u + _bmm(nm, jnp.concatenate([only_a(u), only_b(u)], axis=1))
    bmat = jnp.concatenate([jnp.where(incl2, gh_a[:, c:, :2 * c], 0.0),
                            jnp.where(incl2, gh_b[:, c:, :2 * c], 0.0)], axis=2)
    o = (gh_a[:, c:, 2 * c:] + gh_b[:, c:, 2 * c:]
         + _bmm(bmat, jnp.concatenate([only_a(u), v_a, v_b, only_b(u)], axis=1)))
    row = lax.broadcasted_iota(jnp.int32, s.shape, 1) // HEAD
    col = lax.broadcasted_iota(jnp.int32, s.shape, 2) // HEAD
    upd = _bmm_tn(jnp.concatenate([u, v], axis=1), jnp.concatenate([bfin, kfin], axis=1))
    s_new = s * wtot + jnp.where(row == col, upd, 0.0)
    return o, s_new


def _rwkv_decay_terms(r, v, kk, lw, kd, bm, reverse):
    c = r.shape[0]
    r, v, kk, kd, bm = (z.astype(F32) for z in (r, v, kk, kd, bm))
    incl, _ = _order_masks(c, reverse)
    cum = _dot_sel_l(incl.astype(BF16), lw)
    last = 0 if reverse else c - 1
    wtot = jnp.exp2(cum[last:last + 1, :])
    e_pos = jnp.exp2(cum)
    e_neg = 1.0 / e_pos
    e_fin = e_neg * wtot
    return (kk * jnp.exp2(cum - lw), r * e_pos, bm * e_neg, kd * e_neg, bm * e_fin,
            kd * e_fin, v, wtot)


def _rwkv_scan_kernel(rf, vf, kf, lwf, kdf, bf, rb, vb, kb, lwb, kdb, bb, s0_ref,
                      of_ref, ob_ref, sfin_ref, s_scr, *, heads):
    i = pl.program_id(1)

    nb = rf.shape[0]
    c = rf.shape[1]
    pair = 2 * HEAD

    @pl.when(i == 0)
    def _():
        s_scr[...] = s0_ref[...].reshape(s_scr.shape)

    pairs = heads // 2
    terms = [tm for bi in range(nb)
             for tm in (_rwkv_decay_terms(rf[bi], vf[bi], kf[bi], lwf[bi], kdf[bi], bf[bi], False),
                        _rwkv_decay_terms(rb[bi], vb[bi], kb[bi], lwb[bi], kdb[bi], bb[bi], True))]
    ops = [jnp.stack([z[:, p * pair:(p + 1) * pair] for z in zs for p in range(pairs)])
           for zs in zip(*terms)]
    n = nb * 2 * pairs
    t = lax.broadcasted_iota(jnp.int32, (n, c, 2 * c), 1)
    j = lax.broadcasted_iota(jnp.int32, (n, c, 2 * c), 2) % c
    rev = lax.broadcasted_iota(jnp.int32, (n, c, 2 * c), 0) // pairs % 2 == 1
    ahead = jnp.where(rev, t - j, j - t)
    o, s_new = _rwkv_chunk(*ops[:7], s_scr[...], ops[7], ahead <= 0, ahead < 0)
    s_scr[...] = s_new
    for bi in range(nb):
        base = bi * 2 * pairs
        of_ref[bi] = jnp.concatenate([o[base + p] for p in range(pairs)], axis=1).astype(of_ref.dtype)
        ob_ref[bi] = jnp.concatenate([o[base + pairs + p] for p in range(pairs)],
                                     axis=1).astype(ob_ref.dtype)

    @pl.when(i == pl.num_programs(1) - 1)
    def _():
        sfin_ref[...] = s_scr[...].reshape(sfin_ref.shape)


def _rwkv_scan(prep, s0):
    r, v, kk, lw0, lw1, kd0, kd1, b0, b1 = prep
    b, t, rw_dim = r.shape
    heads = rw_dim // HEAD
    n = t // CHUNK
    nb = SCAN_BATCH if b % SCAN_BATCH == 0 else 1
    fwd = pl.BlockSpec((nb, CHUNK, rw_dim), lambda bi, i: (bi, i, 0))
    bwd = pl.BlockSpec((nb, CHUNK, rw_dim), lambda bi, i: (bi, n - 1 - i, 0))
    st = pl.BlockSpec((nb,) + s0.shape[1:], lambda bi, i: (bi, 0, 0, 0))
    o_sds = jax.ShapeDtypeStruct((b, t, rw_dim), BF16)
    return pl.pallas_call(
        functools.partial(_rwkv_scan_kernel, heads=heads),
        grid=(b // nb, n),
        in_specs=[fwd] * 6 + [bwd] * 6 + [st],
        out_specs=[fwd, bwd, st],
        out_shape=[o_sds, o_sds, jax.ShapeDtypeStruct(s0.shape, F32)],
        scratch_shapes=[pltpu.VMEM((nb * s0.shape[1],) + s0.shape[2:], F32)],
        compiler_params=_params("parallel", "arbitrary"),
        name="rwkv_scan",
    )(r, v, kk, lw0, kd0, b0, r, v, kk, lw1, kd1, b1, s0)


def _softmax_pv(s_list, v_list):
    m = s_list[0].max(axis=-1, keepdims=True)
    for s in s_list[1:]:
        m = jnp.maximum(m, s.max(axis=-1, keepdims=True))
    num = None
    den = None
    for s, v in zip(s_list, v_list):
        p = jnp.exp(s - m)
        d = p.sum(axis=-1, keepdims=True)
        o = _dot(p.astype(BF16), v)
        num = o if num is None else num + o
        den = d if den is None else den + d
    return num / den


def _natten_kernel(q_ref, k_ref, v_ref, kc_ref, vc_ref, bias_ref, o_ref, *, rows, scale, rb):
    kc = kc_ref[0]
    vc = vc_ref[0]
    l = kc.shape[0]
    nwin = NA_ROWS * GRID_W
    head0 = lax.broadcasted_iota(jnp.int32, (1, 2 * HEAD), 1) < HEAD

    def body(it, carry):
        base = it * rb
        start = pl.multiple_of(base * GRID_W, rb * GRID_W)
        q = q_ref[0, pl.ds(start, rb * GRID_W), :] * scale
        zero = jnp.zeros_like(q)
        q0 = jnp.where(head0, q, zero)
        q1 = jnp.where(head0, zero, q)
        qs, kws, vws, bs = [], [], [], []
        for j in range(rb):
            r = base + j
            r0 = jnp.clip(r - NA_ROWS // 2, 0, rows - NA_ROWS)
            rows_j = slice(j * GRID_W, (j + 1) * GRID_W)
            qs.append(jnp.concatenate([q0[rows_j], q1[rows_j]], axis=0))
            win = pl.ds(pl.multiple_of(r0 * GRID_W, GRID_W), nwin)
            kws.append(k_ref[0, win, :])
            vws.append(v_ref[0, win, :])
            bs.append(bias_ref[0, r - r0])
        qs = jnp.stack(qs)
        s_win = _bmm_nt(qs, jnp.stack(kws)) + jnp.stack(bs)
        s_ctx = _dot_nt(qs.reshape(rb * 2 * GRID_W, 2 * HEAD), kc).reshape(rb, 2 * GRID_W, l)
        m = jnp.maximum(s_win.max(axis=-1, keepdims=True), s_ctx.max(axis=-1, keepdims=True))
        p_win = jnp.exp(s_win - m)
        p_ctx = jnp.exp(s_ctx - m)
        den = p_win.sum(axis=-1, keepdims=True) + p_ctx.sum(axis=-1, keepdims=True)
        o = _bmm(p_win, jnp.stack(vws))
        o = o + _dot(p_ctx.reshape(rb * 2 * GRID_W, l).astype(BF16), vc).reshape(rb, 2 * GRID_W, 2 * HEAD)
        o = o / den
        out = jnp.concatenate([jnp.where(head0, o[j, :GRID_W], o[j, GRID_W:]) for j in range(rb)], axis=0)
        o_ref[0, pl.ds(start, rb * GRID_W), :] = out.astype(o_ref.dtype)
        return carry

    lax.fori_loop(0, rows // rb, body, 0)


def _natten(p_lat, p_ctx, bias, rw_cols, na_dim):
    b, t, _ = p_lat.shape
    l = p_ctx.shape[1]
    pairs = na_dim // (2 * HEAD)
    qb = rw_cols // (2 * HEAD)
    kb = qb + pairs
    vb = kb + pairs
    rows = t // GRID_W

    def lat(off):
        return pl.BlockSpec((1, t, 2 * HEAD), lambda bi, hp: (bi, 0, off + hp))

    def ctx(off):
        return pl.BlockSpec((1, l, 2 * HEAD), lambda bi, hp: (bi, 0, off + hp))

    rb = 2 * NA_ROWS if rows % (2 * NA_ROWS) == 0 else 1
    return pl.pallas_call(
        functools.partial(_natten_kernel, rows=rows, scale=HEAD ** -0.5, rb=rb),
        grid=(b, pairs),
        in_specs=[lat(qb), lat(kb), lat(vb), ctx(kb), ctx(vb),
                  pl.BlockSpec((1, NA_ROWS, 2 * GRID_W, NA_ROWS * GRID_W), lambda bi, hp: (hp, 0, 0, 0))],
        out_specs=pl.BlockSpec((1, t, 2 * HEAD), lambda bi, hp: (bi, 0, hp)),
        out_shape=jax.ShapeDtypeStruct((b, t, na_dim), BF16),
        compiler_params=_params("parallel", "parallel"),
        name="natten",
    )(p_lat, p_lat, p_lat, p_ctx, p_ctx, bias)


def _ctx_attn_kernel(q_ref, k_ref, v_ref, o_ref, *, scale):
    q = q_ref[0] * scale
    k = k_ref[0]
    v = v_ref[0]
    outs = []
    for hh in range(2):
        sl = slice(hh * HEAD, (hh + 1) * HEAD)
        outs.append(_softmax_pv([_dot_nt(q[:, sl], k[:, sl])], [v[:, sl]]))
    o_ref[0] = jnp.concatenate(outs, axis=1).astype(o_ref.dtype)


def _ctx_attn(p_ctx, rw_cols, na_dim):
    b, l, _ = p_ctx.shape
    pairs = na_dim // (2 * HEAD)
    qb = rw_cols // (2 * HEAD)

    def blk(off):
        return pl.BlockSpec((1, l, 2 * HEAD), lambda bi, hp: (bi, 0, off + hp))

    return pl.pallas_call(
        functools.partial(_ctx_attn_kernel, scale=HEAD ** -0.5),
        grid=(b, pairs),
        in_specs=[blk(qb), blk(qb + pairs), blk(qb + 2 * pairs)],
        out_specs=pl.BlockSpec((1, l, 2 * HEAD), lambda bi, hp: (bi, 0, hp)),
        out_shape=jax.ShapeDtypeStruct((b, l, na_dim), BF16),
        compiler_params=_params("parallel", "parallel"),
        name="ctx_attn",
    )(p_ctx, p_ctx, p_ctx)


def _natten_bias(rpb):
    cols = np.arange(GRID_W)
    c0 = np.clip(cols - NA_COLS // 2, 0, GRID_W - NA_COLS)
    valid = (cols[None, :] >= c0[:, None]) & (cols[None, :] < c0[:, None] + NA_COLS)
    col_rel = cols[None, :] - cols[:, None] + NA_COLS - 1
    onehot = (col_rel[:, :, None] == np.arange(2 * NA_COLS - 1)).astype(np.float32)
    toe = jnp.einsum("hrj,qkj->hrqk", rpb.astype(F32), onehot, precision=HIGHEST)
    toe = jnp.where(valid[None, None], toe, MASK_BIAS)
    tab = jnp.stack([toe[:, NA_ROWS - 1 - d:2 * NA_ROWS - 1 - d] for d in range(NA_ROWS)], axis=1)
    tab = tab.transpose(0, 1, 3, 2, 4)
    heads = rpb.shape[0]
    tab = tab.reshape(heads // 2, 2, NA_ROWS, GRID_W, NA_ROWS * GRID_W)
    return tab.transpose(0, 2, 1, 3, 4).reshape(heads // 2, NA_ROWS, 2 * GRID_W, NA_ROWS * GRID_W)


def _even_out_kernel(h_ref, of_ref, ob_ref, bonus_ref, g_ref, na_ref, w_ref, gate_ref,
                     lng_ref, lnb_ref, bd_ref, o_ref):
    bd = bd_ref[...]
    o = of_ref[0].astype(F32) + ob_ref[0].astype(F32)
    inv = 1.0 / HEAD
    mu = _dot_sel_r(o, bd) * inv
    xc = o - mu
    var = _dot_sel_r(xc * xc, bd) * inv
    y = xc * lax.rsqrt(var + RW_GN_EPS) * lng_ref[...] + lnb_ref[...]
    rw = (y + bonus_ref[0].astype(F32)) * g_ref[0].astype(F32)
    cat = jnp.concatenate([rw.astype(BF16), na_ref[0]], axis=1)
    o_ref[0] = h_ref[0] + gate_ref[0] * _dot(cat, w_ref[...])


def _even_out(h, o_f, o_b, bonus, g, na, w_out, gate, ln_g, ln_b, bd, tm):
    b, t, d = h.shape
    rw_dim = o_f.shape[2]
    na_dim = na.shape[2]
    big = pl.BlockSpec((1, tm, d), lambda bi, i: (bi, i, 0))
    half = pl.BlockSpec((1, tm, rw_dim), lambda bi, i: (bi, i, 0))
    par = pl.BlockSpec((1, rw_dim), lambda bi, i: (0, 0))
    return pl.pallas_call(
        _even_out_kernel,
        grid=(b, t // tm),
        in_specs=[big, half, half, half, half,
                  pl.BlockSpec((1, tm, na_dim), lambda bi, i: (bi, i, 0)),
                  pl.BlockSpec(w_out.shape, lambda bi, i: (0, 0)),
                  pl.BlockSpec((1, 1, d), lambda bi, i: (bi, 0, 0)),
                  par, par,
                  pl.BlockSpec(bd.shape, lambda bi, i: (0, 0))],
        out_specs=big,
        out_shape=jax.ShapeDtypeStruct((b, t, d), F32),
        compiler_params=_params("parallel", "parallel"),
        name="even_out",
    )(h, o_f, o_b, bonus, g, na, w_out, gate, ln_g.reshape(1, rw_dim), ln_b.reshape(1, rw_dim), bd)


def _odd_out_kernel(h_ref, of_ref, ob_ref, gate_in_ref, w_ref, gate_ref, ng_ref, o_ref):
    o = of_ref[0].astype(F32) + ob_ref[0].astype(F32)
    y = _rms(o, ng_ref[...]) * _silu(gate_in_ref[0].astype(F32))
    o_ref[0] = h_ref[0] + gate_ref[0] * _dot(y.astype(BF16), w_ref[...])


def _odd_out(h, o_f, o_b, p, gate_block, w_out, gate, norm_g, tm):
    b, t, d = h.shape
    vd = o_f.shape[2]
    big = pl.BlockSpec((1, tm, d), lambda bi, i: (bi, i, 0))
    val = pl.BlockSpec((1, tm, vd), lambda bi, i: (bi, i, 0))
    return pl.pallas_call(
        _odd_out_kernel,
        grid=(b, t // tm),
        in_specs=[big, val, val,
                  pl.BlockSpec((1, tm, vd), lambda bi, i: (bi, i, gate_block)),
                  pl.BlockSpec(w_out.shape, lambda bi, i: (0, 0)),
                  pl.BlockSpec((1, 1, d), lambda bi, i: (bi, 0, 0)),
                  pl.BlockSpec((1, vd), lambda bi, i: (0, 0))],
        out_specs=big,
        out_shape=jax.ShapeDtypeStruct((b, t, d), F32),
        compiler_params=_params("parallel", "parallel"),
        name="odd_out",
    )(h, o_f, o_b, p, w_out, gate, norm_g.reshape(1, vd))


def _hgrn_terms(qp, fp, ip, lb, reverse):
    c = qp.shape[0]
    incl, _ = _order_masks(c, reverse)
    q = qp.astype(F32)
    forget = lb + (1.0 - lb) * _sigmoid(fp.astype(F32))
    k = 1.0 - forget
    cum = _dot_sel_l(incl.astype(BF16), jnp.log2(forget))
    last = 0 if reverse else c - 1
    tot = cum[last:last + 1, :]
    mid = cum[c // 2:c // 2 + 1, :]
    e_mid = jnp.exp2(cum - mid)
    q_mid = q * e_mid
    k_mid = k / e_mid
    return q_mid, k_mid, q_mid * jnp.exp2(mid), k_mid * jnp.exp2(tot - mid), ip, jnp.exp2(tot)


def _hgrn_scan_kernel(qf, ff, vf, qb, fb, vb, hl_ref, s0_ref, of_ref, ob_ref, sfin_ref, s_scr,
                      *, layer, heads):
    i = pl.program_id(1)

    @pl.when(i == 0)
    def _():
        s_scr[...] = s0_ref[...].reshape(s_scr.shape)

    hl = hl_ref[...]
    e = jnp.exp(hl - hl.max(axis=0, keepdims=True))
    sm = e / e.sum(axis=0, keepdims=True)
    lb = jnp.zeros_like(sm[0:1])
    for j in range(1, layer + 1):
        lb = lb + sm[j:j + 1]

    nb = qf.shape[0]
    c = qf.shape[1]
    n = nb * 2 * heads
    terms = [tm for bi in range(nb) for tm in (_hgrn_terms(qf[bi], ff[bi], vf[bi], lb, False),
                                               _hgrn_terms(qb[bi], fb[bi], vb[bi], lb, True))]
    q_mid, k_mid, q_cum, k_fin, v, wtot = [
        jnp.stack([z[:, h * HG_KEY:(h + 1) * HG_KEY] for z in zs for h in range(heads)])
        for zs in zip(*terms)]
    t = lax.broadcasted_iota(jnp.int32, (n, c, c), 1)
    j = lax.broadcasted_iota(jnp.int32, (n, c, c), 2)
    rev = lax.broadcasted_iota(jnp.int32, (n, c, c), 0) // heads % 2 == 1
    incl = jnp.where(rev, t - j, j - t) <= 0
    s = s_scr[...]
    att = jnp.where(incl, _bmm_nt(q_mid, k_mid), 0.0)
    o = _bmm(att, v) + _bmm_nt(q_cum, s)
    s_scr[...] = s * wtot + _bmm_tn(v, k_fin)
    for bi in range(nb):
        base = bi * 2 * heads
        of_ref[bi] = jnp.concatenate([o[base + h] for h in range(heads)], axis=1).astype(of_ref.dtype)
        ob_ref[bi] = jnp.concatenate([o[base + heads + h] for h in range(heads)],
                                     axis=1).astype(ob_ref.dtype)

    @pl.when(i == pl.num_programs(1) - 1)
    def _():
        sfin_ref[...] = s_scr[...].reshape(sfin_ref.shape)


def _hgrn_scan(p, hg_lower, s0, layer, heads):
    b, t, _ = p.shape
    n = t // CHUNK
    width = heads * HG_KEY
    depth = hg_lower.shape[0]

    nb = SCAN_BATCH if b % SCAN_BATCH == 0 else 1

    def fwd(blk):
        return pl.BlockSpec((nb, CHUNK, width), lambda bi, i: (bi, i, blk))

    def bwd(blk):
        return pl.BlockSpec((nb, CHUNK, width), lambda bi, i: (bi, n - 1 - i, blk))

    st = pl.BlockSpec((nb, 2 * heads, HG_KEY, HG_KEY), lambda bi, i: (bi, 0, 0, 0))
    o_sds = jax.ShapeDtypeStruct((b, t, width), BF16)
    return pl.pallas_call(
        functools.partial(_hgrn_scan_kernel, layer=layer, heads=heads),
        grid=(b // nb, n),
        in_specs=[fwd(0), fwd(1), fwd(3), bwd(0), bwd(2), bwd(3),
                  pl.BlockSpec((depth, width), lambda bi, i: (0, 0)), st],
        out_specs=[fwd(0), bwd(0), st],
        out_shape=[o_sds, o_sds, jax.ShapeDtypeStruct((b, 2 * heads, HG_KEY, HG_KEY), F32)],
        scratch_shapes=[pltpu.VMEM((nb * 2 * heads, HG_KEY, HG_KEY), F32)],
        compiler_params=_params("parallel", "arbitrary"),
        name="hgrn_scan",
    )(p, p, p, p, p, p, hg_lower, s0)


def _row_tile(t, target):
    return target if t % target == 0 else t


def kernel(x, c, ctx, c_ctx, norm_mix_g, norm_mlp_g, ada_w, ada_b, mlp_w1, mlp_w2, ev_w_in,
           ev_w_out, rw_mu_prev, rw_mu_next, rw_w0, rw_w_up, rw_a0, rw_a_up, rw_g_up, rw_k_k,
           rw_k_a, rw_r_k, rw_ln_g, rw_ln_b, na_rpb, od_w_in, od_w_out, hg_lower, hg_norm_g,
           final_norm_g):
    b, t, d = x.shape
    l = ctx.shape[1]
    depth = ada_w.shape[0]
    rw_dim = rw_k_k.shape[1]
    rw_heads = rw_dim // HEAD
    rw_cols = rw_mu_prev.shape[1]
    na_dim = d - rw_dim
    hg_heads = d // HG_KEY

    pad_rows = -(b + 1) % SUBLANES
    c_rows = jnp.concatenate([c, c_ctx[None], jnp.zeros((pad_rows, d), F32)], axis=0)
    mods = _ada(c_rows, ada_w, ada_b)

    def mod_vecs(layer):
        m = mods[layer]
        lat = [m[:b, j * d:(j + 1) * d].reshape(b, 1, d) for j in range(6)]
        cx = [m[b, j * d:(j + 1) * d].reshape(1, 1, d) for j in range(6)]
        return lat, cx

    eye = jnp.arange(rw_dim) // HEAD
    bd = (eye[:, None] == eye[None, :]).astype(BF16)

    def flat(z):
        return z.reshape(1, b * l, z.shape[-1])

    def per_sample(z):
        return z.reshape(b, l, z.shape[-1])

    tm_lat = _row_tile(t, 512)
    tm_out = _row_tile(t, 1024)
    tm_ctx = _row_tile(b * l, 512)
    tm_mlp_ctx = _row_tile(b * l, 1024)
    h_lat, h_ctx = x, flat(ctx)
    for layer in range(depth):
        last = layer == depth - 1
        (sh1, s1, g1, sh2, s2, g2), (csh1, cs1, cg1, csh2, cs2, cg2) = mod_vecs(layer)
        if layer % 2 == 0:
            e = layer // 2
            w_in = ev_w_in[e].astype(BF16)
            p_lat = _norm_mod_matmul(h_lat, norm_mix_g[layer], sh1, s1, w_in, tm_lat)
            p_ctx = per_sample(_norm_mod_matmul(h_ctx, norm_mix_g[layer], csh1, cs1, w_in, tm_ctx))
            prm = dict(
                mu_prev=rw_mu_prev[e][None], mu_next=rw_mu_next[e][None],
                w0=rw_w0[e].reshape(1, 2 * rw_dim),
                w_up=jnp.concatenate([rw_w_up[e, 0], rw_w_up[e, 1]], axis=1),
                a0=rw_a0[e].reshape(1, 2 * rw_dim),
                a_up=jnp.concatenate([rw_a_up[e, 0], rw_a_up[e, 1]], axis=1),
                g_up=rw_g_up[e], k_k=rw_k_k[e][None], k_a=rw_k_a[e][None],
                r_k=rw_r_k[e].reshape(1, rw_dim), bd=bd)
            s_zero = jnp.zeros((b, rw_heads, 2 * HEAD, 2 * HEAD), F32)
            prep_c = _rwkv_prep(p_ctx, prm, _row_tile(l, 256))
            oc_f, oc_b, s_ctx = _rwkv_scan(prep_c[:9], s_zero)
            prep_l = _rwkv_prep(p_lat, prm, tm_lat)
            ol_f, ol_b, _ = _rwkv_scan(prep_l[:9], s_ctx)
            na_lat = _natten(p_lat, p_ctx, _natten_bias(na_rpb[e]), rw_cols, na_dim)
            w_out = ev_w_out[e].astype(BF16)
            h_lat = _even_out(h_lat, ol_f, ol_b, prep_l[10], prep_l[9], na_lat, w_out, g1,
                              rw_ln_g[e], rw_ln_b[e], bd, tm_out)
            if not last:
                na_ctx = _ctx_attn(p_ctx, rw_cols, na_dim)
                h_ctx = _even_out(h_ctx, flat(oc_f), flat(oc_b), flat(prep_c[10]), flat(prep_c[9]),
                                  flat(na_ctx), w_out, cg1, rw_ln_g[e], rw_ln_b[e], bd, tm_ctx)
        else:
            o = layer // 2
            w_in = od_w_in[o].astype(BF16)
            hg_kdim = hg_heads * HG_KEY
            p_lat = _norm_mod_matmul(h_lat, norm_mix_g[layer], sh1, s1, w_in, tm_lat, hg_kdim)
            p_ctx = _norm_mod_matmul(h_ctx, norm_mix_g[layer], csh1, cs1, w_in, tm_ctx, hg_kdim)
            s_zero = jnp.zeros((b, 2 * hg_heads, HG_KEY, HG_KEY), F32)
            oc_f, oc_b, s_ctx = _hgrn_scan(per_sample(p_ctx), hg_lower, s_zero, layer, hg_heads)
            ol_f, ol_b, _ = _hgrn_scan(p_lat, hg_lower, s_ctx, layer, hg_heads)
            w_out = od_w_out[o].astype(BF16)
            gate_block = p_lat.shape[2] // d - 1
            h_lat = _odd_out(h_lat, ol_f, ol_b, p_lat, gate_block, w_out, g1, hg_norm_g[o], tm_out)
            if not last:
                h_ctx = _odd_out(h_ctx, flat(oc_f), flat(oc_b), p_ctx, gate_block, w_out, cg1,
                                 hg_norm_g[o], tm_ctx)
        w1 = mlp_w1[layer].astype(BF16)
        w2 = mlp_w2[layer].astype(BF16)
        h_lat = _mlp(h_lat, norm_mlp_g[layer], sh2, s2, g2, w1, w2, final_norm_g,
                     _row_tile(t, 1024), 1024, last)
        if not last:
            h_ctx = _mlp(h_ctx, norm_mlp_g[layer], csh2, cs2, cg2, w1, w2, final_norm_g,
                         tm_mlp_ctx, 1024, False)
    return h_lat
```

```python
import functools
import math

import jax
import jax.numpy as jnp
import numpy as np
from jax import lax
from jax.experimental import pallas as pl
from jax.experimental.pallas import tpu as pltpu

F32 = jnp.float32
BF16 = jnp.bfloat16
HIGHEST = lax.Precision.HIGHEST

NORM_EPS = 1e-6
RW_GN_EPS = 64e-5
HEAD = 64
GRID_W = 64
NA_ROWS = 8
NA_COLS = 16
HG_KEY = 128
CHUNK = 64
SOLVE_BASE = 8
MLP_ROW_CHUNK = 256
SCAN_BATCH = 4
MASK_BIAS = -1e30
DECAY_SCALE_LOG2 = math.exp(-0.5) / math.log(2.0)
SUBLANES = 8
HALO_ROWS = 16
VMEM_LIMIT = 52 * 1024 * 1024


def _params(*sem):
    return pltpu.CompilerParams(dimension_semantics=sem, vmem_limit_bytes=VMEM_LIMIT)


def _dot(a, b):
    return jnp.dot(a, b, preferred_element_type=F32)


def _dot_nt(a, b):
    return lax.dot_general(a, b, (((1,), (1,)), ((), ())), preferred_element_type=F32)


def _dot_tn(a, b):
    return lax.dot_general(a, b, (((0,), (0,)), ((), ())), preferred_element_type=F32)


def _mm(a, b):
    return _dot(a.astype(BF16), b.astype(BF16))


def _mm_nt(a, b):
    return _dot_nt(a.astype(BF16), b.astype(BF16))


def _mm_tn(a, b):
    return _dot_tn(a.astype(BF16), b.astype(BF16))


def _bmm(a, b):
    return lax.dot_general(a.astype(BF16), b.astype(BF16), (((2,), (1,)), ((0,), (0,))),
                           preferred_element_type=F32)


def _bmm_nt(a, b):
    return lax.dot_general(a.astype(BF16), b.astype(BF16), (((2,), (2,)), ((0,), (0,))),
                           preferred_element_type=F32)


def _bmm_tn(a, b):
    return lax.dot_general(a.astype(BF16), b.astype(BF16), (((1,), (1,)), ((0,), (0,))),
                           preferred_element_type=F32)


def _hi_lo(x):
    hi = x.astype(BF16)
    return hi, (x - hi.astype(F32)).astype(BF16)


def _dot_sel_r(x, sel):
    return _dot(x.astype(BF16), sel)


def _dot_sel_l(sel, x):
    hi, lo = _hi_lo(x)
    return _dot(sel, hi) + _dot(sel, lo)


def _sigmoid(x):
    return 1.0 / (1.0 + jnp.exp(-x))


def _silu(x):
    return x * _sigmoid(x)


def _rms(x, g):
    return x * lax.rsqrt(jnp.mean(x * x, axis=-1, keepdims=True) + NORM_EPS) * g


def _order_masks(n, reverse):
    t = lax.broadcasted_iota(jnp.int32, (n, n), 0)
    i = lax.broadcasted_iota(jnp.int32, (n, n), 1)
    if reverse:
        return i >= t, i > t
    return i <= t, i < t


def _ada_kernel(c_ref, w_ref, b_ref, o_ref):
    s_hi, s_lo = _hi_lo(_silu(c_ref[...]))
    w_hi, w_lo = _hi_lo(w_ref[0])
    o_ref[0] = _dot(s_hi, w_hi) + (_dot(s_hi, w_lo) + _dot(s_lo, w_hi)) + b_ref[0]


def _ada(c_rows, ada_w, ada_b):
    depth, d, n = ada_w.shape
    rows = c_rows.shape[0]
    tn = 512
    return pl.pallas_call(
        _ada_kernel,
        grid=(depth, n // tn),
        in_specs=[pl.BlockSpec((rows, d), lambda l, j: (0, 0)),
                  pl.BlockSpec((1, d, tn), lambda l, j: (l, 0, j)),
                  pl.BlockSpec((1, 1, tn), lambda l, j: (l, 0, j))],
        out_specs=pl.BlockSpec((1, rows, tn), lambda l, j: (l, 0, j)),
        out_shape=jax.ShapeDtypeStruct((depth, rows, n), F32),
        compiler_params=_params("parallel", "parallel"),
        name="ada",
    )(c_rows, ada_w, ada_b.reshape(depth, 1, n))


def _nmm_kernel(x_ref, g_ref, sh_ref, sc_ref, w_ref, o_ref, *, silu_cols):
    tm = x_ref.shape[1]
    rc = MLP_ROW_CHUNK if tm % MLP_ROW_CHUNK == 0 else tm
    for c in range(tm // rc):
        rows = pl.ds(c * rc, rc)
        a = (_rms(x_ref[0, rows, :], g_ref[...]) * (1.0 + sc_ref[0]) + sh_ref[0]).astype(BF16)
        if silu_cols:
            o_ref[0, rows, :silu_cols] = _silu(_dot(a, w_ref[:, :silu_cols])).astype(o_ref.dtype)
            o_ref[0, rows, silu_cols:] = _dot(a, w_ref[:, silu_cols:]).astype(o_ref.dtype)
        else:
            o_ref[0, rows, :] = _dot(a, w_ref[...]).astype(o_ref.dtype)


def _norm_mod_matmul(x, g, shift, scale, w, tm, silu_cols=0):
    b, t, d = x.shape
    n = w.shape[1]
    return pl.pallas_call(
        functools.partial(_nmm_kernel, silu_cols=silu_cols),
        grid=(b, t // tm),
        in_specs=[pl.BlockSpec((1, tm, d), lambda bi, i: (bi, i, 0)),
                  pl.BlockSpec((1, d), lambda bi, i: (0, 0)),
                  pl.BlockSpec((1, 1, d), lambda bi, i: (bi, 0, 0)),
                  pl.BlockSpec((1, 1, d), lambda bi, i: (bi, 0, 0)),
                  pl.BlockSpec((d, n), lambda bi, i: (0, 0))],
        out_specs=pl.BlockSpec((1, tm, n), lambda bi, i: (bi, i, 0)),
        out_shape=jax.ShapeDtypeStruct((b, t, n), BF16),
        compiler_params=_params("parallel", "parallel"),
        name="norm_mod_matmul",
    )(x, g.reshape(1, d), shift, scale, w)


def _mlp_kernel(h_ref, g_ref, sh_ref, sc_ref, gate_ref, w1_ref, w2_ref, fg_ref, o_ref,
                a_scr, acc_scr, *, final_norm):
    j = pl.program_id(2)
    last = pl.num_programs(2) - 1
    tm = a_scr.shape[0]
    rc = MLP_ROW_CHUNK if tm % MLP_ROW_CHUNK == 0 else tm

    def hidden(a):
        hid = jnp.maximum(_dot(a, w1_ref[...]), 0.0)
        return _dot((hid * hid).astype(BF16), w2_ref[...])

    @pl.when(j == 0)
    def _():
        for c in range(tm // rc):
            rows = pl.ds(c * rc, rc)
            a = _rms(h_ref[0, rows, :], g_ref[...]) * (1.0 + sc_ref[0]) + sh_ref[0]
            a = a.astype(BF16)
            a_scr[rows, :] = a
            acc_scr[rows, :] = hidden(a)

    @pl.when(jnp.logical_and(j > 0, j < last))
    def _():
        acc_scr[...] += hidden(a_scr[...])

    @pl.when(j == last)
    def _():
        for c in range(tm // rc):
            rows = pl.ds(c * rc, rc)
            out = h_ref[0, rows, :] + gate_ref[0] * (acc_scr[rows, :] + hidden(a_scr[rows, :]))
            if final_norm:
                out = _rms(out, fg_ref[...])
            o_ref[0, rows, :] = out


def _mlp(h, g, shift, scale, gate, w1, w2, final_g, tm, th, final_norm):
    b, t, d = h.shape
    hid = w1.shape[1]
    assert hid // th >= 2
    vec = pl.BlockSpec((1, 1, d), lambda bi, i, j: (bi, 0, 0))
    par = pl.BlockSpec((1, d), lambda bi, i, j: (0, 0))
    return pl.pallas_call(
        functools.partial(_mlp_kernel, final_norm=final_norm),
        grid=(b, t // tm, hid // th),
        in_specs=[pl.BlockSpec((1, tm, d), lambda bi, i, j: (bi, i, 0)),
                  par, vec, vec, vec,
                  pl.BlockSpec((d, th), lambda bi, i, j: (0, j)),
                  pl.BlockSpec((th, d), lambda bi, i, j: (j, 0)),
                  par],
        out_specs=pl.BlockSpec((1, tm, d), lambda bi, i, j: (bi, i, 0)),
        out_shape=jax.ShapeDtypeStruct((b, t, d), F32),
        scratch_shapes=[pltpu.VMEM((tm, d), BF16), pltpu.VMEM((tm, d), F32)],
        compiler_params=_params("parallel", "parallel", "arbitrary"),
        name="mlp",
    )(h, g.reshape(1, d), shift, scale, gate, w1, w2, final_g.reshape(1, d))


def _rwkv_prep_kernel(p_ref, pp_ref, pn_ref, mup_ref, mun_ref, w0_ref, wup_ref, a0_ref, aup_ref,
                      gup_ref, kk_ref, ka_ref, rk_ref, bd_ref,
                      r_o, v_o, kkn_o, lw0_o, lw1_o, kd0_o, kd1_o, b0_o, b1_o, g_o, bonus_o,
                      *, rw_dim, w_lora, a_lora):
    i = pl.program_id(1)
    n = pl.num_programs(1)
    p = p_ref[0].astype(F32)
    tm = p.shape[0]
    row = lax.broadcasted_iota(jnp.int32, (SUBLANES, 1), 0)
    first = jnp.where(i > 0, pp_ref[0].astype(F32)[HALO_ROWS - 1:HALO_ROWS, :], 0.0)
    last = jnp.where(i < n - 1, pn_ref[0].astype(F32)[0:1, :], 0.0)
    prev = pltpu.roll(p, 1, axis=0)
    prev = jnp.concatenate([jnp.where(row == 0, first, prev[:SUBLANES]), prev[SUBLANES:]], axis=0)
    nxt = pltpu.roll(p, tm - 1, axis=0)
    nxt = jnp.concatenate([nxt[:tm - SUBLANES],
                           jnp.where(row == SUBLANES - 1, last, nxt[tm - SUBLANES:])], axis=0)
    ps = p + (prev - p) * mup_ref[...] + (nxt - p) * mun_ref[...]

    r = ps[:, :rw_dim]
    k = ps[:, rw_dim:2 * rw_dim]
    v = ps[:, 2 * rw_dim:3 * rw_dim]
    off = 3 * rw_dim
    w_lo = ps[:, off:off + w_lora]
    a_lo = ps[:, off + w_lora:off + w_lora + a_lora]
    g_lo = ps[:, off + w_lora + a_lora:]

    w_pre = w0_ref[...] + _mm(jnp.tanh(w_lo), wup_ref[...])
    log_decay = -DECAY_SCALE_LOG2 * _sigmoid(w_pre)
    a = _sigmoid(a0_ref[...] + _mm(a_lo, aup_ref[...]))
    g = _mm(_sigmoid(g_lo), gup_ref[...])

    bd = bd_ref[...]
    kk = k * kk_ref[...]
    kk = kk / jnp.maximum(jnp.sqrt(_dot_sel_r(kk * kk, bd)), 1e-12)
    a_f = a[:, :rw_dim]
    a_b = a[:, rw_dim:]
    ka = ka_ref[...]
    kd_f = k * (1.0 + (a_f - 1.0) * ka)
    kd_b = k * (1.0 + (a_b - 1.0) * ka)
    bonus = _dot_sel_r(r * (kd_f + kd_b) * rk_ref[...], bd) * v

    r_o[0] = r.astype(BF16)
    v_o[0] = v.astype(BF16)
    kkn_o[0] = kk.astype(BF16)
    lw0_o[0] = log_decay[:, :rw_dim]
    lw1_o[0] = log_decay[:, rw_dim:]
    kd0_o[0] = kd_f.astype(BF16)
    kd1_o[0] = kd_b.astype(BF16)
    b0_o[0] = (kk * a_f).astype(BF16)
    b1_o[0] = (kk * a_b).astype(BF16)
    g_o[0] = g.astype(BF16)
    bonus_o[0] = bonus.astype(BF16)


def _rwkv_prep(p, prm, tm):
    b, t, _ = p.shape
    rw_dim = prm["k_k"].shape[1]
    w_lora = prm["w_up"].shape[0]
    a_lora = prm["a_up"].shape[0]
    g_lora = prm["g_up"].shape[0]
    cols = 3 * rw_dim + w_lora + a_lora + g_lora
    n_halo = t // HALO_ROWS
    tm_halo = tm // HALO_ROWS

    def full(a):
        return pl.BlockSpec(a.shape, lambda bi, i: (0,) * a.ndim)

    consts = [prm["mu_prev"], prm["mu_next"], prm["w0"], prm["w_up"], prm["a0"], prm["a_up"],
              prm["g_up"], prm["k_k"], prm["k_a"], prm["r_k"], prm["bd"]]
    out_spec = pl.BlockSpec((1, tm, rw_dim), lambda bi, i: (bi, i, 0))
    out_dtypes = [BF16, BF16, BF16, F32, F32, BF16, BF16, BF16, BF16, BF16, BF16]
    return pl.pallas_call(
        functools.partial(_rwkv_prep_kernel, rw_dim=rw_dim, w_lora=w_lora, a_lora=a_lora),
        grid=(b, t // tm),
        in_specs=[pl.BlockSpec((1, tm, cols), lambda bi, i: (bi, i, 0)),
                  pl.BlockSpec((1, HALO_ROWS, cols),
                               lambda bi, i: (bi, jnp.maximum(i * tm_halo - 1, 0), 0)),
                  pl.BlockSpec((1, HALO_ROWS, cols),
                               lambda bi, i: (bi, jnp.minimum((i + 1) * tm_halo, n_halo - 1), 0))]
                 + [full(a) for a in consts],
        out_specs=[out_spec] * 11,
        out_shape=[jax.ShapeDtypeStruct((b, t, rw_dim), dt) for dt in out_dtypes],
        compiler_params=_params("parallel", "parallel"),
        name="rwkv_prep",
    )(p, p, p, *consts)


def _rwkv_chunk(kkd, rd, binv, kinv, bfin, kfin, v, s, wtot, incl2, strict2):
    c = v.shape[1]
    lanes = v.shape[2]
    la = lax.broadcasted_iota(jnp.int32, (1, 1, lanes), 2) < HEAD

    def only_a(z):
        return jnp.where(la, z, 0.0)

    def only_b(z):
        return jnp.where(la, 0.0, z)

    x = jnp.concatenate([kkd, rd], axis=1)
    gh_a = _bmm_nt(only_a(x), jnp.concatenate([binv, kinv, s], axis=1))
    gh_b = _bmm_nt(only_b(x), jnp.concatenate([kinv, binv, s], axis=1))
    a2_a = jnp.where(strict2, gh_a[:, :c, :2 * c], 0.0)
    a2_b = jnp.where(strict2, gh_b[:, :c, :2 * c], 0.0)
    v_a = only_a(v)
    v_b = only_b(v)
    akk = jnp.where(la, a2_b, a2_a)
    u = -(gh_a[:, :c, 2 * c:] + gh_b[:, :c, 2 * c:] + _bmm(akk, jnp.concatenate([v_b, v_a], axis=1)))
    def rows(y):
        return jnp.concatenate([only_a(y), only_b(y)], axis=1)

    tt = lax.broadcasted_iota(jnp.int32, (1, c, lanes), 1)
    jj = lax.broadcasted_iota(jnp.int32, (1, c, lanes), 2) % HEAD

    def same_block(m):
        return tt // m == jj // m

    nm = -jnp.where(la, a2_a, a2_b)
    nb_ = jnp.where(same_block(SOLVE_BASE), nm, 0.0)
    p = jnp.where(tt == jj, 1.0, 0.0) + nb_
    pw = _bmm(nb_, rows(nb_))
    for _ in range(SOLVE_BASE.bit_length() - 3):
        w = _bmm(pw, jnp.concatenate([rows(p), rows(pw)], axis=2))
        p = p + w[:, :, :lanes]
        pw = w[:, :, lanes:]
    p = p + _bmm(pw, rows(p))
    m = SOLVE_BASE
    while m < c:
        off = jnp.where(jnp.logical_and(same_block(2 * m), jnp.logical_not(same_block(m))), nm, 0.0)
        p = p + _bmm(_bmm(p, rows(off)), rows(p))
        m *= 2
    u = _bmm(p, rows(u))
    bmat = jnp.concatenate([jnp.where(incl2, gh_a[:, c:, :2 * c], 0.0),
                            jnp.where(incl2, gh_b[:, c:, :2 * c], 0.0)], axis=2)
    o = (gh_a[:, c:, 2 * c:] + gh_b[:, c:, 2 * c:]
         + _bmm(bmat, jnp.concatenate([only_a(u), v_a, v_b, only_b(u)], axis=1)))
    row = lax.broadcasted_iota(jnp.int32, s.shape, 1) // HEAD
    col = lax.broadcasted_iota(jnp.int32, s.shape, 2) // HEAD
    upd = _bmm_tn(jnp.concatenate([u, v], axis=1), jnp.concatenate([bfin, kfin], axis=1))
    s_new = s * wtot + jnp.where(row == col, upd, 0.0)
    return o, s_new


def _rwkv_decay_terms(r, v, kk, lw, kd, bm, reverse):
    c = r.shape[0]
    r, v, kk, kd, bm = (z.astype(F32) for z in (r, v, kk, kd, bm))
    incl, _ = _order_masks(c, reverse)
    cum = _dot_sel_l(incl.astype(BF16), lw)
    last = 0 if reverse else c - 1
    wtot = jnp.exp2(cum[last:last + 1, :])
    e_pos = jnp.exp2(cum)
    e_neg = 1.0 / e_pos
    e_fin = e_neg * wtot
    return (kk * jnp.exp2(cum - lw), r * e_pos, bm * e_neg, kd * e_neg, bm * e_fin,
            kd * e_fin, v, wtot)


def _rwkv_scan_kernel(rf, vf, kf, lwf, kdf, bf, rb, vb, kb, lwb, kdb, bb, s0_ref,
                      of_ref, ob_ref, sfin_ref, s_scr, *, heads):
    i = pl.program_id(1)

    nb = rf.shape[0]
    c = rf.shape[1]
    pair = 2 * HEAD

    @pl.when(i == 0)
    def _():
        s_scr[...] = s0_ref[...].reshape(s_scr.shape)

    pairs = heads // 2
    terms = [tm for bi in range(nb)
             for tm in (_rwkv_decay_terms(rf[bi], vf[bi], kf[bi], lwf[bi], kdf[bi], bf[bi], False),
                        _rwkv_decay_terms(rb[bi], vb[bi], kb[bi], lwb[bi], kdb[bi], bb[bi], True))]
    ops = [jnp.stack([z[:, p * pair:(p + 1) * pair] for z in zs for p in range(pairs)])
           for zs in zip(*terms)]
    n = nb * 2 * pairs
    t = lax.broadcasted_iota(jnp.int32, (n, c, 2 * c), 1)
    j = lax.broadcasted_iota(jnp.int32, (n, c, 2 * c), 2) % c
    rev = lax.broadcasted_iota(jnp.int32, (n, c, 2 * c), 0) // pairs % 2 == 1
    ahead = jnp.where(rev, t - j, j - t)
    o, s_new = _rwkv_chunk(*ops[:7], s_scr[...], ops[7], ahead <= 0, ahead < 0)
    s_scr[...] = s_new
    for bi in range(nb):
        base = bi * 2 * pairs
        of_ref[bi] = jnp.concatenate([o[base + p] for p in range(pairs)], axis=1).astype(of_ref.dtype)
        ob_ref[bi] = jnp.concatenate([o[base + pairs + p] for p in range(pairs)],
                                     axis=1).astype(ob_ref.dtype)

    @pl.when(i == pl.num_programs(1) - 1)
    def _():
        sfin_ref[...] = s_scr[...].reshape(sfin_ref.shape)


def _rwkv_scan(prep, s0):
    r, v, kk, lw0, lw1, kd0, kd1, b0, b1 = prep
    b, t, rw_dim = r.shape
    heads = rw_dim // HEAD
    n = t // CHUNK
    nb = SCAN_BATCH if b % SCAN_BATCH == 0 else 1
    fwd = pl.BlockSpec((nb, CHUNK, rw_dim), lambda bi, i: (bi, i, 0))
    bwd = pl.BlockSpec((nb, CHUNK, rw_dim), lambda bi, i: (bi, n - 1 - i, 0))
    st = pl.BlockSpec((nb,) + s0.shape[1:], lambda bi, i: (bi, 0, 0, 0))
    o_sds = jax.ShapeDtypeStruct((b, t, rw_dim), BF16)
    return pl.pallas_call(
        functools.partial(_rwkv_scan_kernel, heads=heads),
        grid=(b // nb, n),
        in_specs=[fwd] * 6 + [bwd] * 6 + [st],
        out_specs=[fwd, bwd, st],
        out_shape=[o_sds, o_sds, jax.ShapeDtypeStruct(s0.shape, F32)],
        scratch_shapes=[pltpu.VMEM((nb * s0.shape[1],) + s0.shape[2:], F32)],
        compiler_params=_params("parallel", "arbitrary"),
        name="rwkv_scan",
    )(r, v, kk, lw0, kd0, b0, r, v, kk, lw1, kd1, b1, s0)


def _softmax_pv(s_list, v_list):
    m = s_list[0].max(axis=-1, keepdims=True)
    for s in s_list[1:]:
        m = jnp.maximum(m, s.max(axis=-1, keepdims=True))
    num = None
    den = None
    for s, v in zip(s_list, v_list):
        p = jnp.exp(s - m)
        d = p.sum(axis=-1, keepdims=True)
        o = _dot(p.astype(BF16), v)
        num = o if num is None else num + o
        den = d if den is None else den + d
    return num / den


def _natten_kernel(q_ref, k_ref, v_ref, kc_ref, vc_ref, bias_ref, o_ref, *, rows, scale, rb):
    kc = kc_ref[0]
    vc = vc_ref[0]
    l = kc.shape[0]
    nwin = NA_ROWS * GRID_W
    head0 = lax.broadcasted_iota(jnp.int32, (1, 2 * HEAD), 1) < HEAD

    def body(it, carry):
        base = it * rb
        start = pl.multiple_of(base * GRID_W, rb * GRID_W)
        q = q_ref[0, pl.ds(start, rb * GRID_W), :] * scale
        zero = jnp.zeros_like(q)
        q0 = jnp.where(head0, q, zero)
        q1 = jnp.where(head0, zero, q)
        qs, kws, vws, bs = [], [], [], []
        for j in range(rb):
            r = base + j
            r0 = jnp.clip(r - NA_ROWS // 2, 0, rows - NA_ROWS)
            rows_j = slice(j * GRID_W, (j + 1) * GRID_W)
            qs.append(jnp.concatenate([q0[rows_j], q1[rows_j]], axis=0))
            win = pl.ds(pl.multiple_of(r0 * GRID_W, GRID_W), nwin)
            kws.append(k_ref[0, win, :])
            vws.append(v_ref[0, win, :])
            bs.append(bias_ref[0, r - r0])
        qs = jnp.stack(qs)
        s_win = _bmm_nt(qs, jnp.stack(kws)) + jnp.stack(bs)
        s_ctx = _dot_nt(qs.reshape(rb * 2 * GRID_W, 2 * HEAD), kc).reshape(rb, 2 * GRID_W, l)
        m = jnp.maximum(s_win.max(axis=-1, keepdims=True), s_ctx.max(axis=-1, keepdims=True))
        p_win = jnp.exp(s_win - m)
        p_ctx = jnp.exp(s_ctx - m)
        den = p_win.sum(axis=-1, keepdims=True) + p_ctx.sum(axis=-1, keepdims=True)
        o = _bmm(p_win, jnp.stack(vws))
        o = o + _dot(p_ctx.reshape(rb * 2 * GRID_W, l).astype(BF16), vc).reshape(rb, 2 * GRID_W, 2 * HEAD)
        o = o / den
        out = jnp.concatenate([jnp.where(head0, o[j, :GRID_W], o[j, GRID_W:]) for j in range(rb)], axis=0)
        o_ref[0, pl.ds(start, rb * GRID_W), :] = out.astype(o_ref.dtype)
        return carry

    lax.fori_loop(0, rows // rb, body, 0)


def _natten(p_lat, p_ctx, bias, rw_cols, na_dim):
    b, t, _ = p_lat.shape
    l = p_ctx.shape[1]
    pairs = na_dim // (2 * HEAD)
    qb = rw_cols // (2 * HEAD)
    kb = qb + pairs
    vb = kb + pairs
    rows = t // GRID_W

    def lat(off):
        return pl.BlockSpec((1, t, 2 * HEAD), lambda bi, hp: (bi, 0, off + hp))

    def ctx(off):
        return pl.BlockSpec((1, l, 2 * HEAD), lambda bi, hp: (bi, 0, off + hp))

    rb = 2 * NA_ROWS if rows % (2 * NA_ROWS) == 0 else 1
    return pl.pallas_call(
        functools.partial(_natten_kernel, rows=rows, scale=HEAD ** -0.5, rb=rb),
        grid=(b, pairs),
        in_specs=[lat(qb), lat(kb), lat(vb), ctx(kb), ctx(vb),
                  pl.BlockSpec((1, NA_ROWS, 2 * GRID_W, NA_ROWS * GRID_W), lambda bi, hp: (hp, 0, 0, 0))],
        out_specs=pl.BlockSpec((1, t, 2 * HEAD), lambda bi, hp: (bi, 0, hp)),
        out_shape=jax.ShapeDtypeStruct((b, t, na_dim), BF16),
        compiler_params=_params("parallel", "parallel"),
        name="natten",
    )(p_lat, p_lat, p_lat, p_ctx, p_ctx, bias)


def _ctx_attn_kernel(q_ref, k_ref, v_ref, o_ref, *, scale):
    q = q_ref[0] * scale
    k = k_ref[0]
    v = v_ref[0]
    outs = []
    for hh in range(2):
        sl = slice(hh * HEAD, (hh + 1) * HEAD)
        outs.append(_softmax_pv([_dot_nt(q[:, sl], k[:, sl])], [v[:, sl]]))
    o_ref[0] = jnp.concatenate(outs, axis=1).astype(o_ref.dtype)


def _ctx_attn(p_ctx, rw_cols, na_dim):
    b, l, _ = p_ctx.shape
    pairs = na_dim // (2 * HEAD)
    qb = rw_cols // (2 * HEAD)

    def blk(off):
        return pl.BlockSpec((1, l, 2 * HEAD), lambda bi, hp: (bi, 0, off + hp))

    return pl.pallas_call(
        functools.partial(_ctx_attn_kernel, scale=HEAD ** -0.5),
        grid=(b, pairs),
        in_specs=[blk(qb), blk(qb + pairs), blk(qb + 2 * pairs)],
        out_specs=pl.BlockSpec((1, l, 2 * HEAD), lambda bi, hp: (bi, 0, hp)),
        out_shape=jax.ShapeDtypeStruct((b, l, na_dim), BF16),
        compiler_params=_params("parallel", "parallel"),
        name="ctx_attn",
    )(p_ctx, p_ctx, p_ctx)


def _natten_bias(rpb):
    cols = np.arange(GRID_W)
    c0 = np.clip(cols - NA_COLS // 2, 0, GRID_W - NA_COLS)
    valid = (cols[None, :] >= c0[:, None]) & (cols[None, :] < c0[:, None] + NA_COLS)
    col_rel = cols[None, :] - cols[:, None] + NA_COLS - 1
    onehot = (col_rel[:, :, None] == np.arange(2 * NA_COLS - 1)).astype(np.float32)
    toe = jnp.einsum("hrj,qkj->hrqk", rpb.astype(F32), onehot, precision=HIGHEST)
    toe = jnp.where(valid[None, None], toe, MASK_BIAS)
    tab = jnp.stack([toe[:, NA_ROWS - 1 - d:2 * NA_ROWS - 1 - d] for d in range(NA_ROWS)], axis=1)
    tab = tab.transpose(0, 1, 3, 2, 4)
    heads = rpb.shape[0]
    tab = tab.reshape(heads // 2, 2, NA_ROWS, GRID_W, NA_ROWS * GRID_W)
    return tab.transpose(0, 2, 1, 3, 4).reshape(heads // 2, NA_ROWS, 2 * GRID_W, NA_ROWS * GRID_W)


def _even_out_kernel(h_ref, of_ref, ob_ref, bonus_ref, g_ref, na_ref, w_ref, gate_ref,
                     lng_ref, lnb_ref, bd_ref, o_ref):
    bd = bd_ref[...]
    o = of_ref[0].astype(F32) + ob_ref[0].astype(F32)
    inv = 1.0 / HEAD
    mu = _dot_sel_r(o, bd) * inv
    xc = o - mu
    var = _dot_sel_r(xc * xc, bd) * inv
    y = xc * lax.rsqrt(var + RW_GN_EPS) * lng_ref[...] + lnb_ref[...]
    rw = (y + bonus_ref[0].astype(F32)) * g_ref[0].astype(F32)
    cat = jnp.concatenate([rw.astype(BF16), na_ref[0]], axis=1)
    o_ref[0] = h_ref[0] + gate_ref[0] * _dot(cat, w_ref[...])


def _even_out(h, o_f, o_b, bonus, g, na, w_out, gate, ln_g, ln_b, bd, tm):
    b, t, d = h.shape
    rw_dim = o_f.shape[2]
    na_dim = na.shape[2]
    big = pl.BlockSpec((1, tm, d), lambda bi, i: (bi, i, 0))
    half = pl.BlockSpec((1, tm, rw_dim), lambda bi, i: (bi, i, 0))
    par = pl.BlockSpec((1, rw_dim), lambda bi, i: (0, 0))
    return pl.pallas_call(
        _even_out_kernel,
        grid=(b, t // tm),
        in_specs=[big, half, half, half, half,
                  pl.BlockSpec((1, tm, na_dim), lambda bi, i: (bi, i, 0)),
                  pl.BlockSpec(w_out.shape, lambda bi, i: (0, 0)),
                  pl.BlockSpec((1, 1, d), lambda bi, i: (bi, 0, 0)),
                  par, par,
                  pl.BlockSpec(bd.shape, lambda bi, i: (0, 0))],
        out_specs=big,
        out_shape=jax.ShapeDtypeStruct((b, t, d), F32),
        compiler_params=_params("parallel", "parallel"),
        name="even_out",
    )(h, o_f, o_b, bonus, g, na, w_out, gate, ln_g.reshape(1, rw_dim), ln_b.reshape(1, rw_dim), bd)


def _odd_out_kernel(h_ref, of_ref, ob_ref, gate_in_ref, w_ref, gate_ref, ng_ref, o_ref):
    o = of_ref[0].astype(F32) + ob_ref[0].astype(F32)
    y = _rms(o, ng_ref[...]) * _silu(gate_in_ref[0].astype(F32))
    o_ref[0] = h_ref[0] + gate_ref[0] * _dot(y.astype(BF16), w_ref[...])


def _odd_out(h, o_f, o_b, p, gate_block, w_out, gate, norm_g, tm):
    b, t, d = h.shape
    vd = o_f.shape[2]
    big = pl.BlockSpec((1, tm, d), lambda bi, i: (bi, i, 0))
    val = pl.BlockSpec((1, tm, vd), lambda bi, i: (bi, i, 0))
    return pl.pallas_call(
        _odd_out_kernel,
        grid=(b, t // tm),
        in_specs=[big, val, val,
                  pl.BlockSpec((1, tm, vd), lambda bi, i: (bi, i, gate_block)),
                  pl.BlockSpec(w_out.shape, lambda bi, i: (0, 0)),
                  pl.BlockSpec((1, 1, d), lambda bi, i: (bi, 0, 0)),
                  pl.BlockSpec((1, vd), lambda bi, i: (0, 0))],
        out_specs=big,
        out_shape=jax.ShapeDtypeStruct((b, t, d), F32),
        compiler_params=_params("parallel", "parallel"),
        name="odd_out",
    )(h, o_f, o_b, p, w_out, gate, norm_g.reshape(1, vd))


def _hgrn_terms(qp, fp, ip, lb, reverse):
    c = qp.shape[0]
    incl, _ = _order_masks(c, reverse)
    q = qp.astype(F32)
    forget = lb + (1.0 - lb) * _sigmoid(fp.astype(F32))
    k = 1.0 - forget
    cum = _dot_sel_l(incl.astype(BF16), jnp.log2(forget))
    last = 0 if reverse else c - 1
    tot = cum[last:last + 1, :]
    mid = cum[c // 2:c // 2 + 1, :]
    e_mid = jnp.exp2(cum - mid)
    q_mid = q * e_mid
    k_mid = k / e_mid
    return q_mid, k_mid, q_mid * jnp.exp2(mid), k_mid * jnp.exp2(tot - mid), ip, jnp.exp2(tot)


def _hgrn_scan_kernel(qf, ff, vf, qb, fb, vb, hl_ref, s0_ref, of_ref, ob_ref, sfin_ref, s_scr,
                      *, layer, heads):
    i = pl.program_id(1)

    @pl.when(i == 0)
    def _():
        s_scr[...] = s0_ref[...].reshape(s_scr.shape)

    hl = hl_ref[...]
    e = jnp.exp(hl - hl.max(axis=0, keepdims=True))
    sm = e / e.sum(axis=0, keepdims=True)
    lb = jnp.zeros_like(sm[0:1])
    for j in range(1, layer + 1):
        lb = lb + sm[j:j + 1]

    nb = qf.shape[0]
    c = qf.shape[1]
    n = nb * 2 * heads
    terms = [tm for bi in range(nb) for tm in (_hgrn_terms(qf[bi], ff[bi], vf[bi], lb, False),
                                               _hgrn_terms(qb[bi], fb[bi], vb[bi], lb, True))]
    q_mid, k_mid, q_cum, k_fin, v, wtot = [
        jnp.stack([z[:, h * HG_KEY:(h + 1) * HG_KEY] for z in zs for h in range(heads)])
        for zs in zip(*terms)]
    t = lax.broadcasted_iota(jnp.int32, (n, c, c), 1)
    j = lax.broadcasted_iota(jnp.int32, (n, c, c), 2)
    rev = lax.broadcasted_iota(jnp.int32, (n, c, c), 0) // heads % 2 == 1
    incl = jnp.where(rev, t - j, j - t) <= 0
    s = s_scr[...]
    att = jnp.where(incl, _bmm_nt(q_mid, k_mid), 0.0)
    o = _bmm(att, v) + _bmm_nt(q_cum, s)
    s_scr[...] = s * wtot + _bmm_tn(v, k_fin)
    for bi in range(nb):
        base = bi * 2 * heads
        of_ref[bi] = jnp.concatenate([o[base + h] for h in range(heads)], axis=1).astype(of_ref.dtype)
        ob_ref[bi] = jnp.concatenate([o[base + heads + h] for h in range(heads)],
                                     axis=1).astype(ob_ref.dtype)

    @pl.when(i == pl.num_programs(1) - 1)
    def _():
        sfin_ref[...] = s_scr[...].reshape(sfin_ref.shape)


def _hgrn_scan(p, hg_lower, s0, layer, heads):
    b, t, _ = p.shape
    n = t // CHUNK
    width = heads * HG_KEY
    depth = hg_lower.shape[0]

    nb = SCAN_BATCH if b % SCAN_BATCH == 0 else 1

    def fwd(blk):
        return pl.BlockSpec((nb, CHUNK, width), lambda bi, i: (bi, i, blk))

    def bwd(blk):
        return pl.BlockSpec((nb, CHUNK, width), lambda bi, i: (bi, n - 1 - i, blk))

    st = pl.BlockSpec((nb, 2 * heads, HG_KEY, HG_KEY), lambda bi, i: (bi, 0, 0, 0))
    o_sds = jax.ShapeDtypeStruct((b, t, width), BF16)
    return pl.pallas_call(
        functools.partial(_hgrn_scan_kernel, layer=layer, heads=heads),
        grid=(b // nb, n),
        in_specs=[fwd(0), fwd(1), fwd(3), bwd(0), bwd(2), bwd(3),
                  pl.BlockSpec((depth, width), lambda bi, i: (0, 0)), st],
        out_specs=[fwd(0), bwd(0), st],
        out_shape=[o_sds, o_sds, jax.ShapeDtypeStruct((b, 2 * heads, HG_KEY, HG_KEY), F32)],
        scratch_shapes=[pltpu.VMEM((nb * 2 * heads, HG_KEY, HG_KEY), F32)],
        compiler_params=_params("parallel", "arbitrary"),
        name="hgrn_scan",
    )(p, p, p, p, p, p, hg_lower, s0)


def _row_tile(t, target):
    return target if t % target == 0 else t


def kernel(x, c, ctx, c_ctx, norm_mix_g, norm_mlp_g, ada_w, ada_b, mlp_w1, mlp_w2, ev_w_in,
           ev_w_out, rw_mu_prev, rw_mu_next, rw_w0, rw_w_up, rw_a0, rw_a_up, rw_g_up, rw_k_k,
           rw_k_a, rw_r_k, rw_ln_g, rw_ln_b, na_rpb, od_w_in, od_w_out, hg_lower, hg_norm_g,
           final_norm_g):
    b, t, d = x.shape
    l = ctx.shape[1]
    depth = ada_w.shape[0]
    rw_dim = rw_k_k.shape[1]
    rw_heads = rw_dim // HEAD
    rw_cols = rw_mu_prev.shape[1]
    na_dim = d - rw_dim
    hg_heads = d // HG_KEY

    pad_rows = -(b + 1) % SUBLANES
    c_rows = jnp.concatenate([c, c_ctx[None], jnp.zeros((pad_rows, d), F32)], axis=0)
    mods = _ada(c_rows, ada_w, ada_b)

    def mod_vecs(layer):
        m = mods[layer]
        lat = [m[:b, j * d:(j + 1) * d].reshape(b, 1, d) for j in range(6)]
        cx = [m[b, j * d:(j + 1) * d].reshape(1, 1, d) for j in range(6)]
        return lat, cx

    eye = jnp.arange(rw_dim) // HEAD
    bd = (eye[:, None] == eye[None, :]).astype(BF16)

    def flat(z):
        return z.reshape(1, b * l, z.shape[-1])

    def per_sample(z):
        return z.reshape(b, l, z.shape[-1])

    tm_lat = _row_tile(t, 512)
    tm_out = _row_tile(t, 1024)
    tm_ctx = _row_tile(b * l, 512)
    tm_mlp_ctx = _row_tile(b * l, 1024)
    h_lat, h_ctx = x, flat(ctx)
    for layer in range(depth):
        last = layer == depth - 1
        (sh1, s1, g1, sh2, s2, g2), (csh1, cs1, cg1, csh2, cs2, cg2) = mod_vecs(layer)
        if layer % 2 == 0:
            e = layer // 2
            w_in = ev_w_in[e].astype(BF16)
            p_lat = _norm_mod_matmul(h_lat, norm_mix_g[layer], sh1, s1, w_in, tm_lat)
            p_ctx = per_sample(_norm_mod_matmul(h_ctx, norm_mix_g[layer], csh1, cs1, w_in, tm_ctx))
            prm = dict(
                mu_prev=rw_mu_prev[e][None], mu_next=rw_mu_next[e][None],
                w0=rw_w0[e].reshape(1, 2 * rw_dim),
                w_up=jnp.concatenate([rw_w_up[e, 0], rw_w_up[e, 1]], axis=1),
                a0=rw_a0[e].reshape(1, 2 * rw_dim),
                a_up=jnp.concatenate([rw_a_up[e, 0], rw_a_up[e, 1]], axis=1),
                g_up=rw_g_up[e], k_k=rw_k_k[e][None], k_a=rw_k_a[e][None],
                r_k=rw_r_k[e].reshape(1, rw_dim), bd=bd)
            s_zero = jnp.zeros((b, rw_heads, 2 * HEAD, 2 * HEAD), F32)
            prep_c = _rwkv_prep(p_ctx, prm, _row_tile(l, 256))
            oc_f, oc_b, s_ctx = _rwkv_scan(prep_c[:9], s_zero)
            prep_l = _rwkv_prep(p_lat, prm, tm_lat)
            ol_f, ol_b, _ = _rwkv_scan(prep_l[:9], s_ctx)
            na_lat = _natten(p_lat, p_ctx, _natten_bias(na_rpb[e]), rw_cols, na_dim)
            w_out = ev_w_out[e].astype(BF16)
            h_lat = _even_out(h_lat, ol_f, ol_b, prep_l[10], prep_l[9], na_lat, w_out, g1,
                              rw_ln_g[e], rw_ln_b[e], bd, tm_out)
            if not last:
                na_ctx = _ctx_attn(p_ctx, rw_cols, na_dim)
                h_ctx = _even_out(h_ctx, flat(oc_f), flat(oc_b), flat(prep_c[10]), flat(prep_c[9]),
                                  flat(na_ctx), w_out, cg1, rw_ln_g[e], rw_ln_b[e], bd, tm_ctx)
        else:
            o = layer // 2
            w_in = od_w_in[o].astype(BF16)
            hg_kdim = hg_heads * HG_KEY
            p_lat = _norm_mod_matmul(h_lat, norm_mix_g[layer], sh1, s1, w_in, tm_lat, hg_kdim)
            p_ctx = _norm_mod_matmul(h_ctx, norm_mix_g[layer], csh1, cs1, w_in, tm_ctx, hg_kdim)
            s_zero = jnp.zeros((b, 2 * hg_heads, HG_KEY, HG_KEY), F32)
            oc_f, oc_b, s_ctx = _hgrn_scan(per_sample(p_ctx), hg_lower, s_zero, layer, hg_heads)
            ol_f, ol_b, _ = _hgrn_scan(p_lat, hg_lower, s_ctx, layer, hg_heads)
            w_out = od_w_out[o].astype(BF16)
            gate_block = p_lat.shape[2] // d - 1
            h_lat = _odd_out(h_lat, ol_f, ol_b, p_lat, gate_block, w_out, g1, hg_norm_g[o], tm_out)
            if not last:
                h_ctx = _odd_out(h_ctx, flat(oc_f), flat(oc_b), p_ctx, gate_block, w_out, cg1,
                                 hg_norm_g[o], tm_ctx)
        w1 = mlp_w1[layer].astype(BF16)
        w2 = mlp_w2[layer].astype(BF16)
        h_lat = _mlp(h_lat, norm_mlp_g[layer], sh2, s2, g2, w1, w2, final_norm_g,
                     _row_tile(t, 1024), 1024, last)
        if not last:
            h_ctx = _mlp(h_ctx, norm_mlp_g[layer], csh2, cs2, cg2, w1, w2, final_norm_g,
                         tm_mlp_ctx, 1024, False)
    return h_lat
```

```python
import functools
import math

import jax
import jax.numpy as jnp
import numpy as np
from jax import lax
from jax.experimental import pallas as pl
from jax.experimental.pallas import tpu as pltpu

F32 = jnp.float32
BF16 = jnp.bfloat16
HIGHEST = lax.Precision.HIGHEST

NORM_EPS = 1e-6
RW_GN_EPS = 64e-5
HEAD = 64
GRID_W = 64
NA_ROWS = 8
NA_COLS = 16
HG_KEY = 128
CHUNK = 64
HG_EXP_CLAMP = 100.0
SOLVE_BASE = 8
MLP_ROW_CHUNK = 256
SCAN_BATCH = 4
MASK_BIAS = -1e30
DECAY_SCALE_LOG2 = math.exp(-0.5) / math.log(2.0)
SUBLANES = 8
HALO_ROWS = 16
VMEM_LIMIT = 52 * 1024 * 1024


def _params(*sem):
    return pltpu.CompilerParams(dimension_semantics=sem, vmem_limit_bytes=VMEM_LIMIT)


def _dot(a, b):
    return jnp.dot(a, b, preferred_element_type=F32)


def _dot_nt(a, b):
    return lax.dot_general(a, b, (((1,), (1,)), ((), ())), preferred_element_type=F32)


def _dot_tn(a, b):
    return lax.dot_general(a, b, (((0,), (0,)), ((), ())), preferred_element_type=F32)


def _mm(a, b):
    return _dot(a.astype(BF16), b.astype(BF16))


def _mm_nt(a, b):
    return _dot_nt(a.astype(BF16), b.astype(BF16))


def _mm_tn(a, b):
    return _dot_tn(a.astype(BF16), b.astype(BF16))


def _bmm(a, b):
    return lax.dot_general(a.astype(BF16), b.astype(BF16), (((2,), (1,)), ((0,), (0,))),
                           preferred_element_type=F32)


def _bmm_nt(a, b):
    return lax.dot_general(a.astype(BF16), b.astype(BF16), (((2,), (2,)), ((0,), (0,))),
                           preferred_element_type=F32)


def _bmm_tn(a, b):
    return lax.dot_general(a.astype(BF16), b.astype(BF16), (((1,), (1,)), ((0,), (0,))),
                           preferred_element_type=F32)


def _hi_lo(x):
    hi = x.astype(BF16)
    return hi, (x - hi.astype(F32)).astype(BF16)


def _dot_sel_r(x, sel):
    return _dot(x.astype(BF16), sel)


def _dot_sel_l(sel, x):
    hi, lo = _hi_lo(x)
    return _dot(sel, hi) + _dot(sel, lo)


def _sigmoid(x):
    return 1.0 / (1.0 + jnp.exp(-x))


def _silu(x):
    return x * _sigmoid(x)


def _rms(x, g):
    return x * lax.rsqrt(jnp.mean(x * x, axis=-1, keepdims=True) + NORM_EPS) * g


def _order_masks(n, reverse):
    t = lax.broadcasted_iota(jnp.int32, (n, n), 0)
    i = lax.broadcasted_iota(jnp.int32, (n, n), 1)
    if reverse:
        return i >= t, i > t
    return i <= t, i < t


def _ada_kernel(c_ref, w_ref, b_ref, o_ref):
    s_hi, s_lo = _hi_lo(_silu(c_ref[...]))
    w_hi, w_lo = _hi_lo(w_ref[0])
    o_ref[0] = _dot(s_hi, w_hi) + (_dot(s_hi, w_lo) + _dot(s_lo, w_hi)) + b_ref[0]


def _ada(c_rows, ada_w, ada_b):
    depth, d, n = ada_w.shape
    rows = c_rows.shape[0]
    tn = 512
    return pl.pallas_call(
        _ada_kernel,
        grid=(depth, n // tn),
        in_specs=[pl.BlockSpec((rows, d), lambda l, j: (0, 0)),
                  pl.BlockSpec((1, d, tn), lambda l, j: (l, 0, j)),
                  pl.BlockSpec((1, 1, tn), lambda l, j: (l, 0, j))],
        out_specs=pl.BlockSpec((1, rows, tn), lambda l, j: (l, 0, j)),
        out_shape=jax.ShapeDtypeStruct((depth, rows, n), F32),
        compiler_params=_params("parallel", "parallel"),
        name="ada",
    )(c_rows, ada_w, ada_b.reshape(depth, 1, n))


def _nmm_kernel(x_ref, g_ref, sh_ref, sc_ref, w_ref, o_ref, *, silu_cols):
    tm = x_ref.shape[1]
    rc = MLP_ROW_CHUNK if tm % MLP_ROW_CHUNK == 0 else tm
    for c in range(tm // rc):
        rows = pl.ds(c * rc, rc)
        a = (_rms(x_ref[0, rows, :], g_ref[...]) * (1.0 + sc_ref[0]) + sh_ref[0]).astype(BF16)
        if silu_cols:
            o_ref[0, rows, :silu_cols] = _silu(_dot(a, w_ref[:, :silu_cols])).astype(o_ref.dtype)
            o_ref[0, rows, silu_cols:] = _dot(a, w_ref[:, silu_cols:]).astype(o_ref.dtype)
        else:
            o_ref[0, rows, :] = _dot(a, w_ref[...]).astype(o_ref.dtype)


def _norm_mod_matmul(x, g, shift, scale, w, tm, silu_cols=0):
    b, t, d = x.shape
    n = w.shape[1]
    return pl.pallas_call(
        functools.partial(_nmm_kernel, silu_cols=silu_cols),
        grid=(b, t // tm),
        in_specs=[pl.BlockSpec((1, tm, d), lambda bi, i: (bi, i, 0)),
                  pl.BlockSpec((1, d), lambda bi, i: (0, 0)),
                  pl.BlockSpec((1, 1, d), lambda bi, i: (bi, 0, 0)),
                  pl.BlockSpec((1, 1, d), lambda bi, i: (bi, 0, 0)),
                  pl.BlockSpec((d, n), lambda bi, i: (0, 0))],
        out_specs=pl.BlockSpec((1, tm, n), lambda bi, i: (bi, i, 0)),
        out_shape=jax.ShapeDtypeStruct((b, t, n), BF16),
        compiler_params=_params("parallel", "parallel"),
        name="norm_mod_matmul",
    )(x, g.reshape(1, d), shift, scale, w)


def _mlp_kernel(h_ref, g_ref, sh_ref, sc_ref, gate_ref, w1_ref, w2_ref, fg_ref, o_ref,
                a_scr, acc_scr, *, final_norm):
    j = pl.program_id(2)
    last = pl.num_programs(2) - 1
    tm = a_scr.shape[0]
    rc = MLP_ROW_CHUNK if tm % MLP_ROW_CHUNK == 0 else tm

    def hidden(a):
        hid = jnp.maximum(_dot(a, w1_ref[...]), 0.0)
        return _dot((hid * hid).astype(BF16), w2_ref[...])

    @pl.when(j == 0)
    def _():
        for c in range(tm // rc):
            rows = pl.ds(c * rc, rc)
            a = _rms(h_ref[0, rows, :], g_ref[...]) * (1.0 + sc_ref[0]) + sh_ref[0]
            a = a.astype(BF16)
            a_scr[rows, :] = a
            acc_scr[rows, :] = hidden(a)

    @pl.when(jnp.logical_and(j > 0, j < last))
    def _():
        acc_scr[...] += hidden(a_scr[...])

    @pl.when(j == last)
    def _():
        for c in range(tm // rc):
            rows = pl.ds(c * rc, rc)
            out = h_ref[0, rows, :] + gate_ref[0] * (acc_scr[rows, :] + hidden(a_scr[rows, :]))
            if final_norm:
                out = _rms(out, fg_ref[...])
            o_ref[0, rows, :] = out


def _mlp(h, g, shift, scale, gate, w1, w2, final_g, tm, th, final_norm):
    b, t, d = h.shape
    hid = w1.shape[1]
    assert hid // th >= 2
    vec = pl.BlockSpec((1, 1, d), lambda bi, i, j: (bi, 0, 0))
    par = pl.BlockSpec((1, d), lambda bi, i, j: (0, 0))
    return pl.pallas_call(
        functools.partial(_mlp_kernel, final_norm=final_norm),
        grid=(b, t // tm, hid // th),
        in_specs=[pl.BlockSpec((1, tm, d), lambda bi, i, j: (bi, i, 0)),
                  par, vec, vec, vec,
                  pl.BlockSpec((d, th), lambda bi, i, j: (0, j)),
                  pl.BlockSpec((th, d), lambda bi, i, j: (j, 0)),
                  par],
        out_specs=pl.BlockSpec((1, tm, d), lambda bi, i, j: (bi, i, 0)),
        out_shape=jax.ShapeDtypeStruct((b, t, d), F32),
        scratch_shapes=[pltpu.VMEM((tm, d), BF16), pltpu.VMEM((tm, d), F32)],
        compiler_params=_params("parallel", "parallel", "arbitrary"),
        name="mlp",
    )(h, g.reshape(1, d), shift, scale, gate, w1, w2, final_g.reshape(1, d))


def _rwkv_prep_kernel(p_ref, pp_ref, pn_ref, mup_ref, mun_ref, w0_ref, wup_ref, a0_ref, aup_ref,
                      gup_ref, kk_ref, ka_ref, rk_ref, bd_ref,
                      r_o, v_o, kkn_o, lw0_o, lw1_o, kd0_o, kd1_o, b0_o, b1_o, g_o, bonus_o,
                      *, rw_dim, w_lora, a_lora):
    i = pl.program_id(1)
    n = pl.num_programs(1)
    p = p_ref[0].astype(F32)
    tm = p.shape[0]
    row = lax.broadcasted_iota(jnp.int32, (SUBLANES, 1), 0)
    first = jnp.where(i > 0, pp_ref[0].astype(F32)[HALO_ROWS - 1:HALO_ROWS, :], 0.0)
    last = jnp.where(i < n - 1, pn_ref[0].astype(F32)[0:1, :], 0.0)
    prev = pltpu.roll(p, 1, axis=0)
    prev = jnp.concatenate([jnp.where(row == 0, first, prev[:SUBLANES]), prev[SUBLANES:]], axis=0)
    nxt = pltpu.roll(p, tm - 1, axis=0)
    nxt = jnp.concatenate([nxt[:tm - SUBLANES],
                           jnp.where(row == SUBLANES - 1, last, nxt[tm - SUBLANES:])], axis=0)
    ps = p + (prev - p) * mup_ref[...] + (nxt - p) * mun_ref[...]

    r = ps[:, :rw_dim]
    k = ps[:, rw_dim:2 * rw_dim]
    v = ps[:, 2 * rw_dim:3 * rw_dim]
    off = 3 * rw_dim
    w_lo = ps[:, off:off + w_lora]
    a_lo = ps[:, off + w_lora:off + w_lora + a_lora]
    g_lo = ps[:, off + w_lora + a_lora:]

    w_pre = w0_ref[...] + _mm(jnp.tanh(w_lo), wup_ref[...])
    log_decay = -DECAY_SCALE_LOG2 * _sigmoid(w_pre)
    a = _sigmoid(a0_ref[...] + _mm(a_lo, aup_ref[...]))
    g = _mm(_sigmoid(g_lo), gup_ref[...])

    bd = bd_ref[...]
    kk = k * kk_ref[...]
    kk = kk / jnp.maximum(jnp.sqrt(_dot_sel_r(kk * kk, bd)), 1e-12)
    a_f = a[:, :rw_dim]
    a_b = a[:, rw_dim:]
    ka = ka_ref[...]
    kd_f = k * (1.0 + (a_f - 1.0) * ka)
    kd_b = k * (1.0 + (a_b - 1.0) * ka)
    bonus = _dot_sel_r(r * (kd_f + kd_b) * rk_ref[...], bd) * v

    r_o[0] = r.astype(BF16)
    v_o[0] = v.astype(BF16)
    kkn_o[0] = kk.astype(BF16)
    lw0_o[0] = log_decay[:, :rw_dim]
    lw1_o[0] = log_decay[:, rw_dim:]
    kd0_o[0] = kd_f.astype(BF16)
    kd1_o[0] = kd_b.astype(BF16)
    b0_o[0] = (kk * a_f).astype(BF16)
    b1_o[0] = (kk * a_b).astype(BF16)
    g_o[0] = g.astype(BF16)
    bonus_o[0] = bonus.astype(BF16)


def _rwkv_prep(p, prm, tm):
    b, t, _ = p.shape
    rw_dim = prm["k_k"].shape[1]
    w_lora = prm["w_up"].shape[0]
    a_lora = prm["a_up"].shape[0]
    g_lora = prm["g_up"].shape[0]
    cols = 3 * rw_dim + w_lora + a_lora + g_lora
    n_halo = t // HALO_ROWS
    tm_halo = tm // HALO_ROWS

    def full(a):
        return pl.BlockSpec(a.shape, lambda bi, i: (0,) * a.ndim)

    consts = [prm["mu_prev"], prm["mu_next"], prm["w0"], prm["w_up"], prm["a0"], prm["a_up"],
              prm["g_up"], prm["k_k"], prm["k_a"], prm["r_k"], prm["bd"]]
    out_spec = pl.BlockSpec((1, tm, rw_dim), lambda bi, i: (bi, i, 0))
    out_dtypes = [BF16, BF16, BF16, F32, F32, BF16, BF16, BF16, BF16, BF16, BF16]
    return pl.pallas_call(
        functools.partial(_rwkv_prep_kernel, rw_dim=rw_dim, w_lora=w_lora, a_lora=a_lora),
        grid=(b, t // tm),
        in_specs=[pl.BlockSpec((1, tm, cols), lambda bi, i: (bi, i, 0)),
                  pl.BlockSpec((1, HALO_ROWS, cols),
                               lambda bi, i: (bi, jnp.maximum(i * tm_halo - 1, 0), 0)),
                  pl.BlockSpec((1, HALO_ROWS, cols),
                               lambda bi, i: (bi, jnp.minimum((i + 1) * tm_halo, n_halo - 1), 0))]
                 + [full(a) for a in consts],
        out_specs=[out_spec] * 11,
        out_shape=[jax.ShapeDtypeStruct((b, t, rw_dim), dt) for dt in out_dtypes],
        compiler_params=_params("parallel", "parallel"),
        name="rwkv_prep",
    )(p, p, p, *consts)


def _rwkv_chunk(kkd, rd, binv, kinv, bfin, kfin, v, s, wtot, incl2, strict2):
    c = v.shape[1]
    lanes = v.shape[2]
    la = lax.broadcasted_iota(jnp.int32, (1, 1, lanes), 2) < HEAD

    def only_a(z):
        return jnp.where(la, z, 0.0)

    def only_b(z):
        return jnp.where(la, 0.0, z)

    x = jnp.concatenate([kkd, rd], axis=1)
    gh_a = _bmm_nt(only_a(x), jnp.concatenate([binv, kinv, s], axis=1))
    gh_b = _bmm_nt(only_b(x), jnp.concatenate([kinv, binv, s], axis=1))
    a2_a = jnp.where(strict2, gh_a[:, :c, :2 * c], 0.0)
    a2_b = jnp.where(strict2, gh_b[:, :c, :2 * c], 0.0)
    v_a = only_a(v)
    v_b = only_b(v)
    akk = jnp.where(la, a2_b, a2_a)
    u = -(gh_a[:, :c, 2 * c:] + gh_b[:, :c, 2 * c:] + _bmm(akk, jnp.concatenate([v_b, v_a], axis=1)))
    def rows(y):
        return jnp.concatenate([only_a(y), only_b(y)], axis=1)

    tt = lax.broadcasted_iota(jnp.int32, (1, c, lanes), 1)
    jj = lax.broadcasted_iota(jnp.int32, (1, c, lanes), 2) % HEAD

    def same_block(m):
        return tt // m == jj // m

    nm = -jnp.where(la, a2_a, a2_b)
    nb_ = jnp.where(same_block(SOLVE_BASE), nm, 0.0)
    p = jnp.where(tt == jj, 1.0, 0.0) + nb_
    pw = _bmm(nb_, rows(nb_))
    for _ in range(SOLVE_BASE.bit_length() - 3):
        w = _bmm(pw, jnp.concatenate([rows(p), rows(pw)], axis=2))
        p = p + w[:, :, :lanes]
        pw = w[:, :, lanes:]
    p = p + _bmm(pw, rows(p))
    m = SOLVE_BASE
    while m < c:
        off = jnp.where(jnp.logical_and(same_block(2 * m), jnp.logical_not(same_block(m))), nm, 0.0)
        p = p + _bmm(_bmm(p, rows(off)), rows(p))
        m *= 2
    u = _bmm(p, rows(u))
    bmat = jnp.concatenate([jnp.where(incl2, gh_a[:, c:, :2 * c], 0.0),
                            jnp.where(incl2, gh_b[:, c:, :2 * c], 0.0)], axis=2)
    o = (gh_a[:, c:, 2 * c:] + gh_b[:, c:, 2 * c:]
         + _bmm(bmat, jnp.concatenate([only_a(u), v_a, v_b, only_b(u)], axis=1)))
    row = lax.broadcasted_iota(jnp.int32, s.shape, 1) // HEAD
    col = lax.broadcasted_iota(jnp.int32, s.shape, 2) // HEAD
    upd = _bmm_tn(jnp.concatenate([u, v], axis=1), jnp.concatenate([bfin, kfin], axis=1))
    s_new = s * wtot + jnp.where(row == col, upd, 0.0)
    return o, s_new


def _rwkv_decay_terms(r, v, kk, lw, kd, bm, reverse):
    c = r.shape[0]
    r, v, kk, kd, bm = (z.astype(F32) for z in (r, v, kk, kd, bm))
    incl, _ = _order_masks(c, reverse)
    cum = _dot_sel_l(incl.astype(BF16), lw)
    last = 0 if reverse else c - 1
    wtot = jnp.exp2(cum[last:last + 1, :])
    e_pos = jnp.exp2(cum)
    e_neg = 1.0 / e_pos
    e_fin = e_neg * wtot
    return (kk * jnp.exp2(cum - lw), r * e_pos, bm * e_neg, kd * e_neg, bm * e_fin,
            kd * e_fin, v, wtot)


def _rwkv_scan_kernel(rf, vf, kf, lwf, kdf, bf, rb, vb, kb, lwb, kdb, bb, s0_ref,
                      of_ref, ob_ref, sfin_ref, s_scr, *, heads):
    i = pl.program_id(1)

    nb = rf.shape[0]
    c = rf.shape[1]
    pair = 2 * HEAD

    @pl.when(i == 0)
    def _():
        s_scr[...] = s0_ref[...].reshape(s_scr.shape)

    pairs = heads // 2
    terms = [tm for bi in range(nb)
             for tm in (_rwkv_decay_terms(rf[bi], vf[bi], kf[bi], lwf[bi], kdf[bi], bf[bi], False),
                        _rwkv_decay_terms(rb[bi], vb[bi], kb[bi], lwb[bi], kdb[bi], bb[bi], True))]
    ops = [jnp.stack([z[:, p * pair:(p + 1) * pair] for z in zs for p in range(pairs)])
           for zs in zip(*terms)]
    n = nb * 2 * pairs
    t = lax.broadcasted_iota(jnp.int32, (n, c, 2 * c), 1)
    j = lax.broadcasted_iota(jnp.int32, (n, c, 2 * c), 2) % c
    rev = lax.broadcasted_iota(jnp.int32, (n, c, 2 * c), 0) // pairs % 2 == 1
    ahead = jnp.where(rev, t - j, j - t)
    o, s_new = _rwkv_chunk(*ops[:7], s_scr[...], ops[7], ahead <= 0, ahead < 0)
    s_scr[...] = s_new
    for bi in range(nb):
        base = bi * 2 * pairs
        of_ref[bi] = jnp.concatenate([o[base + p] for p in range(pairs)], axis=1).astype(of_ref.dtype)
        ob_ref[bi] = jnp.concatenate([o[base + pairs + p] for p in range(pairs)],
                                     axis=1).astype(ob_ref.dtype)

    @pl.when(i == pl.num_programs(1) - 1)
    def _():
        sfin_ref[...] = s_scr[...].reshape(sfin_ref.shape)


def _rwkv_scan(prep, s0):
    r, v, kk, lw0, lw1, kd0, kd1, b0, b1 = prep
    b, t, rw_dim = r.shape
    heads = rw_dim // HEAD
    n = t // CHUNK
    nb = SCAN_BATCH if b % SCAN_BATCH == 0 else 1
    fwd = pl.BlockSpec((nb, CHUNK, rw_dim), lambda bi, i: (bi, i, 0))
    bwd = pl.BlockSpec((nb, CHUNK, rw_dim), lambda bi, i: (bi, n - 1 - i, 0))
    st = pl.BlockSpec((nb,) + s0.shape[1:], lambda bi, i: (bi, 0, 0, 0))
    o_sds = jax.ShapeDtypeStruct((b, t, rw_dim), BF16)
    return pl.pallas_call(
        functools.partial(_rwkv_scan_kernel, heads=heads),
        grid=(b // nb, n),
        in_specs=[fwd] * 6 + [bwd] * 6 + [st],
        out_specs=[fwd, bwd, st],
        out_shape=[o_sds, o_sds, jax.ShapeDtypeStruct(s0.shape, F32)],
        scratch_shapes=[pltpu.VMEM((nb * s0.shape[1],) + s0.shape[2:], F32)],
        compiler_params=_params("parallel", "arbitrary"),
        name="rwkv_scan",
    )(r, v, kk, lw0, kd0, b0, r, v, kk, lw1, kd1, b1, s0)


def _softmax_pv(s_list, v_list):
    m = s_list[0].max(axis=-1, keepdims=True)
    for s in s_list[1:]:
        m = jnp.maximum(m, s.max(axis=-1, keepdims=True))
    num = None
    den = None
    for s, v in zip(s_list, v_list):
        p = jnp.exp(s - m)
        d = p.sum(axis=-1, keepdims=True)
        o = _dot(p.astype(BF16), v)
        num = o if num is None else num + o
        den = d if den is None else den + d
    return num / den


def _natten_kernel(q_ref, k_ref, v_ref, kc_ref, vc_ref, bias_ref, o_ref, *, rows, scale, rb):
    kc = kc_ref[0]
    vc = vc_ref[0]
    l = kc.shape[0]
    nwin = NA_ROWS * GRID_W
    head0 = lax.broadcasted_iota(jnp.int32, (1, 2 * HEAD), 1) < HEAD

    def body(it, carry):
        base = it * rb
        start = pl.multiple_of(base * GRID_W, rb * GRID_W)
        q = q_ref[0, pl.ds(start, rb * GRID_W), :] * scale
        zero = jnp.zeros_like(q)
        q0 = jnp.where(head0, q, zero)
        q1 = jnp.where(head0, zero, q)
        qs, kws, vws, bs = [], [], [], []
        for j in range(rb):
            r = base + j
            r0 = jnp.clip(r - NA_ROWS // 2, 0, rows - NA_ROWS)
            rows_j = slice(j * GRID_W, (j + 1) * GRID_W)
            qs.append(jnp.concatenate([q0[rows_j], q1[rows_j]], axis=0))
            win = pl.ds(pl.multiple_of(r0 * GRID_W, GRID_W), nwin)
            kws.append(k_ref[0, win, :])
            vws.append(v_ref[0, win, :])
            bs.append(bias_ref[0, r - r0])
        qs = jnp.stack(qs)
        s_win = _bmm_nt(qs, jnp.stack(kws)) + jnp.stack(bs)
        s_ctx = _dot_nt(qs.reshape(rb * 2 * GRID_W, 2 * HEAD), kc).reshape(rb, 2 * GRID_W, l)
        m = jnp.maximum(s_win.max(axis=-1, keepdims=True), s_ctx.max(axis=-1, keepdims=True))
        p_win = jnp.exp(s_win - m)
        p_ctx = jnp.exp(s_ctx - m)
        den = p_win.sum(axis=-1, keepdims=True) + p_ctx.sum(axis=-1, keepdims=True)
        o = _bmm(p_win, jnp.stack(vws))
        o = o + _dot(p_ctx.reshape(rb * 2 * GRID_W, l).astype(BF16), vc).reshape(rb, 2 * GRID_W, 2 * HEAD)
        o = o / den
        out = jnp.concatenate([jnp.where(head0, o[j, :GRID_W], o[j, GRID_W:]) for j in range(rb)], axis=0)
        o_ref[0, pl.ds(start, rb * GRID_W), :] = out.astype(o_ref.dtype)
        return carry

    lax.fori_loop(0, rows // rb, body, 0)


def _natten(p_lat, p_ctx, bias, rw_cols, na_dim):
    b, t, _ = p_lat.shape
    l = p_ctx.shape[1]
    pairs = na_dim // (2 * HEAD)
    qb = rw_cols // (2 * HEAD)
    kb = qb + pairs
    vb = kb + pairs
    rows = t // GRID_W

    def lat(off):
        return pl.BlockSpec((1, t, 2 * HEAD), lambda bi, hp: (bi, 0, off + hp))

    def ctx(off):
        return pl.BlockSpec((1, l, 2 * HEAD), lambda bi, hp: (bi, 0, off + hp))

    rb = 2 * NA_ROWS if rows % (2 * NA_ROWS) == 0 else 1
    return pl.pallas_call(
        functools.partial(_natten_kernel, rows=rows, scale=HEAD ** -0.5, rb=rb),
        grid=(b, pairs),
        in_specs=[lat(qb), lat(kb), lat(vb), ctx(kb), ctx(vb),
                  pl.BlockSpec((1, NA_ROWS, 2 * GRID_W, NA_ROWS * GRID_W), lambda bi, hp: (hp, 0, 0, 0))],
        out_specs=pl.BlockSpec((1, t, 2 * HEAD), lambda bi, hp: (bi, 0, hp)),
        out_shape=jax.ShapeDtypeStruct((b, t, na_dim), BF16),
        compiler_params=_params("parallel", "parallel"),
        name="natten",
    )(p_lat, p_lat, p_lat, p_ctx, p_ctx, bias)


def _ctx_attn_kernel(q_ref, k_ref, v_ref, o_ref, *, scale):
    q = q_ref[0] * scale
    k = k_ref[0]
    v = v_ref[0]
    outs = []
    for hh in range(2):
        sl = slice(hh * HEAD, (hh + 1) * HEAD)
        outs.append(_softmax_pv([_dot_nt(q[:, sl], k[:, sl])], [v[:, sl]]))
    o_ref[0] = jnp.concatenate(outs, axis=1).astype(o_ref.dtype)


def _ctx_attn(p_ctx, rw_cols, na_dim):
    b, l, _ = p_ctx.shape
    pairs = na_dim // (2 * HEAD)
    qb = rw_cols // (2 * HEAD)

    def blk(off):
        return pl.BlockSpec((1, l, 2 * HEAD), lambda bi, hp: (bi, 0, off + hp))

    return pl.pallas_call(
        functools.partial(_ctx_attn_kernel, scale=HEAD ** -0.5),
        grid=(b, pairs),
        in_specs=[blk(qb), blk(qb + pairs), blk(qb + 2 * pairs)],
        out_specs=pl.BlockSpec((1, l, 2 * HEAD), lambda bi, hp: (bi, 0, hp)),
        out_shape=jax.ShapeDtypeStruct((b, l, na_dim), BF16),
        compiler_params=_params("parallel", "parallel"),
        name="ctx_attn",
    )(p_ctx, p_ctx, p_ctx)


def _natten_bias(rpb):
    cols = np.arange(GRID_W)
    c0 = np.clip(cols - NA_COLS // 2, 0, GRID_W - NA_COLS)
    valid = (cols[None, :] >= c0[:, None]) & (cols[None, :] < c0[:, None] + NA_COLS)
    col_rel = cols[None, :] - cols[:, None] + NA_COLS - 1
    onehot = (col_rel[:, :, None] == np.arange(2 * NA_COLS - 1)).astype(np.float32)
    toe = jnp.einsum("hrj,qkj->hrqk", rpb.astype(F32), onehot, precision=HIGHEST)
    toe = jnp.where(valid[None, None], toe, MASK_BIAS)
    tab = jnp.stack([toe[:, NA_ROWS - 1 - d:2 * NA_ROWS - 1 - d] for d in range(NA_ROWS)], axis=1)
    tab = tab.transpose(0, 1, 3, 2, 4)
    heads = rpb.shape[0]
    tab = tab.reshape(heads // 2, 2, NA_ROWS, GRID_W, NA_ROWS * GRID_W)
    return tab.transpose(0, 2, 1, 3, 4).reshape(heads // 2, NA_ROWS, 2 * GRID_W, NA_ROWS * GRID_W)


def _even_out_kernel(h_ref, of_ref, ob_ref, bonus_ref, g_ref, na_ref, w_ref, gate_ref,
                     lng_ref, lnb_ref, bd_ref, o_ref):
    bd = bd_ref[...]
    o = of_ref[0].astype(F32) + ob_ref[0].astype(F32)
    inv = 1.0 / HEAD
    mu = _dot_sel_r(o, bd) * inv
    xc = o - mu
    var = _dot_sel_r(xc * xc, bd) * inv
    y = xc * lax.rsqrt(var + RW_GN_EPS) * lng_ref[...] + lnb_ref[...]
    rw = (y + bonus_ref[0].astype(F32)) * g_ref[0].astype(F32)
    cat = jnp.concatenate([rw.astype(BF16), na_ref[0]], axis=1)
    o_ref[0] = h_ref[0] + gate_ref[0] * _dot(cat, w_ref[...])


def _even_out(h, o_f, o_b, bonus, g, na, w_out, gate, ln_g, ln_b, bd, tm):
    b, t, d = h.shape
    rw_dim = o_f.shape[2]
    na_dim = na.shape[2]
    big = pl.BlockSpec((1, tm, d), lambda bi, i: (bi, i, 0))
    half = pl.BlockSpec((1, tm, rw_dim), lambda bi, i: (bi, i, 0))
    par = pl.BlockSpec((1, rw_dim), lambda bi, i: (0, 0))
    return pl.pallas_call(
        _even_out_kernel,
        grid=(b, t // tm),
        in_specs=[big, half, half, half, half,
                  pl.BlockSpec((1, tm, na_dim), lambda bi, i: (bi, i, 0)),
                  pl.BlockSpec(w_out.shape, lambda bi, i: (0, 0)),
                  pl.BlockSpec((1, 1, d), lambda bi, i: (bi, 0, 0)),
                  par, par,
                  pl.BlockSpec(bd.shape, lambda bi, i: (0, 0))],
        out_specs=big,
        out_shape=jax.ShapeDtypeStruct((b, t, d), F32),
        compiler_params=_params("parallel", "parallel"),
        name="even_out",
    )(h, o_f, o_b, bonus, g, na, w_out, gate, ln_g.reshape(1, rw_dim), ln_b.reshape(1, rw_dim), bd)


def _odd_out_kernel(h_ref, of_ref, ob_ref, gate_in_ref, w_ref, gate_ref, ng_ref, o_ref):
    o = of_ref[0].astype(F32) + ob_ref[0].astype(F32)
    y = _rms(o, ng_ref[...]) * _silu(gate_in_ref[0].astype(F32))
    o_ref[0] = h_ref[0] + gate_ref[0] * _dot(y.astype(BF16), w_ref[...])


def _odd_out(h, o_f, o_b, p, gate_block, w_out, gate, norm_g, tm):
    b, t, d = h.shape
    vd = o_f.shape[2]
    big = pl.BlockSpec((1, tm, d), lambda bi, i: (bi, i, 0))
    val = pl.BlockSpec((1, tm, vd), lambda bi, i: (bi, i, 0))
    return pl.pallas_call(
        _odd_out_kernel,
        grid=(b, t // tm),
        in_specs=[big, val, val,
                  pl.BlockSpec((1, tm, vd), lambda bi, i: (bi, i, gate_block)),
                  pl.BlockSpec(w_out.shape, lambda bi, i: (0, 0)),
                  pl.BlockSpec((1, 1, d), lambda bi, i: (bi, 0, 0)),
                  pl.BlockSpec((1, vd), lambda bi, i: (0, 0))],
        out_specs=big,
        out_shape=jax.ShapeDtypeStruct((b, t, d), F32),
        compiler_params=_params("parallel", "parallel"),
        name="odd_out",
    )(h, o_f, o_b, p, w_out, gate, norm_g.reshape(1, vd))


def _hgrn_terms(qp, fp, ip, lb, reverse):
    c = qp.shape[0]
    incl, _ = _order_masks(c, reverse)
    q = qp.astype(F32)
    forget = lb + (1.0 - lb) * _sigmoid(fp.astype(F32))
    k = 1.0 - forget
    cum = _dot_sel_l(incl.astype(BF16), jnp.log2(forget))
    half, quarter = c // 2, c // 4
    row = lax.broadcasted_iota(jnp.int32, (c, 1), 0)
    if reverse:
        in_first = row >= half
        r_mid1, r_end1, r_mid2, r_last = half + quarter, half, quarter, 0
    else:
        in_first = row < half
        r_mid1, r_end1, r_mid2, r_last = quarter, half - 1, half + quarter, c - 1
    mid1, end1, mid2, tot = (cum[r:r + 1, :] for r in (r_mid1, r_end1, r_mid2, r_last))
    ref = jnp.where(in_first, mid1, mid2)
    e_ref = jnp.exp2(jnp.clip(cum - ref, -HG_EXP_CLAMP, HG_EXP_CLAMP))
    q_ref = q * e_ref
    k_ref = k / e_ref
    q_x = q_ref * jnp.exp2(mid2 - end1)
    k_x = k_ref * jnp.exp2(end1 - mid1)
    return q_ref, k_ref, q_x, k_x, q_ref * jnp.exp2(ref), k_ref * jnp.exp2(tot - ref), ip, jnp.exp2(tot)


def _hgrn_group_chunk(ops, s, reverse):
    q_ref, k_ref, q_x, k_x, q_cum, k_fin, v, wtot = ops
    c = v.shape[1]
    half = c // 2
    first = slice(half, c) if reverse else slice(0, half)
    second = slice(0, half) if reverse else slice(half, c)
    incl, _ = _order_masks(half, reverse)
    inter = _bmm_nt(q_cum, s)
    att1 = jnp.where(incl[None], _bmm_nt(q_ref[:, first], k_ref[:, first]), 0.0)
    att2 = jnp.where(incl[None], _bmm_nt(q_ref[:, second], k_ref[:, second]), 0.0)
    cross = _bmm_nt(q_x[:, second], k_x[:, first])
    o1 = inter[:, first] + _bmm(att1, v[:, first])
    o2 = inter[:, second] + _bmm(att2, v[:, second]) + _bmm(cross, v[:, first])
    o = jnp.concatenate([o2, o1] if reverse else [o1, o2], axis=1)
    return o, s * wtot + _bmm_tn(v, k_fin)


def _hgrn_scan_kernel(qf, ff, vf, qb, fb, vb, hl_ref, s0_ref, of_ref, ob_ref, sfin_ref, s_scr,
                      *, layer, heads):
    i = pl.program_id(1)

    @pl.when(i == 0)
    def _():
        for d in range(2):
            s_scr[d] = s0_ref[:, d * heads:(d + 1) * heads].reshape(s_scr.shape[1:])

    hl = hl_ref[...]
    e = jnp.exp(hl - hl.max(axis=0, keepdims=True))
    sm = e / e.sum(axis=0, keepdims=True)
    lb = jnp.zeros_like(sm[0:1])
    for j in range(1, layer + 1):
        lb = lb + sm[j:j + 1]

    nb = qf.shape[0]
    for d, (q_ref, f_ref, v_ref, o_ref) in enumerate(((qf, ff, vf, of_ref), (qb, fb, vb, ob_ref))):
        terms = [_hgrn_terms(q_ref[bi], f_ref[bi], v_ref[bi], lb, d == 1) for bi in range(nb)]
        ops = [jnp.stack([z[:, h * HG_KEY:(h + 1) * HG_KEY] for z in zs for h in range(heads)])
               for zs in zip(*terms)]
        o, s_new = _hgrn_group_chunk(ops, s_scr[d], d == 1)
        s_scr[d] = s_new
        for bi in range(nb):
            o_ref[bi] = jnp.concatenate([o[bi * heads + h] for h in range(heads)],
                                        axis=1).astype(o_ref.dtype)

    @pl.when(i == pl.num_programs(1) - 1)
    def _():
        for d in range(2):
            sfin_ref[:, d * heads:(d + 1) * heads] = s_scr[d].reshape(nb, heads, HG_KEY, HG_KEY)


def _hgrn_scan(p, hg_lower, s0, layer, heads):
    b, t, _ = p.shape
    n = t // CHUNK
    width = heads * HG_KEY
    depth = hg_lower.shape[0]

    nb = SCAN_BATCH if b % SCAN_BATCH == 0 else 1

    def fwd(blk):
        return pl.BlockSpec((nb, CHUNK, width), lambda bi, i: (bi, i, blk))

    def bwd(blk):
        return pl.BlockSpec((nb, CHUNK, width), lambda bi, i: (bi, n - 1 - i, blk))

    st = pl.BlockSpec((nb, 2 * heads, HG_KEY, HG_KEY), lambda bi, i: (bi, 0, 0, 0))
    o_sds = jax.ShapeDtypeStruct((b, t, width), BF16)
    return pl.pallas_call(
        functools.partial(_hgrn_scan_kernel, layer=layer, heads=heads),
        grid=(b // nb, n),
        in_specs=[fwd(0), fwd(1), fwd(3), bwd(0), bwd(2), bwd(3),
                  pl.BlockSpec((depth, width), lambda bi, i: (0, 0)), st],
        out_specs=[fwd(0), bwd(0), st],
        out_shape=[o_sds, o_sds, jax.ShapeDtypeStruct((b, 2 * heads, HG_KEY, HG_KEY), F32)],
        scratch_shapes=[pltpu.VMEM((2, nb * heads, HG_KEY, HG_KEY), F32)],
        compiler_params=_params("parallel", "arbitrary"),
        name="hgrn_scan",
    )(p, p, p, p, p, p, hg_lower, s0)


def _row_tile(t, target):
    return target if t % target == 0 else t


def kernel(x, c, ctx, c_ctx, norm_mix_g, norm_mlp_g, ada_w, ada_b, mlp_w1, mlp_w2, ev_w_in,
           ev_w_out, rw_mu_prev, rw_mu_next, rw_w0, rw_w_up, rw_a0, rw_a_up, rw_g_up, rw_k_k,
           rw_k_a, rw_r_k, rw_ln_g, rw_ln_b, na_rpb, od_w_in, od_w_out, hg_lower, hg_norm_g,
           final_norm_g):
    b, t, d = x.shape
    l = ctx.shape[1]
    depth = ada_w.shape[0]
    rw_dim = rw_k_k.shape[1]
    rw_heads = rw_dim // HEAD
    rw_cols = rw_mu_prev.shape[1]
    na_dim = d - rw_dim
    hg_heads = d // HG_KEY

    pad_rows = -(b + 1) % SUBLANES
    c_rows = jnp.concatenate([c, c_ctx[None], jnp.zeros((pad_rows, d), F32)], axis=0)
    mods = _ada(c_rows, ada_w, ada_b)

    def mod_vecs(layer):
        m = mods[layer]
        lat = [m[:b, j * d:(j + 1) * d].reshape(b, 1, d) for j in range(6)]
        cx = [m[b, j * d:(j + 1) * d].reshape(1, 1, d) for j in range(6)]
        return lat, cx

    eye = jnp.arange(rw_dim) // HEAD
    bd = (eye[:, None] == eye[None, :]).astype(BF16)

    def flat(z):
        return z.reshape(1, b * l, z.shape[-1])

    def per_sample(z):
        return z.reshape(b, l, z.shape[-1])

    tm_lat = _row_tile(t, 512)
    tm_out = _row_tile(t, 1024)
    tm_ctx = _row_tile(b * l, 512)
    tm_mlp_ctx = _row_tile(b * l, 1024)
    h_lat, h_ctx = x, flat(ctx)
    for layer in range(depth):
        last = layer == depth - 1
        (sh1, s1, g1, sh2, s2, g2), (csh1, cs1, cg1, csh2, cs2, cg2) = mod_vecs(layer)
        if layer % 2 == 0:
            e = layer // 2
            w_in = ev_w_in[e].astype(BF16)
            p_lat = _norm_mod_matmul(h_lat, norm_mix_g[layer], sh1, s1, w_in, tm_lat)
            p_ctx = per_sample(_norm_mod_matmul(h_ctx, norm_mix_g[layer], csh1, cs1, w_in, tm_ctx))
            prm = dict(
                mu_prev=rw_mu_prev[e][None], mu_next=rw_mu_next[e][None],
                w0=rw_w0[e].reshape(1, 2 * rw_dim),
                w_up=jnp.concatenate([rw_w_up[e, 0], rw_w_up[e, 1]], axis=1),
                a0=rw_a0[e].reshape(1, 2 * rw_dim),
                a_up=jnp.concatenate([rw_a_up[e, 0], rw_a_up[e, 1]], axis=1),
                g_up=rw_g_up[e], k_k=rw_k_k[e][None], k_a=rw_k_a[e][None],
                r_k=rw_r_k[e].reshape(1, rw_dim), bd=bd)
            s_zero = jnp.zeros((b, rw_heads, 2 * HEAD, 2 * HEAD), F32)
            prep_c = _rwkv_prep(p_ctx, prm, _row_tile(l, 256))
            oc_f, oc_b, s_ctx = _rwkv_scan(prep_c[:9], s_zero)
            prep_l = _rwkv_prep(p_lat, prm, tm_lat)
            ol_f, ol_b, _ = _rwkv_scan(prep_l[:9], s_ctx)
            na_lat = _natten(p_lat, p_ctx, _natten_bias(na_rpb[e]), rw_cols, na_dim)
            w_out = ev_w_out[e].astype(BF16)
            h_lat = _even_out(h_lat, ol_f, ol_b, prep_l[10], prep_l[9], na_lat, w_out, g1,
                              rw_ln_g[e], rw_ln_b[e], bd, tm_out)
            if not last:
                na_ctx = _ctx_attn(p_ctx, rw_cols, na_dim)
                h_ctx = _even_out(h_ctx, flat(oc_f), flat(oc_b), flat(prep_c[10]), flat(prep_c[9]),
                                  flat(na_ctx), w_out, cg1, rw_ln_g[e], rw_ln_b[e], bd, tm_ctx)
        else:
            o = layer // 2
            w_in = od_w_in[o].astype(BF16)
            hg_kdim = hg_heads * HG_KEY
            p_lat = _norm_mod_matmul(h_lat, norm_mix_g[layer], sh1, s1, w_in, tm_lat, hg_kdim)
            p_ctx = _norm_mod_matmul(h_ctx, norm_mix_g[layer], csh1, cs1, w_in, tm_ctx, hg_kdim)
            s_zero = jnp.zeros((b, 2 * hg_heads, HG_KEY, HG_KEY), F32)
            oc_f, oc_b, s_ctx = _hgrn_scan(per_sample(p_ctx), hg_lower, s_zero, layer, hg_heads)
            ol_f, ol_b, _ = _hgrn_scan(p_lat, hg_lower, s_ctx, layer, hg_heads)
            w_out = od_w_out[o].astype(BF16)
            gate_block = p_lat.shape[2] // d - 1
            h_lat = _odd_out(h_lat, ol_f, ol_b, p_lat, gate_block, w_out, g1, hg_norm_g[o], tm_out)
            if not last:
                h_ctx = _odd_out(h_ctx, flat(oc_f), flat(oc_b), p_ctx, gate_block, w_out, cg1,
                                 hg_norm_g[o], tm_ctx)
        w1 = mlp_w1[layer].astype(BF16)
        w2 = mlp_w2[layer].astype(BF16)
        h_lat = _mlp(h_lat, norm_mlp_g[layer], sh2, s2, g2, w1, w2, final_norm_g,
                     _row_tile(t, 1024), 1024, last)
        if not last:
            h_ctx = _mlp(h_ctx, norm_mlp_g[layer], csh2, cs2, cg2, w1, w2, final_norm_g,
                         tm_mlp_ctx, 1024, False)
    return h_lat
```

```python
import functools
import math

import jax
import jax.numpy as jnp
import numpy as np
from jax import lax
from jax.experimental import pallas as pl
from jax.experimental.pallas import tpu as pltpu

F32 = jnp.float32
BF16 = jnp.bfloat16
HIGHEST = lax.Precision.HIGHEST

NORM_EPS = 1e-6
RW_GN_EPS = 64e-5
HEAD = 64
GRID_W = 64
NA_ROWS = 8
NA_COLS = 16
HG_KEY = 128
CHUNK = 64
HG_EXP_CLAMP = 100.0
SOLVE_BASE = 8
MLP_ROW_CHUNK = 256
SCAN_BATCH = 4
MASK_BIAS = -1e30
DECAY_SCALE_LOG2 = math.exp(-0.5) / math.log(2.0)
SUBLANES = 8
HALO_ROWS = 16
VMEM_LIMIT = 52 * 1024 * 1024


def _params(*sem):
    return pltpu.CompilerParams(dimension_semantics=sem, vmem_limit_bytes=VMEM_LIMIT)


def _dot(a, b):
    return jnp.dot(a, b, preferred_element_type=F32)


def _dot_nt(a, b):
    return lax.dot_general(a, b, (((1,), (1,)), ((), ())), preferred_element_type=F32)


def _dot_tn(a, b):
    return lax.dot_general(a, b, (((0,), (0,)), ((), ())), preferred_element_type=F32)


def _mm(a, b):
    return _dot(a.astype(BF16), b.astype(BF16))


def _mm_nt(a, b):
    return _dot_nt(a.astype(BF16), b.astype(BF16))


def _mm_tn(a, b):
    return _dot_tn(a.astype(BF16), b.astype(BF16))


def _bmm(a, b):
    return lax.dot_general(a.astype(BF16), b.astype(BF16), (((2,), (1,)), ((0,), (0,))),
                           preferred_element_type=F32)


def _bmm_nt(a, b):
    return lax.dot_general(a.astype(BF16), b.astype(BF16), (((2,), (2,)), ((0,), (0,))),
                           preferred_element_type=F32)


def _bmm_tn(a, b):
    return lax.dot_general(a.astype(BF16), b.astype(BF16), (((1,), (1,)), ((0,), (0,))),
                           preferred_element_type=F32)


def _hi_lo(x):
    hi = x.astype(BF16)
    return hi, (x - hi.astype(F32)).astype(BF16)


def _dot_sel_r(x, sel):
    return _dot(x.astype(BF16), sel)


def _dot_sel_l(sel, x):
    hi, lo = _hi_lo(x)
    return _dot(sel, hi) + _dot(sel, lo)


def _sigmoid(x):
    return 1.0 / (1.0 + jnp.exp(-x))


def _silu(x):
    return x * _sigmoid(x)


def _rms(x, g):
    return x * lax.rsqrt(jnp.mean(x * x, axis=-1, keepdims=True) + NORM_EPS) * g


def _order_masks(n, reverse):
    t = lax.broadcasted_iota(jnp.int32, (n, n), 0)
    i = lax.broadcasted_iota(jnp.int32, (n, n), 1)
    if reverse:
        return i >= t, i > t
    return i <= t, i < t


def _ada_kernel(c_ref, w_ref, b_ref, o_ref):
    s_hi, s_lo = _hi_lo(_silu(c_ref[...]))
    w_hi, w_lo = _hi_lo(w_ref[0])
    o_ref[0] = _dot(s_hi, w_hi) + (_dot(s_hi, w_lo) + _dot(s_lo, w_hi)) + b_ref[0]


def _ada(c_rows, ada_w, ada_b):
    depth, d, n = ada_w.shape
    rows = c_rows.shape[0]
    tn = 512
    return pl.pallas_call(
        _ada_kernel,
        grid=(depth, n // tn),
        in_specs=[pl.BlockSpec((rows, d), lambda l, j: (0, 0)),
                  pl.BlockSpec((1, d, tn), lambda l, j: (l, 0, j)),
                  pl.BlockSpec((1, 1, tn), lambda l, j: (l, 0, j))],
        out_specs=pl.BlockSpec((1, rows, tn), lambda l, j: (l, 0, j)),
        out_shape=jax.ShapeDtypeStruct((depth, rows, n), F32),
        compiler_params=_params("parallel", "parallel"),
        name="ada",
    )(c_rows, ada_w, ada_b.reshape(depth, 1, n))


def _nmm_kernel(x_ref, g_ref, sh_ref, sc_ref, w_ref, o_ref, *, silu_cols):
    tm = x_ref.shape[1]
    rc = MLP_ROW_CHUNK if tm % MLP_ROW_CHUNK == 0 else tm
    for c in range(tm // rc):
        rows = pl.ds(c * rc, rc)
        a = (_rms(x_ref[0, rows, :], g_ref[...]) * (1.0 + sc_ref[0]) + sh_ref[0]).astype(BF16)
        if silu_cols:
            o_ref[0, rows, :silu_cols] = _silu(_dot(a, w_ref[:, :silu_cols])).astype(o_ref.dtype)
            o_ref[0, rows, silu_cols:] = _dot(a, w_ref[:, silu_cols:]).astype(o_ref.dtype)
        else:
            o_ref[0, rows, :] = _dot(a, w_ref[...]).astype(o_ref.dtype)


def _norm_mod_matmul(x, g, shift, scale, w, tm, silu_cols=0):
    b, t, d = x.shape
    n = w.shape[1]
    return pl.pallas_call(
        functools.partial(_nmm_kernel, silu_cols=silu_cols),
        grid=(b, t // tm),
        in_specs=[pl.BlockSpec((1, tm, d), lambda bi, i: (bi, i, 0)),
                  pl.BlockSpec((1, d), lambda bi, i: (0, 0)),
                  pl.BlockSpec((1, 1, d), lambda bi, i: (bi, 0, 0)),
                  pl.BlockSpec((1, 1, d), lambda bi, i: (bi, 0, 0)),
                  pl.BlockSpec((d, n), lambda bi, i: (0, 0))],
        out_specs=pl.BlockSpec((1, tm, n), lambda bi, i: (bi, i, 0)),
        out_shape=jax.ShapeDtypeStruct((b, t, n), BF16),
        compiler_params=_params("parallel", "parallel"),
        name="norm_mod_matmul",
    )(x, g.reshape(1, d), shift, scale, w)


def _mlp_kernel(h_ref, g_ref, sh_ref, sc_ref, gate_ref, w1_ref, w2_ref, fg_ref, o_ref,
                a_scr, acc_scr, *, final_norm):
    j = pl.program_id(2)
    last = pl.num_programs(2) - 1
    tm = a_scr.shape[0]
    rc = MLP_ROW_CHUNK if tm % MLP_ROW_CHUNK == 0 else tm

    def hidden(a):
        hid = jnp.maximum(_dot(a, w1_ref[...]), 0.0)
        return _dot((hid * hid).astype(BF16), w2_ref[...])

    @pl.when(j == 0)
    def _():
        for c in range(tm // rc):
            rows = pl.ds(c * rc, rc)
            a = _rms(h_ref[0, rows, :], g_ref[...]) * (1.0 + sc_ref[0]) + sh_ref[0]
            a = a.astype(BF16)
            a_scr[rows, :] = a
            acc_scr[rows, :] = hidden(a)

    @pl.when(jnp.logical_and(j > 0, j < last))
    def _():
        acc_scr[...] += hidden(a_scr[...])

    @pl.when(j == last)
    def _():
        for c in range(tm // rc):
            rows = pl.ds(c * rc, rc)
            out = h_ref[0, rows, :] + gate_ref[0] * (acc_scr[rows, :] + hidden(a_scr[rows, :]))
            if final_norm:
                out = _rms(out, fg_ref[...])
            o_ref[0, rows, :] = out


def _mlp(h, g, shift, scale, gate, w1, w2, layer, final_g, tm, th, final_norm):
    b, t, d = h.shape
    hid = w1.shape[2]
    assert hid // th >= 2
    vec = pl.BlockSpec((1, 1, d), lambda bi, i, j: (bi, 0, 0))
    par = pl.BlockSpec((1, d), lambda bi, i, j: (0, 0))
    return pl.pallas_call(
        functools.partial(_mlp_kernel, final_norm=final_norm),
        grid=(b, t // tm, hid // th),
        in_specs=[pl.BlockSpec((1, tm, d), lambda bi, i, j: (bi, i, 0)),
                  par, vec, vec, vec,
                  pl.BlockSpec((None, d, th), lambda bi, i, j: (layer, 0, j)),
                  pl.BlockSpec((None, th, d), lambda bi, i, j: (layer, j, 0)),
                  par],
        out_specs=pl.BlockSpec((1, tm, d), lambda bi, i, j: (bi, i, 0)),
        out_shape=jax.ShapeDtypeStruct((b, t, d), F32),
        scratch_shapes=[pltpu.VMEM((tm, d), BF16), pltpu.VMEM((tm, d), F32)],
        compiler_params=_params("parallel", "parallel", "arbitrary"),
        name="mlp",
    )(h, g.reshape(1, d), shift, scale, gate, w1, w2, final_g.reshape(1, d))


def _rwkv_prep_kernel(p_ref, pp_ref, pn_ref, mup_ref, mun_ref, w0_ref, wup_ref, a0_ref, aup_ref,
                      gup_ref, kk_ref, ka_ref, rk_ref, bd_ref,
                      r_o, v_o, kkn_o, lw0_o, lw1_o, kd0_o, kd1_o, b0_o, b1_o, g_o, bonus_o,
                      *, rw_dim, w_lora, a_lora):
    i = pl.program_id(1)
    n = pl.num_programs(1)
    p = p_ref[0].astype(F32)
    tm = p.shape[0]
    row = lax.broadcasted_iota(jnp.int32, (SUBLANES, 1), 0)
    first = jnp.where(i > 0, pp_ref[0].astype(F32)[HALO_ROWS - 1:HALO_ROWS, :], 0.0)
    last = jnp.where(i < n - 1, pn_ref[0].astype(F32)[0:1, :], 0.0)
    prev = pltpu.roll(p, 1, axis=0)
    prev = jnp.concatenate([jnp.where(row == 0, first, prev[:SUBLANES]), prev[SUBLANES:]], axis=0)
    nxt = pltpu.roll(p, tm - 1, axis=0)
    nxt = jnp.concatenate([nxt[:tm - SUBLANES],
                           jnp.where(row == SUBLANES - 1, last, nxt[tm - SUBLANES:])], axis=0)
    ps = p + (prev - p) * mup_ref[...] + (nxt - p) * mun_ref[...]

    r = ps[:, :rw_dim]
    k = ps[:, rw_dim:2 * rw_dim]
    v = ps[:, 2 * rw_dim:3 * rw_dim]
    off = 3 * rw_dim
    w_lo = ps[:, off:off + w_lora]
    a_lo = ps[:, off + w_lora:off + w_lora + a_lora]
    g_lo = ps[:, off + w_lora + a_lora:]

    w_pre = w0_ref[...] + _mm(jnp.tanh(w_lo), wup_ref[...])
    log_decay = -DECAY_SCALE_LOG2 * _sigmoid(w_pre)
    a = _sigmoid(a0_ref[...] + _mm(a_lo, aup_ref[...]))
    g = _mm(_sigmoid(g_lo), gup_ref[...])

    bd = bd_ref[...]
    kk = k * kk_ref[...]
    kk = kk / jnp.maximum(jnp.sqrt(_dot_sel_r(kk * kk, bd)), 1e-12)
    a_f = a[:, :rw_dim]
    a_b = a[:, rw_dim:]
    ka = ka_ref[...]
    kd_f = k * (1.0 + (a_f - 1.0) * ka)
    kd_b = k * (1.0 + (a_b - 1.0) * ka)
    bonus = _dot_sel_r(r * (kd_f + kd_b) * rk_ref[...], bd) * v

    r_o[0] = r.astype(BF16)
    v_o[0] = v.astype(BF16)
    kkn_o[0] = kk.astype(BF16)
    lw0_o[0] = log_decay[:, :rw_dim]
    lw1_o[0] = log_decay[:, rw_dim:]
    kd0_o[0] = kd_f.astype(BF16)
    kd1_o[0] = kd_b.astype(BF16)
    b0_o[0] = (kk * a_f).astype(BF16)
    b1_o[0] = (kk * a_b).astype(BF16)
    g_o[0] = g.astype(BF16)
    bonus_o[0] = bonus.astype(BF16)


def _rwkv_prep(p, prm, tm):
    b, t, _ = p.shape
    rw_dim = prm["k_k"].shape[1]
    w_lora = prm["w_up"].shape[0]
    a_lora = prm["a_up"].shape[0]
    g_lora = prm["g_up"].shape[0]
    cols = 3 * rw_dim + w_lora + a_lora + g_lora
    n_halo = t // HALO_ROWS
    tm_halo = tm // HALO_ROWS

    def full(a):
        return pl.BlockSpec(a.shape, lambda bi, i: (0,) * a.ndim)

    consts = [prm["mu_prev"], prm["mu_next"], prm["w0"], prm["w_up"], prm["a0"], prm["a_up"],
              prm["g_up"], prm["k_k"], prm["k_a"], prm["r_k"], prm["bd"]]
    out_spec = pl.BlockSpec((1, tm, rw_dim), lambda bi, i: (bi, i, 0))
    out_dtypes = [BF16, BF16, BF16, F32, F32, BF16, BF16, BF16, BF16, BF16, BF16]
    return pl.pallas_call(
        functools.partial(_rwkv_prep_kernel, rw_dim=rw_dim, w_lora=w_lora, a_lora=a_lora),
        grid=(b, t // tm),
        in_specs=[pl.BlockSpec((1, tm, cols), lambda bi, i: (bi, i, 0)),
                  pl.BlockSpec((1, HALO_ROWS, cols),
                               lambda bi, i: (bi, jnp.maximum(i * tm_halo - 1, 0), 0)),
                  pl.BlockSpec((1, HALO_ROWS, cols),
                               lambda bi, i: (bi, jnp.minimum((i + 1) * tm_halo, n_halo - 1), 0))]
                 + [full(a) for a in consts],
        out_specs=[out_spec] * 11,
        out_shape=[jax.ShapeDtypeStruct((b, t, rw_dim), dt) for dt in out_dtypes],
        compiler_params=_params("parallel", "parallel"),
        name="rwkv_prep",
    )(p, p, p, *consts)


def _rwkv_chunk(kkd, rd, binv, kinv, bfin, kfin, v, s, wtot, incl2, strict2):
    c = v.shape[1]
    lanes = v.shape[2]
    la = lax.broadcasted_iota(jnp.int32, (1, 1, lanes), 2) < HEAD

    def only_a(z):
        return jnp.where(la, z, 0.0)

    def only_b(z):
        return jnp.where(la, 0.0, z)

    x = jnp.concatenate([kkd, rd], axis=1)
    gh_a = _bmm_nt(only_a(x), jnp.concatenate([binv, kinv, s], axis=1))
    gh_b = _bmm_nt(only_b(x), jnp.concatenate([kinv, binv, s], axis=1))
    a2_a = jnp.where(strict2, gh_a[:, :c, :2 * c], 0.0)
    a2_b = jnp.where(strict2, gh_b[:, :c, :2 * c], 0.0)
    v_a = only_a(v)
    v_b = only_b(v)
    akk = jnp.where(la, a2_b, a2_a)
    u = -(gh_a[:, :c, 2 * c:] + gh_b[:, :c, 2 * c:] + _bmm(akk, jnp.concatenate([v_b, v_a], axis=1)))
    def rows(y):
        return jnp.concatenate([only_a(y), only_b(y)], axis=1)

    tt = lax.broadcasted_iota(jnp.int32, (1, c, lanes), 1)
    jj = lax.broadcasted_iota(jnp.int32, (1, c, lanes), 2) % HEAD

    def same_block(m):
        return tt // m == jj // m

    nm = -jnp.where(la, a2_a, a2_b)
    nb_ = jnp.where(same_block(SOLVE_BASE), nm, 0.0)
    p = jnp.where(tt == jj, 1.0, 0.0) + nb_
    pw = _bmm(nb_, rows(nb_))
    for _ in range(SOLVE_BASE.bit_length() - 3):
        w = _bmm(pw, jnp.concatenate([rows(p), rows(pw)], axis=2))
        p = p + w[:, :, :lanes]
        pw = w[:, :, lanes:]
    p = p + _bmm(pw, rows(p))
    m = SOLVE_BASE
    while 2 * m < c:
        off = jnp.where(jnp.logical_and(same_block(2 * m), jnp.logical_not(same_block(m))), nm, 0.0)
        p = p + _bmm(_bmm(p, rows(off)), rows(p))
        m *= 2
    off = jnp.where(same_block(m), 0.0, nm)
    w = _bmm(p, jnp.concatenate([rows(u), rows(off)], axis=2))
    u = w[:, :, :lanes]
    u = u + _bmm(w[:, :, lanes:], rows(u))
    bmat = jnp.concatenate([jnp.where(incl2, gh_a[:, c:, :2 * c], 0.0),
                            jnp.where(incl2, gh_b[:, c:, :2 * c], 0.0)], axis=2)
    o = (gh_a[:, c:, 2 * c:] + gh_b[:, c:, 2 * c:]
         + _bmm(bmat, jnp.concatenate([only_a(u), v_a, v_b, only_b(u)], axis=1)))
    row = lax.broadcasted_iota(jnp.int32, s.shape, 1) // HEAD
    col = lax.broadcasted_iota(jnp.int32, s.shape, 2) // HEAD
    upd = _bmm_tn(jnp.concatenate([u, v], axis=1), jnp.concatenate([bfin, kfin], axis=1))
    s_new = s * wtot + jnp.where(row == col, upd, 0.0)
    return o, s_new


def _rwkv_decay_terms(r, v, kk, lw, kd, bm, reverse):
    c = r.shape[0]
    r, v, kk, kd, bm = (z.astype(F32) for z in (r, v, kk, kd, bm))
    incl, _ = _order_masks(c, reverse)
    cum = _dot_sel_l(incl.astype(BF16), lw)
    last = 0 if reverse else c - 1
    wtot = jnp.exp2(cum[last:last + 1, :])
    e_pos = jnp.exp2(cum)
    e_neg = 1.0 / e_pos
    e_fin = e_neg * wtot
    return (kk * jnp.exp2(cum - lw), r * e_pos, bm * e_neg, kd * e_neg, bm * e_fin,
            kd * e_fin, v, wtot)


def _rwkv_scan_kernel(rf, vf, kf, lwf, kdf, bf, rb, vb, kb, lwb, kdb, bb, s0_ref,
                      of_ref, ob_ref, sfin_ref, s_scr, *, heads):
    i = pl.program_id(1)

    nb = rf.shape[0]
    c = rf.shape[1]
    pair = 2 * HEAD

    @pl.when(i == 0)
    def _():
        s_scr[...] = s0_ref[...].reshape(s_scr.shape)

    pairs = heads // 2
    terms = [tm for bi in range(nb)
             for tm in (_rwkv_decay_terms(rf[bi], vf[bi], kf[bi], lwf[bi], kdf[bi], bf[bi], False),
                        _rwkv_decay_terms(rb[bi], vb[bi], kb[bi], lwb[bi], kdb[bi], bb[bi], True))]
    ops = [jnp.stack([z[:, p * pair:(p + 1) * pair] for z in zs for p in range(pairs)])
           for zs in zip(*terms)]
    n = nb * 2 * pairs
    t = lax.broadcasted_iota(jnp.int32, (n, c, 2 * c), 1)
    j = lax.broadcasted_iota(jnp.int32, (n, c, 2 * c), 2) % c
    rev = lax.broadcasted_iota(jnp.int32, (n, c, 2 * c), 0) // pairs % 2 == 1
    ahead = jnp.where(rev, t - j, j - t)
    o, s_new = _rwkv_chunk(*ops[:7], s_scr[...], ops[7], ahead <= 0, ahead < 0)
    s_scr[...] = s_new
    for bi in range(nb):
        base = bi * 2 * pairs
        of_ref[bi] = jnp.concatenate([o[base + p] for p in range(pairs)], axis=1).astype(of_ref.dtype)
        ob_ref[bi] = jnp.concatenate([o[base + pairs + p] for p in range(pairs)],
                                     axis=1).astype(ob_ref.dtype)

    @pl.when(i == pl.num_programs(1) - 1)
    def _():
        sfin_ref[...] = s_scr[...].reshape(sfin_ref.shape)


def _rwkv_scan(prep, s0):
    r, v, kk, lw0, lw1, kd0, kd1, b0, b1 = prep
    b, t, rw_dim = r.shape
    heads = rw_dim // HEAD
    n = t // CHUNK
    nb = SCAN_BATCH if b % SCAN_BATCH == 0 else 1
    fwd = pl.BlockSpec((nb, CHUNK, rw_dim), lambda bi, i: (bi, i, 0))
    bwd = pl.BlockSpec((nb, CHUNK, rw_dim), lambda bi, i: (bi, n - 1 - i, 0))
    st = pl.BlockSpec((nb,) + s0.shape[1:], lambda bi, i: (bi, 0, 0, 0))
    o_sds = jax.ShapeDtypeStruct((b, t, rw_dim), BF16)
    return pl.pallas_call(
        functools.partial(_rwkv_scan_kernel, heads=heads),
        grid=(b // nb, n),
        in_specs=[fwd] * 6 + [bwd] * 6 + [st],
        out_specs=[fwd, bwd, st],
        out_shape=[o_sds, o_sds, jax.ShapeDtypeStruct(s0.shape, F32)],
        scratch_shapes=[pltpu.VMEM((nb * s0.shape[1],) + s0.shape[2:], F32)],
        compiler_params=_params("parallel", "arbitrary"),
        name="rwkv_scan",
    )(r, v, kk, lw0, kd0, b0, r, v, kk, lw1, kd1, b1, s0)


def _softmax_pv(s_list, v_list):
    m = s_list[0].max(axis=-1, keepdims=True)
    for s in s_list[1:]:
        m = jnp.maximum(m, s.max(axis=-1, keepdims=True))
    num = None
    den = None
    for s, v in zip(s_list, v_list):
        p = jnp.exp(s - m)
        d = p.sum(axis=-1, keepdims=True)
        o = _dot(p.astype(BF16), v)
        num = o if num is None else num + o
        den = d if den is None else den + d
    return num / den


def _natten_kernel(q_ref, k_ref, v_ref, kc_ref, vc_ref, bias_ref, o_ref, *, rows, scale, rb):
    kc = kc_ref[0]
    vc = vc_ref[0]
    l = kc.shape[0]
    nwin = NA_ROWS * GRID_W
    head0 = lax.broadcasted_iota(jnp.int32, (1, 2 * HEAD), 1) < HEAD

    def body(it, carry):
        base = it * rb
        start = pl.multiple_of(base * GRID_W, rb * GRID_W)
        q = q_ref[0, pl.ds(start, rb * GRID_W), :] * scale
        zero = jnp.zeros_like(q)
        q0 = jnp.where(head0, q, zero)
        q1 = jnp.where(head0, zero, q)
        qs, kws, vws, bs = [], [], [], []
        for j in range(rb):
            r = base + j
            r0 = jnp.clip(r - NA_ROWS // 2, 0, rows - NA_ROWS)
            rows_j = slice(j * GRID_W, (j + 1) * GRID_W)
            qs.append(jnp.concatenate([q0[rows_j], q1[rows_j]], axis=0))
            win = pl.ds(pl.multiple_of(r0 * GRID_W, GRID_W), nwin)
            kws.append(k_ref[0, win, :])
            vws.append(v_ref[0, win, :])
            bs.append(bias_ref[0, r - r0])
        qs = jnp.stack(qs)
        s_win = _bmm_nt(qs, jnp.stack(kws)) + jnp.stack(bs)
        s_ctx = _dot_nt(qs.reshape(rb * 2 * GRID_W, 2 * HEAD), kc).reshape(rb, 2 * GRID_W, l)
        m = jnp.maximum(s_win.max(axis=-1, keepdims=True), s_ctx.max(axis=-1, keepdims=True))
        p_win = jnp.exp(s_win - m)
        p_ctx = jnp.exp(s_ctx - m)
        den = p_win.sum(axis=-1, keepdims=True) + p_ctx.sum(axis=-1, keepdims=True)
        o = _bmm(p_win, jnp.stack(vws))
        o = o + _dot(p_ctx.reshape(rb * 2 * GRID_W, l).astype(BF16), vc).reshape(rb, 2 * GRID_W, 2 * HEAD)
        o = o / den
        out = jnp.concatenate([jnp.where(head0, o[j, :GRID_W], o[j, GRID_W:]) for j in range(rb)], axis=0)
        o_ref[0, pl.ds(start, rb * GRID_W), :] = out.astype(o_ref.dtype)
        return carry

    lax.fori_loop(0, rows // rb, body, 0)


def _natten(p_lat, p_ctx, bias, rw_cols, na_dim):
    b, t, _ = p_lat.shape
    l = p_ctx.shape[1]
    pairs = na_dim // (2 * HEAD)
    qb = rw_cols // (2 * HEAD)
    kb = qb + pairs
    vb = kb + pairs
    rows = t // GRID_W

    def lat(off):
        return pl.BlockSpec((1, t, 2 * HEAD), lambda bi, hp: (bi, 0, off + hp))

    def ctx(off):
        return pl.BlockSpec((1, l, 2 * HEAD), lambda bi, hp: (bi, 0, off + hp))

    rb = 2 * NA_ROWS if rows % (2 * NA_ROWS) == 0 else 1
    return pl.pallas_call(
        functools.partial(_natten_kernel, rows=rows, scale=HEAD ** -0.5, rb=rb),
        grid=(b, pairs),
        in_specs=[lat(qb), lat(kb), lat(vb), ctx(kb), ctx(vb),
                  pl.BlockSpec((1, NA_ROWS, 2 * GRID_W, NA_ROWS * GRID_W), lambda bi, hp: (hp, 0, 0, 0))],
        out_specs=pl.BlockSpec((1, t, 2 * HEAD), lambda bi, hp: (bi, 0, hp)),
        out_shape=jax.ShapeDtypeStruct((b, t, na_dim), BF16),
        compiler_params=_params("parallel", "parallel"),
        name="natten",
    )(p_lat, p_lat, p_lat, p_ctx, p_ctx, bias)


def _ctx_attn_kernel(q_ref, k_ref, v_ref, o_ref, *, scale):
    q = q_ref[0] * scale
    k = k_ref[0]
    v = v_ref[0]
    outs = []
    for hh in range(2):
        sl = slice(hh * HEAD, (hh + 1) * HEAD)
        outs.append(_softmax_pv([_dot_nt(q[:, sl], k[:, sl])], [v[:, sl]]))
    o_ref[0] = jnp.concatenate(outs, axis=1).astype(o_ref.dtype)


def _ctx_attn(p_ctx, rw_cols, na_dim):
    b, l, _ = p_ctx.shape
    pairs = na_dim // (2 * HEAD)
    qb = rw_cols // (2 * HEAD)

    def blk(off):
        return pl.BlockSpec((1, l, 2 * HEAD), lambda bi, hp: (bi, 0, off + hp))

    return pl.pallas_call(
        functools.partial(_ctx_attn_kernel, scale=HEAD ** -0.5),
        grid=(b, pairs),
        in_specs=[blk(qb), blk(qb + pairs), blk(qb + 2 * pairs)],
        out_specs=pl.BlockSpec((1, l, 2 * HEAD), lambda bi, hp: (bi, 0, hp)),
        out_shape=jax.ShapeDtypeStruct((b, l, na_dim), BF16),
        compiler_params=_params("parallel", "parallel"),
        name="ctx_attn",
    )(p_ctx, p_ctx, p_ctx)


def _natten_bias(rpb):
    cols = np.arange(GRID_W)
    c0 = np.clip(cols - NA_COLS // 2, 0, GRID_W - NA_COLS)
    valid = (cols[None, :] >= c0[:, None]) & (cols[None, :] < c0[:, None] + NA_COLS)
    col_rel = cols[None, :] - cols[:, None] + NA_COLS - 1
    onehot = (col_rel[:, :, None] == np.arange(2 * NA_COLS - 1)).astype(np.float32)
    toe = jnp.einsum("hrj,qkj->hrqk", rpb.astype(F32), onehot, precision=HIGHEST)
    toe = jnp.where(valid[None, None], toe, MASK_BIAS)
    tab = jnp.stack([toe[:, NA_ROWS - 1 - d:2 * NA_ROWS - 1 - d] for d in range(NA_ROWS)], axis=1)
    tab = tab.transpose(0, 1, 3, 2, 4)
    heads = rpb.shape[0]
    tab = tab.reshape(heads // 2, 2, NA_ROWS, GRID_W, NA_ROWS * GRID_W)
    return tab.transpose(0, 2, 1, 3, 4).reshape(heads // 2, NA_ROWS, 2 * GRID_W, NA_ROWS * GRID_W)


def _even_out_kernel(h_ref, of_ref, ob_ref, bonus_ref, g_ref, na_ref, w_ref, gate_ref,
                     lng_ref, lnb_ref, bd_ref, o_ref):
    bd = bd_ref[...]
    o = of_ref[0].astype(F32) + ob_ref[0].astype(F32)
    inv = 1.0 / HEAD
    mu = _dot_sel_r(o, bd) * inv
    xc = o - mu
    var = _dot_sel_r(xc * xc, bd) * inv
    y = xc * lax.rsqrt(var + RW_GN_EPS) * lng_ref[...] + lnb_ref[...]
    rw = (y + bonus_ref[0].astype(F32)) * g_ref[0].astype(F32)
    cat = jnp.concatenate([rw.astype(BF16), na_ref[0]], axis=1)
    o_ref[0] = h_ref[0] + gate_ref[0] * _dot(cat, w_ref[...])


def _even_out(h, o_f, o_b, bonus, g, na, w_out, gate, ln_g, ln_b, bd, tm):
    b, t, d = h.shape
    rw_dim = o_f.shape[2]
    na_dim = na.shape[2]
    big = pl.BlockSpec((1, tm, d), lambda bi, i: (bi, i, 0))
    half = pl.BlockSpec((1, tm, rw_dim), lambda bi, i: (bi, i, 0))
    par = pl.BlockSpec((1, rw_dim), lambda bi, i: (0, 0))
    return pl.pallas_call(
        _even_out_kernel,
        grid=(b, t // tm),
        in_specs=[big, half, half, half, half,
                  pl.BlockSpec((1, tm, na_dim), lambda bi, i: (bi, i, 0)),
                  pl.BlockSpec(w_out.shape, lambda bi, i: (0, 0)),
                  pl.BlockSpec((1, 1, d), lambda bi, i: (bi, 0, 0)),
                  par, par,
                  pl.BlockSpec(bd.shape, lambda bi, i: (0, 0))],
        out_specs=big,
        out_shape=jax.ShapeDtypeStruct((b, t, d), F32),
        compiler_params=_params("parallel", "parallel"),
        name="even_out",
    )(h, o_f, o_b, bonus, g, na, w_out, gate, ln_g.reshape(1, rw_dim), ln_b.reshape(1, rw_dim), bd)


def _odd_out_kernel(h_ref, of_ref, ob_ref, gate_in_ref, w_ref, gate_ref, ng_ref, o_ref):
    o = of_ref[0].astype(F32) + ob_ref[0].astype(F32)
    y = _rms(o, ng_ref[...]) * _silu(gate_in_ref[0].astype(F32))
    o_ref[0] = h_ref[0] + gate_ref[0] * _dot(y.astype(BF16), w_ref[...])


def _odd_out(h, o_f, o_b, p, gate_block, w_out, gate, norm_g, tm):
    b, t, d = h.shape
    vd = o_f.shape[2]
    big = pl.BlockSpec((1, tm, d), lambda bi, i: (bi, i, 0))
    val = pl.BlockSpec((1, tm, vd), lambda bi, i: (bi, i, 0))
    return pl.pallas_call(
        _odd_out_kernel,
        grid=(b, t // tm),
        in_specs=[big, val, val,
                  pl.BlockSpec((1, tm, vd), lambda bi, i: (bi, i, gate_block)),
                  pl.BlockSpec(w_out.shape, lambda bi, i: (0, 0)),
                  pl.BlockSpec((1, 1, d), lambda bi, i: (bi, 0, 0)),
                  pl.BlockSpec((1, vd), lambda bi, i: (0, 0))],
        out_specs=big,
        out_shape=jax.ShapeDtypeStruct((b, t, d), F32),
        compiler_params=_params("parallel", "parallel"),
        name="odd_out",
    )(h, o_f, o_b, p, w_out, gate, norm_g.reshape(1, vd))


def _hgrn_terms(qp, fp, ip, lb, reverse):
    c = qp.shape[0]
    incl, _ = _order_masks(c, reverse)
    q = qp.astype(F32)
    forget = lb + (1.0 - lb) * _sigmoid(fp.astype(F32))
    k = 1.0 - forget
    cum = _dot_sel_l(incl.astype(BF16), jnp.log2(forget))
    half, quarter = c // 2, c // 4
    row = lax.broadcasted_iota(jnp.int32, (c, 1), 0)
    if reverse:
        in_first = row >= half
        r_mid1, r_end1, r_mid2, r_last = half + quarter, half, quarter, 0
    else:
        in_first = row < half
        r_mid1, r_end1, r_mid2, r_last = quarter, half - 1, half + quarter, c - 1
    mid1, end1, mid2, tot = (cum[r:r + 1, :] for r in (r_mid1, r_end1, r_mid2, r_last))
    ref = jnp.where(in_first, mid1, mid2)
    e_ref = jnp.exp2(jnp.clip(cum - ref, -HG_EXP_CLAMP, HG_EXP_CLAMP))
    q_ref = q * e_ref
    k_ref = k / e_ref
    q_x = q_ref * jnp.exp2(mid2 - end1)
    k_x = k_ref * jnp.exp2(end1 - mid1)
    return q_ref, k_ref, q_x, k_x, q_ref * jnp.exp2(ref), k_ref * jnp.exp2(tot - ref), ip, jnp.exp2(tot)


def _hgrn_group_chunk(ops, s, reverse):
    q_ref, k_ref, q_x, k_x, q_cum, k_fin, v, wtot = ops
    c = v.shape[1]
    half = c // 2
    first = slice(half, c) if reverse else slice(0, half)
    second = slice(0, half) if reverse else slice(half, c)
    incl, _ = _order_masks(half, reverse)
    inter = _bmm_nt(q_cum, s)
    att1 = jnp.where(incl[None], _bmm_nt(q_ref[:, first], k_ref[:, first]), 0.0)
    att2 = jnp.where(incl[None], _bmm_nt(q_ref[:, second], k_ref[:, second]), 0.0)
    cross = _bmm_nt(q_x[:, second], k_x[:, first])
    o1 = inter[:, first] + _bmm(att1, v[:, first])
    o2 = inter[:, second] + _bmm(att2, v[:, second]) + _bmm(cross, v[:, first])
    o = jnp.concatenate([o2, o1] if reverse else [o1, o2], axis=1)
    return o, s * wtot + _bmm_tn(v, k_fin)


def _hgrn_scan_kernel(qf, ff, vf, qb, fb, vb, hl_ref, s0_ref, of_ref, ob_ref, sfin_ref, s_scr,
                      *, layer, heads):
    i = pl.program_id(1)

    @pl.when(i == 0)
    def _():
        for d in range(2):
            s_scr[d] = s0_ref[:, d * heads:(d + 1) * heads].reshape(s_scr.shape[1:])

    hl = hl_ref[...]
    e = jnp.exp(hl - hl.max(axis=0, keepdims=True))
    sm = e / e.sum(axis=0, keepdims=True)
    lb = jnp.zeros_like(sm[0:1])
    for j in range(1, layer + 1):
        lb = lb + sm[j:j + 1]

    nb = qf.shape[0]
    for d, (q_ref, f_ref, v_ref, o_ref) in enumerate(((qf, ff, vf, of_ref), (qb, fb, vb, ob_ref))):
        terms = [_hgrn_terms(q_ref[bi], f_ref[bi], v_ref[bi], lb, d == 1) for bi in range(nb)]
        ops = [jnp.stack([z[:, h * HG_KEY:(h + 1) * HG_KEY] for z in zs for h in range(heads)])
               for zs in zip(*terms)]
        o, s_new = _hgrn_group_chunk(ops, s_scr[d], d == 1)
        s_scr[d] = s_new
        for bi in range(nb):
            o_ref[bi] = jnp.concatenate([o[bi * heads + h] for h in range(heads)],
                                        axis=1).astype(o_ref.dtype)

    @pl.when(i == pl.num_programs(1) - 1)
    def _():
        for d in range(2):
            sfin_ref[:, d * heads:(d + 1) * heads] = s_scr[d].reshape(nb, heads, HG_KEY, HG_KEY)


def _hgrn_scan(p, hg_lower, s0, layer, heads):
    b, t, _ = p.shape
    n = t // CHUNK
    width = heads * HG_KEY
    depth = hg_lower.shape[0]

    nb = SCAN_BATCH if b % SCAN_BATCH == 0 else 1

    def fwd(blk):
        return pl.BlockSpec((nb, CHUNK, width), lambda bi, i: (bi, i, blk))

    def bwd(blk):
        return pl.BlockSpec((nb, CHUNK, width), lambda bi, i: (bi, n - 1 - i, blk))

    st = pl.BlockSpec((nb, 2 * heads, HG_KEY, HG_KEY), lambda bi, i: (bi, 0, 0, 0))
    o_sds = jax.ShapeDtypeStruct((b, t, width), BF16)
    return pl.pallas_call(
        functools.partial(_hgrn_scan_kernel, layer=layer, heads=heads),
        grid=(b // nb, n),
        in_specs=[fwd(0), fwd(1), fwd(3), bwd(0), bwd(2), bwd(3),
                  pl.BlockSpec((depth, width), lambda bi, i: (0, 0)), st],
        out_specs=[fwd(0), bwd(0), st],
        out_shape=[o_sds, o_sds, jax.ShapeDtypeStruct((b, 2 * heads, HG_KEY, HG_KEY), F32)],
        scratch_shapes=[pltpu.VMEM((2, nb * heads, HG_KEY, HG_KEY), F32)],
        compiler_params=_params("parallel", "arbitrary"),
        name="hgrn_scan",
    )(p, p, p, p, p, p, hg_lower, s0)


def _row_tile(t, target):
    return target if t % target == 0 else t


def kernel(x, c, ctx, c_ctx, norm_mix_g, norm_mlp_g, ada_w, ada_b, mlp_w1, mlp_w2, ev_w_in,
           ev_w_out, rw_mu_prev, rw_mu_next, rw_w0, rw_w_up, rw_a0, rw_a_up, rw_g_up, rw_k_k,
           rw_k_a, rw_r_k, rw_ln_g, rw_ln_b, na_rpb, od_w_in, od_w_out, hg_lower, hg_norm_g,
           final_norm_g):
    b, t, d = x.shape
    l = ctx.shape[1]
    depth = ada_w.shape[0]
    rw_dim = rw_k_k.shape[1]
    rw_heads = rw_dim // HEAD
    rw_cols = rw_mu_prev.shape[1]
    na_dim = d - rw_dim
    hg_heads = d // HG_KEY

    pad_rows = -(b + 1) % SUBLANES
    c_rows = jnp.concatenate([c, c_ctx[None], jnp.zeros((pad_rows, d), F32)], axis=0)
    mods = _ada(c_rows, ada_w, ada_b)

    def mod_vecs(layer):
        m = mods[layer]
        lat = [m[:b, j * d:(j + 1) * d].reshape(b, 1, d) for j in range(6)]
        cx = [m[b, j * d:(j + 1) * d].reshape(1, 1, d) for j in range(6)]
        return lat, cx

    eye = jnp.arange(rw_dim) // HEAD
    bd = (eye[:, None] == eye[None, :]).astype(BF16)

    def flat(z):
        return z.reshape(1, b * l, z.shape[-1])

    def per_sample(z):
        return z.reshape(b, l, z.shape[-1])

    tm_lat = _row_tile(t, 512)
    tm_out = _row_tile(t, 1024)
    tm_ctx = _row_tile(b * l, 512)
    tm_mlp_ctx = _row_tile(b * l, 1024)
    w1_all = mlp_w1.astype(BF16)
    w2_all = mlp_w2.astype(BF16)
    h_lat, h_ctx = x, flat(ctx)
    for layer in range(depth):
        last = layer == depth - 1
        (sh1, s1, g1, sh2, s2, g2), (csh1, cs1, cg1, csh2, cs2, cg2) = mod_vecs(layer)
        if layer % 2 == 0:
            e = layer // 2
            w_in = ev_w_in[e].astype(BF16)
            p_lat = _norm_mod_matmul(h_lat, norm_mix_g[layer], sh1, s1, w_in, tm_lat)
            p_ctx = per_sample(_norm_mod_matmul(h_ctx, norm_mix_g[layer], csh1, cs1, w_in, tm_ctx))
            prm = dict(
                mu_prev=rw_mu_prev[e][None], mu_next=rw_mu_next[e][None],
                w0=rw_w0[e].reshape(1, 2 * rw_dim),
                w_up=jnp.concatenate([rw_w_up[e, 0], rw_w_up[e, 1]], axis=1),
                a0=rw_a0[e].reshape(1, 2 * rw_dim),
                a_up=jnp.concatenate([rw_a_up[e, 0], rw_a_up[e, 1]], axis=1),
                g_up=rw_g_up[e], k_k=rw_k_k[e][None], k_a=rw_k_a[e][None],
                r_k=rw_r_k[e].reshape(1, rw_dim), bd=bd)
            s_zero = jnp.zeros((b, rw_heads, 2 * HEAD, 2 * HEAD), F32)
            prep_c = _rwkv_prep(p_ctx, prm, _row_tile(l, 256))
            oc_f, oc_b, s_ctx = _rwkv_scan(prep_c[:9], s_zero)
            prep_l = _rwkv_prep(p_lat, prm, tm_lat)
            ol_f, ol_b, _ = _rwkv_scan(prep_l[:9], s_ctx)
            na_lat = _natten(p_lat, p_ctx, _natten_bias(na_rpb[e]), rw_cols, na_dim)
            w_out = ev_w_out[e].astype(BF16)
            h_lat = _even_out(h_lat, ol_f, ol_b, prep_l[10], prep_l[9], na_lat, w_out, g1,
                              rw_ln_g[e], rw_ln_b[e], bd, tm_out)
            if not last:
                na_ctx = _ctx_attn(p_ctx, rw_cols, na_dim)
                h_ctx = _even_out(h_ctx, flat(oc_f), flat(oc_b), flat(prep_c[10]), flat(prep_c[9]),
                                  flat(na_ctx), w_out, cg1, rw_ln_g[e], rw_ln_b[e], bd, tm_ctx)
        else:
            o = layer // 2
            w_in = od_w_in[o].astype(BF16)
            hg_kdim = hg_heads * HG_KEY
            p_lat = _norm_mod_matmul(h_lat, norm_mix_g[layer], sh1, s1, w_in, tm_lat, hg_kdim)
            p_ctx = _norm_mod_matmul(h_ctx, norm_mix_g[layer], csh1, cs1, w_in, tm_ctx, hg_kdim)
            s_zero = jnp.zeros((b, 2 * hg_heads, HG_KEY, HG_KEY), F32)
            oc_f, oc_b, s_ctx = _hgrn_scan(per_sample(p_ctx), hg_lower, s_zero, layer, hg_heads)
            ol_f, ol_b, _ = _hgrn_scan(p_lat, hg_lower, s_ctx, layer, hg_heads)
            w_out = od_w_out[o].astype(BF16)
            gate_block = p_lat.shape[2] // d - 1
            h_lat = _odd_out(h_lat, ol_f, ol_b, p_lat, gate_block, w_out, g1, hg_norm_g[o], tm_out)
            if not last:
                h_ctx = _odd_out(h_ctx, flat(oc_f), flat(oc_b), p_ctx, gate_block, w_out, cg1,
                                 hg_norm_g[o], tm_ctx)
        h_lat = _mlp(h_lat, norm_mlp_g[layer], sh2, s2, g2, w1_all, w2_all, layer, final_norm_g,
                     _row_tile(t, 1024), 1024, last)
        if not last:
            h_ctx = _mlp(h_ctx, norm_mlp_g[layer], csh2, cs2, cg2, w1_all, w2_all, layer, final_norm_g,
                         tm_mlp_ctx, 1024, False)
    return h_lat
```

```python
import functools
import math

import jax
import jax.numpy as jnp
import numpy as np
from jax import lax
from jax.experimental import pallas as pl
from jax.experimental.pallas import tpu as pltpu

F32 = jnp.float32
BF16 = jnp.bfloat16
HIGHEST = lax.Precision.HIGHEST

NORM_EPS = 1e-6
RW_GN_EPS = 64e-5
HEAD = 64
GRID_W = 64
NA_ROWS = 8
NA_COLS = 16
HG_KEY = 128
CHUNK = 64
HG_EXP_CLAMP = 100.0
SOLVE_BASE = 8
MLP_ROW_CHUNK = 256
SCAN_BATCH = 4
MASK_BIAS = -1e30
DECAY_SCALE_LOG2 = math.exp(-0.5) / math.log(2.0)
SUBLANES = 8
HALO_ROWS = 16
VMEM_LIMIT = 52 * 1024 * 1024


def _params(*sem):
    return pltpu.CompilerParams(dimension_semantics=sem, vmem_limit_bytes=VMEM_LIMIT)


def _dot(a, b):
    return jnp.dot(a, b, preferred_element_type=F32)


def _dot_nt(a, b):
    return lax.dot_general(a, b, (((1,), (1,)), ((), ())), preferred_element_type=F32)


def _mm(a, b):
    return _dot(a.astype(BF16), b.astype(BF16))


def _bmm(a, b):
    return lax.dot_general(a.astype(BF16), b.astype(BF16), (((2,), (1,)), ((0,), (0,))),
                           preferred_element_type=F32)


def _bmm_nt(a, b):
    return lax.dot_general(a.astype(BF16), b.astype(BF16), (((2,), (2,)), ((0,), (0,))),
                           preferred_element_type=F32)


def _bmm_tn(a, b):
    return lax.dot_general(a.astype(BF16), b.astype(BF16), (((1,), (1,)), ((0,), (0,))),
                           preferred_element_type=F32)


def _hi_lo(x):
    hi = x.astype(BF16)
    return hi, (x - hi.astype(F32)).astype(BF16)


def _dot_sel_r(x, sel):
    return _dot(x.astype(BF16), sel)


def _dot_sel_l(sel, x):
    hi, lo = _hi_lo(x)
    return _dot(sel, hi) + _dot(sel, lo)


def _sigmoid(x):
    return 1.0 / (1.0 + jnp.exp(-x))


def _silu(x):
    return x * _sigmoid(x)


def _rms(x, g):
    return x * lax.rsqrt(jnp.mean(x * x, axis=-1, keepdims=True) + NORM_EPS) * g


def _order_masks(n, reverse):
    t = lax.broadcasted_iota(jnp.int32, (n, n), 0)
    i = lax.broadcasted_iota(jnp.int32, (n, n), 1)
    if reverse:
        return i >= t, i > t
    return i <= t, i < t


def _ada_kernel(c_ref, w_ref, b_ref, o_ref):
    s_hi, s_lo = _hi_lo(_silu(c_ref[...]))
    w_hi, w_lo = _hi_lo(w_ref[0])
    o_ref[0] = _dot(s_hi, w_hi) + (_dot(s_hi, w_lo) + _dot(s_lo, w_hi)) + b_ref[0]


def _ada(c_rows, ada_w, ada_b):
    depth, d, n = ada_w.shape
    rows = c_rows.shape[0]
    tn = 512
    return pl.pallas_call(
        _ada_kernel,
        grid=(depth, n // tn),
        in_specs=[pl.BlockSpec((rows, d), lambda l, j: (0, 0)),
                  pl.BlockSpec((1, d, tn), lambda l, j: (l, 0, j)),
                  pl.BlockSpec((1, 1, tn), lambda l, j: (l, 0, j))],
        out_specs=pl.BlockSpec((1, rows, tn), lambda l, j: (l, 0, j)),
        out_shape=jax.ShapeDtypeStruct((depth, rows, n), F32),
        compiler_params=_params("parallel", "parallel"),
        name="ada",
    )(c_rows, ada_w, ada_b.reshape(depth, 1, n))


def _nmm_kernel(x_ref, g_ref, sh_ref, sc_ref, w_ref, o_ref, *, silu_cols):
    tm = x_ref.shape[1]
    rc = MLP_ROW_CHUNK if tm % MLP_ROW_CHUNK == 0 else tm
    for c in range(tm // rc):
        rows = pl.ds(c * rc, rc)
        a = (_rms(x_ref[0, rows, :], g_ref[...]) * (1.0 + sc_ref[0]) + sh_ref[0]).astype(BF16)
        if silu_cols:
            o_ref[0, rows, :silu_cols] = _silu(_dot(a, w_ref[:, :silu_cols])).astype(o_ref.dtype)
            o_ref[0, rows, silu_cols:] = _dot(a, w_ref[:, silu_cols:]).astype(o_ref.dtype)
        else:
            o_ref[0, rows, :] = _dot(a, w_ref[...]).astype(o_ref.dtype)


def _norm_mod_matmul(x, g, shift, scale, w, tm, silu_cols=0):
    b, t, d = x.shape
    n = w.shape[1]
    return pl.pallas_call(
        functools.partial(_nmm_kernel, silu_cols=silu_cols),
        grid=(b, t // tm),
        in_specs=[pl.BlockSpec((1, tm, d), lambda bi, i: (bi, i, 0)),
                  pl.BlockSpec((1, d), lambda bi, i: (0, 0)),
                  pl.BlockSpec((1, 1, d), lambda bi, i: (bi, 0, 0)),
                  pl.BlockSpec((1, 1, d), lambda bi, i: (bi, 0, 0)),
                  pl.BlockSpec((d, n), lambda bi, i: (0, 0))],
        out_specs=pl.BlockSpec((1, tm, n), lambda bi, i: (bi, i, 0)),
        out_shape=jax.ShapeDtypeStruct((b, t, n), BF16),
        compiler_params=_params("parallel", "parallel"),
        name="norm_mod_matmul",
    )(x, g.reshape(1, d), shift, scale, w)


def _mlp_kernel(h_ref, g_ref, sh_ref, sc_ref, gate_ref, w1_ref, w2_ref, fg_ref, o_ref,
                a_scr, acc_scr, *, final_norm):
    j = pl.program_id(2)
    last = pl.num_programs(2) - 1
    tm = a_scr.shape[0]
    rc = MLP_ROW_CHUNK if tm % MLP_ROW_CHUNK == 0 else tm

    def hidden(a):
        hid = jnp.maximum(_dot(a, w1_ref[...]), 0.0)
        return _dot((hid * hid).astype(BF16), w2_ref[...])

    @pl.when(j == 0)
    def _():
        for c in range(tm // rc):
            rows = pl.ds(c * rc, rc)
            a = _rms(h_ref[0, rows, :], g_ref[...]) * (1.0 + sc_ref[0]) + sh_ref[0]
            a = a.astype(BF16)
            a_scr[rows, :] = a
            acc_scr[rows, :] = hidden(a)

    @pl.when(jnp.logical_and(j > 0, j < last))
    def _():
        acc_scr[...] += hidden(a_scr[...])

    @pl.when(j == last)
    def _():
        for c in range(tm // rc):
            rows = pl.ds(c * rc, rc)
            out = h_ref[0, rows, :] + gate_ref[0] * (acc_scr[rows, :] + hidden(a_scr[rows, :]))
            if final_norm:
                out = _rms(out, fg_ref[...])
            o_ref[0, rows, :] = out


def _mlp(h, g, shift, scale, gate, w1, w2, layer, final_g, tm, th, final_norm):
    b, t, d = h.shape
    hid = w1.shape[2]
    assert hid // th >= 2
    vec = pl.BlockSpec((1, 1, d), lambda bi, i, j: (bi, 0, 0))
    par = pl.BlockSpec((1, d), lambda bi, i, j: (0, 0))
    return pl.pallas_call(
        functools.partial(_mlp_kernel, final_norm=final_norm),
        grid=(b, t // tm, hid // th),
        in_specs=[pl.BlockSpec((1, tm, d), lambda bi, i, j: (bi, i, 0)),
                  par, vec, vec, vec,
                  pl.BlockSpec((None, d, th), lambda bi, i, j: (layer, 0, j)),
                  pl.BlockSpec((None, th, d), lambda bi, i, j: (layer, j, 0)),
                  par],
        out_specs=pl.BlockSpec((1, tm, d), lambda bi, i, j: (bi, i, 0)),
        out_shape=jax.ShapeDtypeStruct((b, t, d), F32),
        scratch_shapes=[pltpu.VMEM((tm, d), BF16), pltpu.VMEM((tm, d), F32)],
        compiler_params=_params("parallel", "parallel", "arbitrary"),
        name="mlp",
    )(h, g.reshape(1, d), shift, scale, gate, w1, w2, final_g.reshape(1, d))


def _rwkv_prep_kernel(p_ref, pp_ref, pn_ref, mup_ref, mun_ref, w0_ref, wup_ref, a0_ref, aup_ref,
                      gup_ref, kk_ref, ka_ref, rk_ref, bd_ref,
                      r_o, v_o, kkn_o, lw0_o, lw1_o, kd0_o, kd1_o, b0_o, b1_o, g_o, bonus_o,
                      *, rw_dim, w_lora, a_lora):
    i = pl.program_id(1)
    n = pl.num_programs(1)
    p = p_ref[0].astype(F32)
    tm = p.shape[0]
    row = lax.broadcasted_iota(jnp.int32, (SUBLANES, 1), 0)
    first = jnp.where(i > 0, pp_ref[0].astype(F32)[HALO_ROWS - 1:HALO_ROWS, :], 0.0)
    last = jnp.where(i < n - 1, pn_ref[0].astype(F32)[0:1, :], 0.0)
    prev = pltpu.roll(p, 1, axis=0)
    prev = jnp.concatenate([jnp.where(row == 0, first, prev[:SUBLANES]), prev[SUBLANES:]], axis=0)
    nxt = pltpu.roll(p, tm - 1, axis=0)
    nxt = jnp.concatenate([nxt[:tm - SUBLANES],
                           jnp.where(row == SUBLANES - 1, last, nxt[tm - SUBLANES:])], axis=0)
    ps = p + (prev - p) * mup_ref[...] + (nxt - p) * mun_ref[...]

    r = ps[:, :rw_dim]
    k = ps[:, rw_dim:2 * rw_dim]
    v = ps[:, 2 * rw_dim:3 * rw_dim]
    off = 3 * rw_dim
    w_lo = ps[:, off:off + w_lora]
    a_lo = ps[:, off + w_lora:off + w_lora + a_lora]
    g_lo = ps[:, off + w_lora + a_lora:]

    w_pre = w0_ref[...] + _mm(jnp.tanh(w_lo), wup_ref[...])
    log_decay = -DECAY_SCALE_LOG2 * _sigmoid(w_pre)
    a = _sigmoid(a0_ref[...] + _mm(a_lo, aup_ref[...]))
    g = _mm(_sigmoid(g_lo), gup_ref[...])

    bd = bd_ref[...]
    kk = k * kk_ref[...]
    kk = kk / jnp.maximum(jnp.sqrt(_dot_sel_r(kk * kk, bd)), 1e-12)
    a_f = a[:, :rw_dim]
    a_b = a[:, rw_dim:]
    ka = ka_ref[...]
    kd_f = k * (1.0 + (a_f - 1.0) * ka)
    kd_b = k * (1.0 + (a_b - 1.0) * ka)
    bonus = _dot_sel_r(r * (kd_f + kd_b) * rk_ref[...], bd) * v

    r_o[0] = r.astype(BF16)
    v_o[0] = v.astype(BF16)
    kkn_o[0] = kk.astype(BF16)
    lw0_o[0] = log_decay[:, :rw_dim]
    lw1_o[0] = log_decay[:, rw_dim:]
    kd0_o[0] = kd_f.astype(BF16)
    kd1_o[0] = kd_b.astype(BF16)
    b0_o[0] = (kk * a_f).astype(BF16)
    b1_o[0] = (kk * a_b).astype(BF16)
    g_o[0] = g.astype(BF16)
    bonus_o[0] = bonus.astype(BF16)


def _rwkv_prep(p, prm, tm):
    b, t, _ = p.shape
    rw_dim = prm["k_k"].shape[1]
    w_lora = prm["w_up"].shape[0]
    a_lora = prm["a_up"].shape[0]
    g_lora = prm["g_up"].shape[0]
    cols = 3 * rw_dim + w_lora + a_lora + g_lora
    n_halo = t // HALO_ROWS
    tm_halo = tm // HALO_ROWS

    def full(a):
        return pl.BlockSpec(a.shape, lambda bi, i: (0,) * a.ndim)

    consts = [prm["mu_prev"], prm["mu_next"], prm["w0"], prm["w_up"], prm["a0"], prm["a_up"],
              prm["g_up"], prm["k_k"], prm["k_a"], prm["r_k"], prm["bd"]]
    out_spec = pl.BlockSpec((1, tm, rw_dim), lambda bi, i: (bi, i, 0))
    out_dtypes = [BF16, BF16, BF16, F32, F32, BF16, BF16, BF16, BF16, BF16, BF16]
    return pl.pallas_call(
        functools.partial(_rwkv_prep_kernel, rw_dim=rw_dim, w_lora=w_lora, a_lora=a_lora),
        grid=(b, t // tm),
        in_specs=[pl.BlockSpec((1, tm, cols), lambda bi, i: (bi, i, 0)),
                  pl.BlockSpec((1, HALO_ROWS, cols),
                               lambda bi, i: (bi, jnp.maximum(i * tm_halo - 1, 0), 0)),
                  pl.BlockSpec((1, HALO_ROWS, cols),
                               lambda bi, i: (bi, jnp.minimum((i + 1) * tm_halo, n_halo - 1), 0))]
                 + [full(a) for a in consts],
        out_specs=[out_spec] * 11,
        out_shape=[jax.ShapeDtypeStruct((b, t, rw_dim), dt) for dt in out_dtypes],
        compiler_params=_params("parallel", "parallel"),
        name="rwkv_prep",
    )(p, p, p, *consts)


def _rwkv_chunk(kkd, rd, binv, kinv, bfin, kfin, v, s, wtot, incl2, strict2):
    c = v.shape[1]
    lanes = v.shape[2]
    la = lax.broadcasted_iota(jnp.int32, (1, 1, lanes), 2) < HEAD

    def only_a(z):
        return jnp.where(la, z, 0.0)

    def only_b(z):
        return jnp.where(la, 0.0, z)

    x = jnp.concatenate([kkd, rd], axis=1)
    gh_a = _bmm_nt(only_a(x), jnp.concatenate([binv, kinv, s], axis=1))
    gh_b = _bmm_nt(only_b(x), jnp.concatenate([kinv, binv, s], axis=1))
    a2_a = jnp.where(strict2, gh_a[:, :c, :2 * c], 0.0)
    a2_b = jnp.where(strict2, gh_b[:, :c, :2 * c], 0.0)
    v_a = only_a(v)
    v_b = only_b(v)
    akk = jnp.where(la, a2_b, a2_a)
    u = -(gh_a[:, :c, 2 * c:] + gh_b[:, :c, 2 * c:] + _bmm(akk, jnp.concatenate([v_b, v_a], axis=1)))
    def rows(y):
        return jnp.concatenate([only_a(y), only_b(y)], axis=1)

    tt = lax.broadcasted_iota(jnp.int32, (1, c, lanes), 1)
    jj = lax.broadcasted_iota(jnp.int32, (1, c, lanes), 2) % HEAD

    def same_block(m):
        return tt // m == jj // m

    nm = -jnp.where(la, a2_a, a2_b)
    nb_ = jnp.where(same_block(SOLVE_BASE), nm, 0.0)
    p = jnp.where(tt == jj, 1.0, 0.0) + nb_
    pw = _bmm(nb_, rows(nb_))
    for _ in range(SOLVE_BASE.bit_length() - 3):
        w = _bmm(pw, jnp.concatenate([rows(p), rows(pw)], axis=2))
        p = p + w[:, :, :lanes]
        pw = w[:, :, lanes:]
    p = p + _bmm(pw, rows(p))
    m = SOLVE_BASE
    while 2 * m < c:
        off = jnp.where(jnp.logical_and(same_block(2 * m), jnp.logical_not(same_block(m))), nm, 0.0)
        p = p + _bmm(_bmm(p, rows(off)), rows(p))
        m *= 2
    off = jnp.where(same_block(m), 0.0, nm)
    w = _bmm(p, jnp.concatenate([rows(u), rows(off)], axis=2))
    u = w[:, :, :lanes]
    u = u + _bmm(w[:, :, lanes:], rows(u))
    bmat = jnp.concatenate([jnp.where(incl2, gh_a[:, c:, :2 * c], 0.0),
                            jnp.where(incl2, gh_b[:, c:, :2 * c], 0.0)], axis=2)
    o = (gh_a[:, c:, 2 * c:] + gh_b[:, c:, 2 * c:]
         + _bmm(bmat, jnp.concatenate([only_a(u), v_a, v_b, only_b(u)], axis=1)))
    row = lax.broadcasted_iota(jnp.int32, s.shape, 1) // HEAD
    col = lax.broadcasted_iota(jnp.int32, s.shape, 2) // HEAD
    upd = _bmm_tn(jnp.concatenate([u, v], axis=1), jnp.concatenate([bfin, kfin], axis=1))
    s_new = s * wtot + jnp.where(row == col, upd, 0.0)
    return o, s_new


def _rwkv_decay_terms(r, v, kk, lw, kd, bm, reverse):
    c = r.shape[0]
    r, v, kk, kd, bm = (z.astype(F32) for z in (r, v, kk, kd, bm))
    incl, _ = _order_masks(c, reverse)
    cum = _dot_sel_l(incl.astype(BF16), lw)
    last = 0 if reverse else c - 1
    wtot = jnp.exp2(cum[last:last + 1, :])
    e_pos = jnp.exp2(cum)
    e_neg = 1.0 / e_pos
    e_fin = e_neg * wtot
    return (kk * jnp.exp2(cum - lw), r * e_pos, bm * e_neg, kd * e_neg, bm * e_fin,
            kd * e_fin, v, wtot)


def _rwkv_scan_kernel(rf, vf, kf, lwf, kdf, bf, rb, vb, kb, lwb, kdb, bb, s0_ref,
                      of_ref, ob_ref, sfin_ref, s_scr, *, heads):
    i = pl.program_id(1)

    nb = rf.shape[0]
    c = rf.shape[1]
    pair = 2 * HEAD

    @pl.when(i == 0)
    def _():
        s_scr[...] = s0_ref[...].reshape(s_scr.shape)

    pairs = heads // 2
    terms = [tm for bi in range(nb)
             for tm in (_rwkv_decay_terms(rf[bi], vf[bi], kf[bi], lwf[bi], kdf[bi], bf[bi], False),
                        _rwkv_decay_terms(rb[bi], vb[bi], kb[bi], lwb[bi], kdb[bi], bb[bi], True))]
    ops = [jnp.stack([z[:, p * pair:(p + 1) * pair] for z in zs for p in range(pairs)])
           for zs in zip(*terms)]
    n = nb * 2 * pairs
    t = lax.broadcasted_iota(jnp.int32, (n, c, 2 * c), 1)
    j = lax.broadcasted_iota(jnp.int32, (n, c, 2 * c), 2) % c
    rev = lax.broadcasted_iota(jnp.int32, (n, c, 2 * c), 0) // pairs % 2 == 1
    ahead = jnp.where(rev, t - j, j - t)
    o, s_new = _rwkv_chunk(*ops[:7], s_scr[...], ops[7], ahead <= 0, ahead < 0)
    s_scr[...] = s_new
    for bi in range(nb):
        base = bi * 2 * pairs
        of_ref[bi] = jnp.concatenate([o[base + p] for p in range(pairs)], axis=1).astype(of_ref.dtype)
        ob_ref[bi] = jnp.concatenate([o[base + pairs + p] for p in range(pairs)],
                                     axis=1).astype(ob_ref.dtype)

    @pl.when(i == pl.num_programs(1) - 1)
    def _():
        sfin_ref[...] = s_scr[...].reshape(sfin_ref.shape)


def _rwkv_scan(prep, s0):
    r, v, kk, lw0, lw1, kd0, kd1, b0, b1 = prep
    b, t, rw_dim = r.shape
    heads = rw_dim // HEAD
    n = t // CHUNK
    nb = SCAN_BATCH if b % SCAN_BATCH == 0 else 1
    fwd = pl.BlockSpec((nb, CHUNK, rw_dim), lambda bi, i: (bi, i, 0))
    bwd = pl.BlockSpec((nb, CHUNK, rw_dim), lambda bi, i: (bi, n - 1 - i, 0))
    st = pl.BlockSpec((nb,) + s0.shape[1:], lambda bi, i: (bi, 0, 0, 0))
    o_sds = jax.ShapeDtypeStruct((b, t, rw_dim), BF16)
    return pl.pallas_call(
        functools.partial(_rwkv_scan_kernel, heads=heads),
        grid=(b // nb, n),
        in_specs=[fwd] * 6 + [bwd] * 6 + [st],
        out_specs=[fwd, bwd, st],
        out_shape=[o_sds, o_sds, jax.ShapeDtypeStruct(s0.shape, F32)],
        scratch_shapes=[pltpu.VMEM((nb * s0.shape[1],) + s0.shape[2:], F32)],
        compiler_params=_params("parallel", "arbitrary"),
        name="rwkv_scan",
    )(r, v, kk, lw0, kd0, b0, r, v, kk, lw1, kd1, b1, s0)


def _softmax_pv(s_list, v_list):
    m = s_list[0].max(axis=-1, keepdims=True)
    for s in s_list[1:]:
        m = jnp.maximum(m, s.max(axis=-1, keepdims=True))
    num = None
    den = None
    for s, v in zip(s_list, v_list):
        p = jnp.exp(s - m)
        d = p.sum(axis=-1, keepdims=True)
        o = _dot(p.astype(BF16), v)
        num = o if num is None else num + o
        den = d if den is None else den + d
    return num / den


def _natten_kernel(q_ref, k_ref, v_ref, kc_ref, vc_ref, bias_ref, o_ref, *, rows, scale, rb):
    kc = kc_ref[0]
    vc = vc_ref[0]
    l = kc.shape[0]
    nwin = NA_ROWS * GRID_W
    head0 = lax.broadcasted_iota(jnp.int32, (1, 2 * HEAD), 1) < HEAD

    def body(it, carry):
        base = it * rb
        start = pl.multiple_of(base * GRID_W, rb * GRID_W)
        q = q_ref[0, pl.ds(start, rb * GRID_W), :] * scale
        zero = jnp.zeros_like(q)
        q0 = jnp.where(head0, q, zero)
        q1 = jnp.where(head0, zero, q)
        qs, kws, vws, bs = [], [], [], []
        for j in range(rb):
            r = base + j
            r0 = jnp.clip(r - NA_ROWS // 2, 0, rows - NA_ROWS)
            rows_j = slice(j * GRID_W, (j + 1) * GRID_W)
            qs.append(jnp.concatenate([q0[rows_j], q1[rows_j]], axis=0))
            win = pl.ds(pl.multiple_of(r0 * GRID_W, GRID_W), nwin)
            kws.append(k_ref[0, win, :])
            vws.append(v_ref[0, win, :])
            bs.append(bias_ref[0, r - r0])
        qs = jnp.stack(qs)
        s_win = _bmm_nt(qs, jnp.stack(kws)) + jnp.stack(bs)
        s_ctx = _dot_nt(qs.reshape(rb * 2 * GRID_W, 2 * HEAD), kc).reshape(rb, 2 * GRID_W, l)
        m = jnp.maximum(s_win.max(axis=-1, keepdims=True), s_ctx.max(axis=-1, keepdims=True))
        p_win = jnp.exp(s_win - m)
        p_ctx = jnp.exp(s_ctx - m)
        den = p_win.sum(axis=-1, keepdims=True) + p_ctx.sum(axis=-1, keepdims=True)
        o = _bmm(p_win, jnp.stack(vws))
        o = o + _dot(p_ctx.reshape(rb * 2 * GRID_W, l).astype(BF16), vc).reshape(rb, 2 * GRID_W, 2 * HEAD)
        o = o / den
        out = jnp.concatenate([jnp.where(head0, o[j, :GRID_W], o[j, GRID_W:]) for j in range(rb)], axis=0)
        o_ref[0, pl.ds(start, rb * GRID_W), :] = out.astype(o_ref.dtype)
        return carry

    lax.fori_loop(0, rows // rb, body, 0)


def _natten(p_lat, p_ctx, bias, rw_cols, na_dim):
    b, t, _ = p_lat.shape
    l = p_ctx.shape[1]
    pairs = na_dim // (2 * HEAD)
    qb = rw_cols // (2 * HEAD)
    kb = qb + pairs
    vb = kb + pairs
    rows = t // GRID_W

    def lat(off):
        return pl.BlockSpec((1, t, 2 * HEAD), lambda bi, hp: (bi, 0, off + hp))

    def ctx(off):
        return pl.BlockSpec((1, l, 2 * HEAD), lambda bi, hp: (bi, 0, off + hp))

    rb = 2 * NA_ROWS if rows % (2 * NA_ROWS) == 0 else 1
    return pl.pallas_call(
        functools.partial(_natten_kernel, rows=rows, scale=HEAD ** -0.5, rb=rb),
        grid=(b, pairs),
        in_specs=[lat(qb), lat(kb), lat(vb), ctx(kb), ctx(vb),
                  pl.BlockSpec((1, NA_ROWS, 2 * GRID_W, NA_ROWS * GRID_W), lambda bi, hp: (hp, 0, 0, 0))],
        out_specs=pl.BlockSpec((1, t, 2 * HEAD), lambda bi, hp: (bi, 0, hp)),
        out_shape=jax.ShapeDtypeStruct((b, t, na_dim), BF16),
        compiler_params=_params("parallel", "parallel"),
        name="natten",
    )(p_lat, p_lat, p_lat, p_ctx, p_ctx, bias)


def _ctx_attn_kernel(q_ref, k_ref, v_ref, o_ref, *, scale):
    q = q_ref[0] * scale
    k = k_ref[0]
    v = v_ref[0]
    outs = []
    for hh in range(2):
        sl = slice(hh * HEAD, (hh + 1) * HEAD)
        outs.append(_softmax_pv([_dot_nt(q[:, sl], k[:, sl])], [v[:, sl]]))
    o_ref[0] = jnp.concatenate(outs, axis=1).astype(o_ref.dtype)


def _ctx_attn(p_ctx, rw_cols, na_dim):
    b, l, _ = p_ctx.shape
    pairs = na_dim // (2 * HEAD)
    qb = rw_cols // (2 * HEAD)

    def blk(off):
        return pl.BlockSpec((1, l, 2 * HEAD), lambda bi, hp: (bi, 0, off + hp))

    return pl.pallas_call(
        functools.partial(_ctx_attn_kernel, scale=HEAD ** -0.5),
        grid=(b, pairs),
        in_specs=[blk(qb), blk(qb + pairs), blk(qb + 2 * pairs)],
        out_specs=pl.BlockSpec((1, l, 2 * HEAD), lambda bi, hp: (bi, 0, hp)),
        out_shape=jax.ShapeDtypeStruct((b, l, na_dim), BF16),
        compiler_params=_params("parallel", "parallel"),
        name="ctx_attn",
    )(p_ctx, p_ctx, p_ctx)


def _natten_bias(rpb):
    cols = np.arange(GRID_W)
    c0 = np.clip(cols - NA_COLS // 2, 0, GRID_W - NA_COLS)
    valid = (cols[None, :] >= c0[:, None]) & (cols[None, :] < c0[:, None] + NA_COLS)
    col_rel = cols[None, :] - cols[:, None] + NA_COLS - 1
    onehot = (col_rel[:, :, None] == np.arange(2 * NA_COLS - 1)).astype(np.float32)
    toe = jnp.einsum("hrj,qkj->hrqk", rpb.astype(F32), onehot, precision=HIGHEST)
    toe = jnp.where(valid[None, None], toe, MASK_BIAS)
    tab = jnp.stack([toe[:, NA_ROWS - 1 - d:2 * NA_ROWS - 1 - d] for d in range(NA_ROWS)], axis=1)
    tab = tab.transpose(0, 1, 3, 2, 4)
    heads = rpb.shape[0]
    tab = tab.reshape(heads // 2, 2, NA_ROWS, GRID_W, NA_ROWS * GRID_W)
    return tab.transpose(0, 2, 1, 3, 4).reshape(heads // 2, NA_ROWS, 2 * GRID_W, NA_ROWS * GRID_W)


def _even_out_kernel(h_ref, of_ref, ob_ref, bonus_ref, g_ref, na_ref, w_ref, gate_ref,
                     lng_ref, lnb_ref, bd_ref, o_ref):
    bd = bd_ref[...]
    o = of_ref[0].astype(F32) + ob_ref[0].astype(F32)
    inv = 1.0 / HEAD
    mu = _dot_sel_r(o, bd) * inv
    xc = o - mu
    var = _dot_sel_r(xc * xc, bd) * inv
    y = xc * lax.rsqrt(var + RW_GN_EPS) * lng_ref[...] + lnb_ref[...]
    rw = (y + bonus_ref[0].astype(F32)) * g_ref[0].astype(F32)
    cat = jnp.concatenate([rw.astype(BF16), na_ref[0]], axis=1)
    o_ref[0] = h_ref[0] + gate_ref[0] * _dot(cat, w_ref[...])


def _even_out(h, o_f, o_b, bonus, g, na, w_out, gate, ln_g, ln_b, bd, tm):
    b, t, d = h.shape
    rw_dim = o_f.shape[2]
    na_dim = na.shape[2]
    big = pl.BlockSpec((1, tm, d), lambda bi, i: (bi, i, 0))
    half = pl.BlockSpec((1, tm, rw_dim), lambda bi, i: (bi, i, 0))
    par = pl.BlockSpec((1, rw_dim), lambda bi, i: (0, 0))
    return pl.pallas_call(
        _even_out_kernel,
        grid=(b, t // tm),
        in_specs=[big, half, half, half, half,
                  pl.BlockSpec((1, tm, na_dim), lambda bi, i: (bi, i, 0)),
                  pl.BlockSpec(w_out.shape, lambda bi, i: (0, 0)),
                  pl.BlockSpec((1, 1, d), lambda bi, i: (bi, 0, 0)),
                  par, par,
                  pl.BlockSpec(bd.shape, lambda bi, i: (0, 0))],
        out_specs=big,
        out_shape=jax.ShapeDtypeStruct((b, t, d), F32),
        compiler_params=_params("parallel", "parallel"),
        name="even_out",
    )(h, o_f, o_b, bonus, g, na, w_out, gate, ln_g.reshape(1, rw_dim), ln_b.reshape(1, rw_dim), bd)


def _odd_out_kernel(h_ref, of_ref, ob_ref, gate_in_ref, w_ref, gate_ref, ng_ref, o_ref):
    o = of_ref[0].astype(F32) + ob_ref[0].astype(F32)
    y = _rms(o, ng_ref[...]) * _silu(gate_in_ref[0].astype(F32))
    o_ref[0] = h_ref[0] + gate_ref[0] * _dot(y.astype(BF16), w_ref[...])


def _odd_out(h, o_f, o_b, p, gate_block, w_out, gate, norm_g, tm):
    b, t, d = h.shape
    vd = o_f.shape[2]
    big = pl.BlockSpec((1, tm, d), lambda bi, i: (bi, i, 0))
    val = pl.BlockSpec((1, tm, vd), lambda bi, i: (bi, i, 0))
    return pl.pallas_call(
        _odd_out_kernel,
        grid=(b, t // tm),
        in_specs=[big, val, val,
                  pl.BlockSpec((1, tm, vd), lambda bi, i: (bi, i, gate_block)),
                  pl.BlockSpec(w_out.shape, lambda bi, i: (0, 0)),
                  pl.BlockSpec((1, 1, d), lambda bi, i: (bi, 0, 0)),
                  pl.BlockSpec((1, vd), lambda bi, i: (0, 0))],
        out_specs=big,
        out_shape=jax.ShapeDtypeStruct((b, t, d), F32),
        compiler_params=_params("parallel", "parallel"),
        name="odd_out",
    )(h, o_f, o_b, p, w_out, gate, norm_g.reshape(1, vd))


def _hgrn_terms(qp, fp, ip, lb, reverse):
    c = qp.shape[0]
    incl, _ = _order_masks(c, reverse)
    q = qp.astype(F32)
    forget = lb + (1.0 - lb) * _sigmoid(fp.astype(F32))
    k = 1.0 - forget
    cum = _dot_sel_l(incl.astype(BF16), jnp.log2(forget))
    half, quarter = c // 2, c // 4
    row = lax.broadcasted_iota(jnp.int32, (c, 1), 0)
    if reverse:
        in_first = row >= half
        r_mid1, r_end1, r_mid2, r_last = half + quarter, half, quarter, 0
    else:
        in_first = row < half
        r_mid1, r_end1, r_mid2, r_last = quarter, half - 1, half + quarter, c - 1
    mid1, end1, mid2, tot = (cum[r:r + 1, :] for r in (r_mid1, r_end1, r_mid2, r_last))
    ref = jnp.where(in_first, mid1, mid2)
    e_ref = jnp.exp2(jnp.clip(cum - ref, -HG_EXP_CLAMP, HG_EXP_CLAMP))
    q_ref = q * e_ref
    k_ref = k / e_ref
    q_x = q_ref * jnp.exp2(mid2 - end1)
    k_x = k_ref * jnp.exp2(end1 - mid1)
    return q_ref, k_ref, q_x, k_x, q_ref * jnp.exp2(ref), k_ref * jnp.exp2(tot - ref), ip, jnp.exp2(tot)


def _hgrn_group_chunk(ops, s, reverse):
    q_ref, k_ref, q_x, k_x, q_cum, k_fin, v, wtot = ops
    c = v.shape[1]
    half = c // 2
    first = slice(half, c) if reverse else slice(0, half)
    second = slice(0, half) if reverse else slice(half, c)
    incl, _ = _order_masks(half, reverse)
    inter = _bmm_nt(q_cum, s)
    att1 = jnp.where(incl[None], _bmm_nt(q_ref[:, first], k_ref[:, first]), 0.0)
    att2 = jnp.where(incl[None], _bmm_nt(q_ref[:, second], k_ref[:, second]), 0.0)
    cross = _bmm_nt(q_x[:, second], k_x[:, first])
    o1 = inter[:, first] + _bmm(att1, v[:, first])
    o2 = inter[:, second] + _bmm(att2, v[:, second]) + _bmm(cross, v[:, first])
    o = jnp.concatenate([o2, o1] if reverse else [o1, o2], axis=1)
    return o, s * wtot + _bmm_tn(v, k_fin)


def _hgrn_scan_kernel(qf, ff, vf, qb, fb, vb, hl_ref, s0_ref, of_ref, ob_ref, sfin_ref, s_scr,
                      *, layer, heads):
    i = pl.program_id(1)

    @pl.when(i == 0)
    def _():
        for d in range(2):
            s_scr[d] = s0_ref[:, d * heads:(d + 1) * heads].reshape(s_scr.shape[1:])

    hl = hl_ref[...]
    e = jnp.exp(hl - hl.max(axis=0, keepdims=True))
    sm = e / e.sum(axis=0, keepdims=True)
    lb = jnp.zeros_like(sm[0:1])
    for j in range(1, layer + 1):
        lb = lb + sm[j:j + 1]

    nb = qf.shape[0]
    for d, (q_ref, f_ref, v_ref, o_ref) in enumerate(((qf, ff, vf, of_ref), (qb, fb, vb, ob_ref))):
        terms = [_hgrn_terms(q_ref[bi], f_ref[bi], v_ref[bi], lb, d == 1) for bi in range(nb)]
        ops = [jnp.stack([z[:, h * HG_KEY:(h + 1) * HG_KEY] for z in zs for h in range(heads)])
               for zs in zip(*terms)]
        o, s_new = _hgrn_group_chunk(ops, s_scr[d], d == 1)
        s_scr[d] = s_new
        for bi in range(nb):
            o_ref[bi] = jnp.concatenate([o[bi * heads + h] for h in range(heads)],
                                        axis=1).astype(o_ref.dtype)

    @pl.when(i == pl.num_programs(1) - 1)
    def _():
        for d in range(2):
            sfin_ref[:, d * heads:(d + 1) * heads] = s_scr[d].reshape(nb, heads, HG_KEY, HG_KEY)


def _hgrn_scan(p, hg_lower, s0, layer, heads):
    b, t, _ = p.shape
    n = t // CHUNK
    width = heads * HG_KEY
    depth = hg_lower.shape[0]

    nb = SCAN_BATCH if b % SCAN_BATCH == 0 else 1

    def fwd(blk):
        return pl.BlockSpec((nb, CHUNK, width), lambda bi, i: (bi, i, blk))

    def bwd(blk):
        return pl.BlockSpec((nb, CHUNK, width), lambda bi, i: (bi, n - 1 - i, blk))

    st = pl.BlockSpec((nb, 2 * heads, HG_KEY, HG_KEY), lambda bi, i: (bi, 0, 0, 0))
    o_sds = jax.ShapeDtypeStruct((b, t, width), BF16)
    return pl.pallas_call(
        functools.partial(_hgrn_scan_kernel, layer=layer, heads=heads),
        grid=(b // nb, n),
        in_specs=[fwd(0), fwd(1), fwd(3), bwd(0), bwd(2), bwd(3),
                  pl.BlockSpec((depth, width), lambda bi, i: (0, 0)), st],
        out_specs=[fwd(0), bwd(0), st],
        out_shape=[o_sds, o_sds, jax.ShapeDtypeStruct((b, 2 * heads, HG_KEY, HG_KEY), F32)],
        scratch_shapes=[pltpu.VMEM((2, nb * heads, HG_KEY, HG_KEY), F32)],
        compiler_params=_params("parallel", "arbitrary"),
        name="hgrn_scan",
    )(p, p, p, p, p, p, hg_lower, s0)


def _row_tile(t, target):
    return target if t % target == 0 else t


def kernel(x, c, ctx, c_ctx, norm_mix_g, norm_mlp_g, ada_w, ada_b, mlp_w1, mlp_w2, ev_w_in,
           ev_w_out, rw_mu_prev, rw_mu_next, rw_w0, rw_w_up, rw_a0, rw_a_up, rw_g_up, rw_k_k,
           rw_k_a, rw_r_k, rw_ln_g, rw_ln_b, na_rpb, od_w_in, od_w_out, hg_lower, hg_norm_g,
           final_norm_g):
    b, t, d = x.shape
    l = ctx.shape[1]
    depth = ada_w.shape[0]
    rw_dim = rw_k_k.shape[1]
    rw_heads = rw_dim // HEAD
    rw_cols = rw_mu_prev.shape[1]
    na_dim = d - rw_dim
    hg_heads = d // HG_KEY

    pad_rows = -(b + 1) % SUBLANES
    c_rows = jnp.concatenate([c, c_ctx[None], jnp.zeros((pad_rows, d), F32)], axis=0)
    mods = _ada(c_rows, ada_w, ada_b)

    def mod_vecs(layer):
        m = mods[layer]
        lat = [m[:b, j * d:(j + 1) * d].reshape(b, 1, d) for j in range(6)]
        cx = [m[b, j * d:(j + 1) * d].reshape(1, 1, d) for j in range(6)]
        return lat, cx

    eye = jnp.arange(rw_dim) // HEAD
    bd = (eye[:, None] == eye[None, :]).astype(BF16)

    def flat(z):
        return z.reshape(1, b * l, z.shape[-1])

    def per_sample(z):
        return z.reshape(b, l, z.shape[-1])

    tm_lat = _row_tile(t, 512)
    tm_out = _row_tile(t, 1024)
    tm_ctx = _row_tile(b * l, 512)
    tm_mlp_ctx = _row_tile(b * l, 1024)
    w1_all = mlp_w1.astype(BF16)
    w2_all = mlp_w2.astype(BF16)
    h_lat, h_ctx = x, flat(ctx)
    for layer in range(depth):
        last = layer == depth - 1
        (sh1, s1, g1, sh2, s2, g2), (csh1, cs1, cg1, csh2, cs2, cg2) = mod_vecs(layer)
        if layer % 2 == 0:
            e = layer // 2
            w_in = ev_w_in[e].astype(BF16)
            p_lat = _norm_mod_matmul(h_lat, norm_mix_g[layer], sh1, s1, w_in, tm_lat)
            p_ctx = per_sample(_norm_mod_matmul(h_ctx, norm_mix_g[layer], csh1, cs1, w_in, tm_ctx))
            prm = dict(
                mu_prev=rw_mu_prev[e][None], mu_next=rw_mu_next[e][None],
                w0=rw_w0[e].reshape(1, 2 * rw_dim),
                w_up=jnp.concatenate([rw_w_up[e, 0], rw_w_up[e, 1]], axis=1),
                a0=rw_a0[e].reshape(1, 2 * rw_dim),
                a_up=jnp.concatenate([rw_a_up[e, 0], rw_a_up[e, 1]], axis=1),
                g_up=rw_g_up[e], k_k=rw_k_k[e][None], k_a=rw_k_a[e][None],
                r_k=rw_r_k[e].reshape(1, rw_dim), bd=bd)
            s_zero = jnp.zeros((b, rw_heads, 2 * HEAD, 2 * HEAD), F32)
            prep_c = _rwkv_prep(p_ctx, prm, _row_tile(l, 256))
            oc_f, oc_b, s_ctx = _rwkv_scan(prep_c[:9], s_zero)
            prep_l = _rwkv_prep(p_lat, prm, tm_lat)
            ol_f, ol_b, _ = _rwkv_scan(prep_l[:9], s_ctx)
            na_lat = _natten(p_lat, p_ctx, _natten_bias(na_rpb[e]), rw_cols, na_dim)
            w_out = ev_w_out[e].astype(BF16)
            h_lat = _even_out(h_lat, ol_f, ol_b, prep_l[10], prep_l[9], na_lat, w_out, g1,
                              rw_ln_g[e], rw_ln_b[e], bd, tm_out)
            if not last:
                na_ctx = _ctx_attn(p_ctx, rw_cols, na_dim)
                h_ctx = _even_out(h_ctx, flat(oc_f), flat(oc_b), flat(prep_c[10]), flat(prep_c[9]),
                                  flat(na_ctx), w_out, cg1, rw_ln_g[e], rw_ln_b[e], bd, tm_ctx)
        else:
            o = layer // 2
            w_in = od_w_in[o].astype(BF16)
            hg_kdim = hg_heads * HG_KEY
            p_lat = _norm_mod_matmul(h_lat, norm_mix_g[layer], sh1, s1, w_in, tm_lat, hg_kdim)
            p_ctx = _norm_mod_matmul(h_ctx, norm_mix_g[layer], csh1, cs1, w_in, tm_ctx, hg_kdim)
            s_zero = jnp.zeros((b, 2 * hg_heads, HG_KEY, HG_KEY), F32)
            oc_f, oc_b, s_ctx = _hgrn_scan(per_sample(p_ctx), hg_lower, s_zero, layer, hg_heads)
            ol_f, ol_b, _ = _hgrn_scan(p_lat, hg_lower, s_ctx, layer, hg_heads)
            w_out = od_w_out[o].astype(BF16)
            gate_block = p_lat.shape[2] // d - 1
            h_lat = _odd_out(h_lat, ol_f, ol_b, p_lat, gate_block, w_out, g1, hg_norm_g[o], tm_out)
            if not last:
                h_ctx = _odd_out(h_ctx, flat(oc_f), flat(oc_b), p_ctx, gate_block, w_out, cg1,
                                 hg_norm_g[o], tm_ctx)
        h_lat = _mlp(h_lat, norm_mlp_g[layer], sh2, s2, g2, w1_all, w2_all, layer, final_norm_g,
                     _row_tile(t, 1024), 1024, last)
        if not last:
            h_ctx = _mlp(h_ctx, norm_mlp_g[layer], csh2, cs2, cg2, w1_all, w2_all, layer, final_norm_g,
                         tm_mlp_ctx, 1024, False)
    return h_lat
```
